```python
import math
import jax, jax.numpy as jnp
from jax import lax
import numpy as np

D_MODEL = 1024
BATCH = 8
SEQ = 4096
DEPTH = 1

CHUNK = 64
N_META = 16
Q_BLOCK = 128
HEAD_DIM = 64
ROPE_DIM = HEAD_DIM // 4
ROPE_THETA = 500000.0
SB_HEADS = 8
DSA_HEADS = 8
KV_LATENT = 128
NOPE_DIM = HEAD_DIM - ROPE_DIM
IDX_HEADS = 8
IDX_DIM = 64
TOPK_MAX = 256
N_GROUPS = 4
EXP_PER_GROUP = 8
N_EXPERTS = N_GROUPS * EXP_PER_GROUP
TOP_E = 2
D_EXPERT = 512
EXPERT_BLOCK = 128
NORM_EPS = 1e-6
SB_W = SB_HEADS * HEAD_DIM
DSA_W = DSA_HEADS * HEAD_DIM
SPLITS = (SB_W, SB_W, SB_W, DSA_W, KV_LATENT, ROPE_DIM, IDX_HEADS * IDX_DIM, IDX_DIM, IDX_HEADS, D_MODEL, D_MODEL)
IN_WIDTH = 3 * SB_W + DSA_W + KV_LATENT + ROPE_DIM + IDX_HEADS * IDX_DIM + IDX_DIM + IDX_HEADS + 2 * D_MODEL

kernel_name = "hybrid_sb_dsa_hmoe_meta"


def rmsnorm(x, g):
    xf = x.astype(jnp.float32)
    y = xf * lax.rsqrt(jnp.mean(xf * xf, axis=-1, keepdims=True) + NORM_EPS)
    return (y * g.astype(jnp.float32)).astype(x.dtype)


def partial_rope(x, pos):
    half = ROPE_DIM // 2
    inv = ROPE_THETA ** (-jnp.arange(half, dtype=jnp.float32) / half)
    ang = pos.astype(jnp.float32)[:, None] * inv[None, :]
    cos = jnp.cos(ang)[None, :, None, :]
    sin = jnp.sin(ang)[None, :, None, :]
    x1 = x[..., :half].astype(jnp.float32)
    x2 = x[..., half:ROPE_DIM].astype(jnp.float32)
    rot = jnp.concatenate([x1 * cos - x2 * sin, x2 * cos + x1 * sin], axis=-1).astype(x.dtype)
    return jnp.concatenate([rot, x[..., ROPE_DIM:]], axis=-1)


def chunk_ids(pos):
    return jnp.where(pos < N_META, 0, 1 + (pos - N_META) // CHUNK)


def to_blocks(a, t_pad):
    b, t = a.shape[0], a.shape[1]
    pad = [(0, 0), (0, t_pad - t)] + [(0, 0)] * (a.ndim - 2)
    a = jnp.pad(a, pad).reshape((b, t_pad // Q_BLOCK, Q_BLOCK) + a.shape[2:])
    return jnp.moveaxis(a, 1, 0)


def from_blocks(a, t):
    a = jnp.moveaxis(a, 0, 1)
    return a.reshape((a.shape[0], a.shape[1] * a.shape[2]) + a.shape[3:])[:, :t]


def stick_breaking(q, k, v, pos):
    t = q.shape[1]
    t_pad = -(-t // Q_BLOCK) * Q_BLOCK
    scale = 1.0 / math.sqrt(HEAD_DIM)
    qpos = jnp.arange(t_pad).reshape(t_pad // Q_BLOCK, Q_BLOCK)

    def block(args):
        qb, qp = args
        z = jnp.einsum('bqhd,bkhd->bhqk', qb, k).astype(jnp.float32) * scale
        strict = pos[None, :] < qp[:, None]
        log_keep = jnp.where(strict, jax.nn.log_sigmoid(-z), 0.0)
        later = lax.cumsum(log_keep, axis=3, reverse=True) - log_keep
        a = jnp.where(strict, jnp.exp(jax.nn.log_sigmoid(z) + later), 0.0)
        return jnp.einsum('bhqk,bkhd->bqhd', a.astype(v.dtype), v)

    out = lax.map(block, (to_blocks(q, t_pad), qpos))
    return from_blocks(out, t)


def dsa_attention(q_full, kv_all, q_idx, w_idx, k_idx, pos, topk):
    t = q_full.shape[1]
    t_pad = -(-t // Q_BLOCK) * Q_BLOCK
    scale = 1.0 / math.sqrt(HEAD_DIM)
    cid = chunk_ids(pos)
    qcid = chunk_ids(jnp.arange(t_pad)).reshape(t_pad // Q_BLOCK, Q_BLOCK)

    def block(args):
        qf, qi, wi, qc = args
        s_idx = jax.nn.relu(jnp.einsum('bqhd,bkd->bqhk', qi, k_idx))
        score = jnp.einsum('bqh,bqhk->bqk', wi, s_idx).astype(jnp.float32)
        adm = cid[None, :] <= qc[:, None]
        score = jnp.where(adm[None], score, -jnp.inf)
        _, sel = lax.top_k(score, topk)
        valid = cid[sel] <= qc[None, :, None]
        kv_sel = jax.vmap(lambda kv, i: kv[i])(kv_all, sel)
        logits = jnp.einsum('bqhc,bqkc->bqhk', qf, kv_sel).astype(jnp.float32) * scale
        logits = jnp.where(valid[:, :, None, :], logits, -1e30)
        p = jax.nn.softmax(logits, axis=-1)
        return jnp.einsum('bqhk,bqkc->bqhc', p.astype(kv_sel.dtype), kv_sel[..., ROPE_DIM:])

    out = lax.map(block, (to_blocks(q_full, t_pad), to_blocks(q_idx, t_pad), to_blocks(w_idx, t_pad), qcid))
    return from_blocks(out, t)


def mixer(u, w_in, w_uk, w_uv, w_up_a, w_up_b, w_o, pos, topk):
    b, t, _ = u.shape
    proj = u @ w_in
    qa, ka, va, qb, ckv, krope, qi, ki, wi, ga, gb = jnp.split(proj, list(np.cumsum(SPLITS)[:-1]), axis=-1)
    y_a = stick_breaking(qa.reshape(b, t, SB_HEADS, HEAD_DIM), ka.reshape(b, t, SB_HEADS, HEAD_DIM),
                         va.reshape(b, t, SB_HEADS, HEAD_DIM), pos).reshape(b, t, SB_W)
    qb = partial_rope(qb.reshape(b, t, DSA_HEADS, HEAD_DIM), pos)
    q_lat = jnp.einsum('bthn,hcn->bthc', qb[..., ROPE_DIM:], w_uk)
    q_full = jnp.concatenate([qb[..., :ROPE_DIM], q_lat], axis=-1)
    krope = partial_rope(krope[:, :, None, :], pos)[:, :, 0]
    kv_all = jnp.concatenate([krope, ckv], axis=-1)
    qi = partial_rope(qi.reshape(b, t, IDX_HEADS, IDX_DIM), pos)
    ki = partial_rope(ki[:, :, None, :], pos)[:, :, 0]
    o_lat = dsa_attention(q_full, kv_all, qi, wi, ki, pos, topk)
    y_b = jnp.einsum('bthc,hcd->bthd', o_lat, w_uv).reshape(b, t, DSA_W)
    z = jax.nn.sigmoid(ga) * (y_a @ w_up_a) + jax.nn.sigmoid(gb) * (y_b @ w_up_b)
    return z @ w_o


def hier_moe(u, w_group, b_group, w_router, b_router, w1, w3, w2):
    n, d = u.shape
    g_logits = (u @ w_group).astype(jnp.float32) + b_group.astype(jnp.float32)
    g_prob = jax.nn.softmax(g_logits, axis=-1)
    p_grp, g_sel = lax.top_k(g_prob, 1)
    e_logits = ((u @ w_router).astype(jnp.float32) + b_router.astype(jnp.float32)).reshape(n, N_GROUPS, EXP_PER_GROUP)
    gidx = jnp.broadcast_to(g_sel[:, :, None], (n, 1, EXP_PER_GROUP))
    in_grp = jnp.take_along_axis(e_logits, gidx, axis=1)[:, 0]
    e_prob = jax.nn.softmax(in_grp, axis=-1)
    p_exp, e_sel = lax.top_k(e_prob, TOP_E)
    gate = p_grp * p_exp / jnp.sum(p_exp, axis=-1, keepdims=True)
    expert = g_sel * EXP_PER_GROUP + e_sel
    m = n * TOP_E
    e_flat = expert.reshape(m).astype(jnp.int32)
    tok = jnp.repeat(jnp.arange(n, dtype=jnp.int32), TOP_E)
    w_flat = gate.reshape(m)
    order = jnp.argsort(e_flat)
    e_s = e_flat[order]
    counts = jnp.zeros((N_EXPERTS,), jnp.int32).at[e_flat].add(1)
    padded = (counts + EXPERT_BLOCK - 1) // EXPERT_BLOCK * EXPERT_BLOCK
    p_end = jnp.cumsum(padded)
    p_start = p_end - padded
    start = jnp.cumsum(counts) - counts
    dest = p_start[e_s] + jnp.arange(m, dtype=jnp.int32) - start[e_s]
    n_blk = -(-m // EXPERT_BLOCK) + N_EXPERTS
    p_tot = n_blk * EXPERT_BLOCK
    buf_tok = jnp.full((p_tot,), n, jnp.int32).at[dest].set(tok[order])
    buf_w = jnp.zeros((p_tot,), jnp.float32).at[dest].set(w_flat[order])
    blk_e = jnp.minimum(jnp.searchsorted(p_end, jnp.arange(n_blk, dtype=jnp.int32) * EXPERT_BLOCK, side='right'), N_EXPERTS - 1)
    u_pad = jnp.concatenate([u, jnp.zeros((1, d), u.dtype)], axis=0)
    xb = u_pad[buf_tok].reshape(n_blk, EXPERT_BLOCK, d)

    def expert_block(args):
        xe, e = args
        hid = jax.nn.silu(xe @ w1[e]) * (xe @ w3[e])
        return hid @ w2[e]

    yb = lax.map(expert_block, (xb, blk_e)).reshape(p_tot, d)
    out = jnp.zeros((n + 1, d), u.dtype).at[buf_tok].add(yb * buf_w[:, None].astype(yb.dtype))
    return out[:n]


def setup_inputs(seed: int = 0) -> dict:
    key = jax.random.key(seed)
    ks = jax.random.split(key, 20)
    f32 = jnp.float32
    nrm = lambda k, shape, s: jax.random.normal(k, shape, f32) * s
    return {
        "x": nrm(ks[0], (BATCH, SEQ, D_MODEL), 1.0),
        "meta_tokens": nrm(ks[1], (N_META, D_MODEL), 1.0),
        "norm_mix_g": 1.0 + nrm(ks[2], (DEPTH, D_MODEL), 0.02),
        "w_in": nrm(ks[3], (DEPTH, D_MODEL, IN_WIDTH), D_MODEL ** -0.5),
        "w_uk": nrm(ks[4], (DEPTH, DSA_HEADS, KV_LATENT, NOPE_DIM), KV_LATENT ** -0.5),
        "w_uv": nrm(ks[5], (DEPTH, DSA_HEADS, KV_LATENT, HEAD_DIM), KV_LATENT ** -0.5),
        "w_up_a": nrm(ks[6], (DEPTH, SB_W, D_MODEL), SB_W ** -0.5),
        "w_up_b": nrm(ks[7], (DEPTH, DSA_W, D_MODEL), DSA_W ** -0.5),
        "w_o": nrm(ks[8], (DEPTH, D_MODEL, D_MODEL), D_MODEL ** -0.5),
        "norm_ffn_g": 1.0 + nrm(ks[9], (DEPTH, D_MODEL), 0.02),
        "w_group": nrm(ks[10], (DEPTH, D_MODEL, N_GROUPS), D_MODEL ** -0.5),
        "b_group": nrm(ks[11], (DEPTH, N_GROUPS), 0.01),
        "w_router": nrm(ks[12], (DEPTH, D_MODEL, N_EXPERTS), D_MODEL ** -0.5),
        "b_router": nrm(ks[13], (DEPTH, N_EXPERTS), 0.01),
        "w1": nrm(ks[14], (DEPTH, N_EXPERTS, D_MODEL, D_EXPERT), D_MODEL ** -0.5),
        "w3": nrm(ks[15], (DEPTH, N_EXPERTS, D_MODEL, D_EXPERT), D_MODEL ** -0.5),
        "w2": nrm(ks[16], (DEPTH, N_EXPERTS, D_EXPERT, D_MODEL), D_EXPERT ** -0.5),
        "norm_final_g": 1.0 + nrm(ks[17], (D_MODEL,), 0.02),
    }


def reference(x, meta_tokens, norm_mix_g, w_in, w_uk, w_uv, w_up_a, w_up_b, w_o, norm_ffn_g,
              w_group, b_group, w_router, b_router, w1, w3, w2, norm_final_g):
    b, s, d = x.shape
    topk = min(TOPK_MAX, s // 4)
    meta = jnp.broadcast_to(meta_tokens[None].astype(x.dtype), (b, N_META, d))
    h = jnp.concatenate([meta, x], axis=1)
    t = s + N_META
    pos = jnp.arange(t)
    for l in range(DEPTH):
        h = h + mixer(rmsnorm(h, norm_mix_g[l]), w_in[l], w_uk[l], w_uv[l], w_up_a[l], w_up_b[l], w_o[l], pos, topk)
        u = rmsnorm(h, norm_ffn_g[l]).reshape(b * t, d)
        h = h + hier_moe(u, w_group[l], b_group[l], w_router[l], b_router[l], w1[l], w3[l], w2[l]).reshape(b, t, d)
    return rmsnorm(h, norm_final_g)[:, N_META:]
```

```python
import functools
import math

import numpy as np
import jax
import jax.numpy as jnp
from jax import lax
from jax.experimental import pallas as pl
from jax.experimental.pallas import tpu as pltpu

D_MODEL = 1024
CHUNK = 64
N_META = 16
HEAD_DIM = 64
ROPE_DIM = 16
ROPE_THETA = 500000.0
N_HEADS = 8
KV_LATENT = 128
IDX_DIM = 64
TOPK_MAX = 256
N_GROUPS = 4
EXP_PER_GROUP = 8
N_EXPERTS = 32
D_EXPERT = 512
NORM_EPS = 1e-6
HEADS_W = N_HEADS * HEAD_DIM
SPLITS = (HEADS_W, HEADS_W, HEADS_W, HEADS_W, KV_LATENT, ROPE_DIM, N_HEADS * IDX_DIM, IDX_DIM, N_HEADS,
          D_MODEL, D_MODEL)

LANES = 128
QF_W = 2 * LANES
INT_MIN = -2 ** 31
EXP_UNDERFLOW = -104.0
ATTN_SCALE = 1.0 / math.sqrt(HEAD_DIM)
VMEM_LIMIT = 56 * 1024 * 1024
FFN_TILE = 256
MERGE_TILE = 512
FINAL_TILE = 256
DISPATCH_TILE = 512

BF16 = jnp.bfloat16
F32 = jnp.float32

_C_QKV = (0, 3 * HEADS_W)
_C_QB = (_C_QKV[1], _C_QKV[1] + HEADS_W)
_C_QI = (_C_QB[1], _C_QB[1] + HEADS_W)
_C_KI = (_C_QI[1], _C_QI[1] + LANES)
_C_KV = (_C_KI[1], _C_KI[1] + QF_W)
_C_G = (_C_KV[1], _C_KV[1] + 2 * D_MODEL)
PROJ_W = _C_G[1]


def _dot(a, b):
    return jnp.dot(a, b, preferred_element_type=F32)


def _dot_t(a, b):
    return lax.dot_general(a, b, (((1,), (1,)), ((), ())), preferred_element_type=F32)


def _rmsnorm(x, g):
    return x * lax.rsqrt(jnp.mean(x * x, axis=-1, keepdims=True) + NORM_EPS) * g


def _rope(x, cos, sa, sb):
    w = x.shape[1]
    n = w // LANES
    if n > 1:
        cos, sa, sb = (jnp.concatenate([t] * n, axis=1) for t in (cos, sa, sb))
    return x * cos + pltpu.roll(x, 8, 1) * sa + pltpu.roll(x, w - 8, 1) * sb


def _proj_kernel(x_ref, g_ref, cos_ref, sa_ref, sb_ref, w_ref, wabs_ref, wwi_ref, wckv_ref,
                 qkv_ref, qf_ref, qi_ref, ki_ref, kv_ref, gt_ref, wit_ref, ckvt_ref):
    ub = _rmsnorm(x_ref[...], g_ref[...]).astype(BF16)
    cos, sa, sb = cos_ref[...], sa_ref[...], sb_ref[...]
    qkv_ref[...] = _dot(ub, w_ref[:, _C_QKV[0]:_C_QKV[1]]).astype(BF16)
    qb = _rope(_dot(ub, w_ref[:, _C_QB[0]:_C_QB[1]]), cos, sa, sb).astype(BF16)
    qf_ref[...] = _dot(qb, wabs_ref[...]).astype(BF16)
    qi_ref[...] = _rope(_dot(ub, w_ref[:, _C_QI[0]:_C_QI[1]]), cos, sa, sb).astype(BF16)
    ki_ref[...] = _rope(_dot(ub, w_ref[:, _C_KI[0]:_C_KI[1]]), cos, sa, sb).astype(BF16)
    kv = _dot(ub, w_ref[:, _C_KV[0]:_C_KV[1]])
    kv_ref[...] = jnp.concatenate([_rope(kv[:, :LANES], cos, sa, sb), kv[:, LANES:]], axis=1).astype(BF16)
    gt_ref[...] = jax.nn.sigmoid(_dot(ub, w_ref[:, _C_G[0]:_C_G[1]])).astype(BF16)
    wit_ref[...] = _dot_t(wwi_ref[...], ub)
    ckvt_ref[...] = _dot_t(wckv_ref[...], ub).astype(BF16)


def _proj(x2, g, tabs, w, wabs, wwi, wckv, tile, tab_tiles):
    n = x2.shape[0]
    row = lambda width: pl.BlockSpec((tile, width), lambda i: (i, 0))
    const = lambda shape: pl.BlockSpec(shape, lambda i: (0, 0))
    tab = pl.BlockSpec((tile, LANES), lambda i: (i % tab_tiles, 0))
    out_shape = (
        jax.ShapeDtypeStruct((n, 3 * HEADS_W), BF16),
        jax.ShapeDtypeStruct((n, N_HEADS * QF_W), BF16),
        jax.ShapeDtypeStruct((n, HEADS_W), BF16),
        jax.ShapeDtypeStruct((n, LANES), BF16),
        jax.ShapeDtypeStruct((n, QF_W), BF16),
        jax.ShapeDtypeStruct((n, 2 * D_MODEL), BF16),
        jax.ShapeDtypeStruct((16, n), F32),
        jax.ShapeDtypeStruct((KV_LATENT, n), BF16),
    )
    out_specs = (row(3 * HEADS_W), row(N_HEADS * QF_W), row(HEADS_W), row(LANES), row(QF_W), row(2 * D_MODEL),
                 pl.BlockSpec((16, tile), lambda i: (0, i)), pl.BlockSpec((KV_LATENT, tile), lambda i: (0, i)))
    return pl.pallas_call(
        _proj_kernel,
        grid=(n // tile,),
        in_specs=[row(D_MODEL), const((1, D_MODEL)), tab, tab, tab, const(w.shape), const(wabs.shape),
                  const(wwi.shape), const(wckv.shape)],
        out_specs=out_specs,
        out_shape=out_shape,
        compiler_params=pltpu.CompilerParams(dimension_semantics=("arbitrary",), vmem_limit_bytes=VMEM_LIMIT),
        name="proj",
    )(x2, g, *tabs, w, wabs, wwi, wckv)


def _sb_kernel(q_ref, k_ref, v_ref, su_ref, o_ref):
    ib = pl.program_id(2) + 1
    q2 = q_ref[...]
    lane = lax.broadcasted_iota(jnp.int32, (LANES, LANES), 1)
    row = lax.broadcasted_iota(jnp.int32, (LANES, LANES), 0)
    low = lane < HEAD_DIM
    zero = jnp.zeros_like(q2)
    q_heads = (jnp.where(low, q2, zero), jnp.where(low, zero, q2))
    su = su_ref[...]
    qpos = ib * LANES + row

    def body(st):
        j, accs, carries, _ = st
        start = pl.multiple_of(j * LANES, LANES)
        kb = k_ref[pl.ds(start, LANES), :]
        vb = v_ref[pl.ds(start, LANES), :]
        kpos = j * LANES + lane
        mask = (kpos < qpos) & ((kpos < N_META) | (kpos >= LANES))
        new_accs, new_carries = [], []
        for qh, acc, carry in zip(q_heads, accs, carries):
            z = _dot_t(qh, kb) * ATTN_SCALE
            ls_pos = jnp.minimum(z, 0.0) - jnp.log1p(jnp.exp(-jnp.abs(z)))
            log_keep = jnp.where(mask, ls_pos - z, 0.0)
            hi = log_keep.astype(BF16)
            lo = (log_keep - hi.astype(F32)).astype(BF16)
            cs = _dot(hi, su) + _dot(lo, su)
            a = jnp.where(mask, jnp.exp(ls_pos + cs[:, :LANES] + carry), 0.0)
            new_accs.append(acc + _dot(a.astype(BF16), vb))
            new_carries.append(carry + cs[:, LANES:])
        top = jnp.maximum(jnp.max(new_carries[0]), jnp.max(new_carries[1]))
        return j - 1, tuple(new_accs), tuple(new_carries), top

    def cond(st):
        return (st[0] >= 0) & (st[3] > EXP_UNDERFLOW)

    zeros = jnp.zeros((LANES, LANES), F32)
    _, accs, _, _ = lax.while_loop(cond, body, (ib, (zeros, zeros), (zeros, zeros), jnp.float32(0.0)))
    o_ref[...] = jnp.where(low, accs[0], accs[1]).astype(BF16)


def _sb(qkv_x, kvp, su, batch, seq):
    nq = seq // LANES
    tp = seq + LANES
    return pl.pallas_call(
        _sb_kernel,
        grid=(batch, N_HEADS // 2, nq),
        in_specs=[pl.BlockSpec((LANES, LANES), lambda b, p, i: (b * nq + i, p)),
                  pl.BlockSpec((tp, LANES), lambda b, p, i: (b, p)),
                  pl.BlockSpec((tp, LANES), lambda b, p, i: (b, N_HEADS // 2 + p)),
                  pl.BlockSpec(su.shape, lambda b, p, i: (0, 0))],
        out_specs=pl.BlockSpec((LANES, LANES), lambda b, p, i: (b * nq + i, p)),
        out_shape=jax.ShapeDtypeStruct((batch * seq, HEADS_W), BF16),
        compiler_params=pltpu.CompilerParams(dimension_semantics=("arbitrary",) * 3, vmem_limit_bytes=VMEM_LIMIT),
        name="sb",
    )(qkv_x, kvp, kvp, su)


def _dsa_kernel(qi_ref, wit_ref, qf_ref, ki_ref, kv_ref, ckvt_ref, lt_ref, wuvt_ref, o_ref,
                sk_ref, acc_ref, m_ref, l_ref, qim_ref, ybt_ref, *, ktop):
    ib = pl.program_id(1) + 1
    nt = ib + 1
    lane = lax.broadcasted_iota(jnp.int32, (LANES, LANES), 1)
    row = lax.broadcasted_iota(jnp.int32, (LANES, LANES), 0)
    low = lane < HEAD_DIM
    for h in range(N_HEADS):
        blk = qi_ref[:, (h // 2) * LANES:(h // 2 + 1) * LANES]
        keep = low if h % 2 == 0 else jnp.logical_not(low)
        qim_ref[h] = jnp.where(keep, blk, jnp.zeros_like(blk))
    wit = wit_ref[...]
    cq = 1 + ((ib * LANES + lane - LANES) >> 6)

    def tile(t):
        return pl.ds(pl.multiple_of(t * LANES, LANES), LANES)

    def scores(t, c):
        kb = ki_ref[tile(t), :]
        s = jnp.zeros((LANES, LANES), F32)
        for h in range(N_HEADS):
            s = s + wit[h:h + 1, :] * jnp.maximum(_dot_t(kb, qim_ref[h]), 0.0)
        s = jnp.where(s == 0.0, 0.0, s)
        bits = lax.bitcast_convert_type(s, jnp.int32)
        key = bits ^ ((bits >> 31) & 0x7FFFFFFF)
        rk = t * LANES + row
        ck = jnp.where(rk < LANES, 0, 1 + ((rk - LANES) >> 6))
        adm = ((rk < N_META) | (rk >= LANES)) & (ck <= cq)
        sk_ref[tile(t), :] = jnp.where(adm, key, INT_MIN)
        return c

    lax.fori_loop(0, nt, scores, 0)

    def count(pred):
        def step(t, acc):
            return acc + pred(sk_ref[tile(t), :]).astype(jnp.int32)
        acc = lax.fori_loop(0, nt, step, jnp.zeros((LANES, LANES), jnp.int32))
        return jnp.sum(acc, axis=0, keepdims=True)

    zero_row = jnp.zeros((1, LANES), jnp.int32)
    thr = jnp.where(count(lambda k: k >= zero_row) >= ktop, zero_row, jnp.full((1, LANES), INT_MIN, jnp.int32))

    def bit_step(it, thr):
        cand = thr | jnp.left_shift(jnp.int32(1), 30 - it)
        return jnp.where(count(lambda k: k >= cand) >= ktop, cand, thr)

    thr = lax.fori_loop(0, 31, bit_step, thr)
    need = (ktop - count(lambda k: k > thr)).astype(F32)

    acc_ref[...] = jnp.zeros_like(acc_ref)
    m_ref[...] = jnp.full_like(m_ref, -1e30)
    l_ref[...] = jnp.zeros_like(l_ref)
    lt = lt_ref[...]

    def attend(t, tie_carry):
        skt = sk_ref[tile(t), :]
        tie = (skt == thr) & (skt > INT_MIN)
        rank = _dot(lt, tie.astype(BF16)) + tie_carry
        sel = (skt > thr) | (tie & (rank <= need))
        kvb = kv_ref[tile(t), :]
        cb = ckvt_ref[t]
        for h in range(N_HEADS):
            lg = _dot_t(kvb, qf_ref[:, h * QF_W:(h + 1) * QF_W]) * ATTN_SCALE
            s = jnp.where(sel, lg, -1e30)
            m_old = m_ref[h:h + 1, :]
            m_new = jnp.maximum(m_old, jnp.max(s, axis=0, keepdims=True))
            alpha = jnp.exp(m_old - m_new)
            p = jnp.where(sel, jnp.exp(s - m_new), 0.0)
            l_ref[h:h + 1, :] = l_ref[h:h + 1, :] * alpha + jnp.sum(p, axis=0, keepdims=True)
            acc_ref[h] = acc_ref[h] * alpha + _dot(cb, p.astype(BF16))
            m_ref[h:h + 1, :] = m_new
        return rank[LANES - 1:LANES, :]

    lax.fori_loop(0, nt, attend, jnp.zeros((1, LANES), F32))

    for h in range(N_HEADS):
        o = (acc_ref[h] / l_ref[h:h + 1, :]).astype(BF16)
        ybt_ref[h * HEAD_DIM:(h + 1) * HEAD_DIM, :] = _dot(wuvt_ref[h], o)
    o_ref[...] = ybt_ref[...].T.astype(BF16)


def _dsa(qi, wit, qf, kip, kvp, ckvt3, lt, wuvt, batch, seq, ktop):
    nq = seq // LANES
    nb = nq + 1
    tp = seq + LANES
    return pl.pallas_call(
        functools.partial(_dsa_kernel, ktop=ktop),
        grid=(batch, nq),
        in_specs=[pl.BlockSpec((LANES, HEADS_W), lambda b, i: (b * nq + i, 0)),
                  pl.BlockSpec((16, LANES), lambda b, i: (0, b * nq + i)),
                  pl.BlockSpec((LANES, N_HEADS * QF_W), lambda b, i: (b * nq + i, 0)),
                  pl.BlockSpec((tp, LANES), lambda b, i: (b, 0)),
                  pl.BlockSpec((tp, QF_W), lambda b, i: (b, 0)),
                  pl.BlockSpec((nb, KV_LATENT, LANES), lambda b, i: (b, 0, 0)),
                  pl.BlockSpec(lt.shape, lambda b, i: (0, 0)),
                  pl.BlockSpec(wuvt.shape, lambda b, i: (0, 0, 0))],
        out_specs=pl.BlockSpec((LANES, HEADS_W), lambda b, i: (b * nq + i, 0)),
        out_shape=jax.ShapeDtypeStruct((batch * seq, HEADS_W), BF16),
        scratch_shapes=[pltpu.VMEM((tp, LANES), jnp.int32),
                        pltpu.VMEM((N_HEADS, KV_LATENT, LANES), F32),
                        pltpu.VMEM((N_HEADS, LANES), F32),
                        pltpu.VMEM((N_HEADS, LANES), F32),
                        pltpu.VMEM((N_HEADS, LANES, LANES), BF16),
                        pltpu.VMEM((HEADS_W, LANES), F32)],
        compiler_params=pltpu.CompilerParams(dimension_semantics=("arbitrary",) * 2, vmem_limit_bytes=VMEM_LIMIT),
        name="dsa",
    )(qi, wit, qf, kip, kvp, ckvt3, lt, wuvt)


def _merge_kernel(x_ref, ya_ref, yb_ref, gt_ref, wua_ref, wub_ref, wo_ref, g_ref, wrt_ref, brt_ref, slt_ref,
                  h1_ref, u2_ref, r_ref, cnt_ref, carry_ref):
    @pl.when(pl.program_id(0) == 0)
    def _():
        carry_ref[...] = jnp.zeros_like(carry_ref)

    gt = gt_ref[...].astype(F32)
    z = gt[:, :D_MODEL] * _dot(ya_ref[...], wua_ref[...]) + gt[:, D_MODEL:] * _dot(yb_ref[...], wub_ref[...])
    h1 = x_ref[...] + _dot(z.astype(BF16), wo_ref[...])
    h1_ref[...] = h1
    u2 = _rmsnorm(h1, g_ref[...])
    u2_ref[...] = u2
    lg = jnp.dot(u2, wrt_ref[...], preferred_element_type=F32, precision=lax.Precision.HIGHEST) + brt_ref[...]

    tm = lg.shape[0]
    lane = lax.broadcasted_iota(jnp.int32, (tm, LANES), 1)
    big = jnp.int32(LANES)

    def softmax(mask):
        x = jnp.where(mask, lg, -jnp.inf)
        e = jnp.where(mask, jnp.exp(x - jnp.max(x, axis=-1, keepdims=True)), 0.0)
        return e / jnp.sum(e, axis=-1, keepdims=True)

    def top1(p, mask):
        best = jnp.max(jnp.where(mask, p, -1.0), axis=-1, keepdims=True)
        idx = jnp.min(jnp.where(mask & (p == best), lane, big), axis=-1, keepdims=True)
        return best, idx

    gmask = (lane >= N_EXPERTS) & (lane < N_EXPERTS + N_GROUPS)
    p_grp, g_lane = top1(softmax(gmask), gmask)
    first = (g_lane - N_EXPERTS) * EXP_PER_GROUP
    emask = (lane >= first) & (lane < first + EXP_PER_GROUP)
    ep = softmax(emask)
    p0, e0 = top1(ep, emask)
    rest = emask & (lane != e0)
    p1, e1 = top1(ep, rest)
    den = p0 + p1
    w0 = p_grp * p0 / den
    w1 = p_grp * p1 / den

    oh0 = lane == e0
    oh1 = lane == e1
    both = (oh0 | oh1).astype(BF16)
    before = _dot(slt_ref[...], both) + carry_ref[...]
    rank0 = jnp.sum(jnp.where(oh0, before, 0.0), axis=-1, keepdims=True)
    rank1 = jnp.sum(jnp.where(oh1, before, 0.0), axis=-1, keepdims=True)
    carry = carry_ref[...] + jnp.sum(both.astype(F32), axis=0, keepdims=True)
    carry_ref[...] = carry
    cnt_ref[...] = carry

    cols = (e0.astype(F32), e1.astype(F32), w0, w1, rank0, rank1)
    r = jnp.zeros((tm, LANES), F32)
    for c, v in enumerate(cols):
        r = jnp.where(lane == c, v, r)
    r_ref[...] = r


def _merge(x2, ya, yb, gt, wua, wub, wo, g, wrt, brt, slt):
    n = x2.shape[0]
    tm = MERGE_TILE
    row = lambda width: pl.BlockSpec((tm, width), lambda i: (i, 0))
    const = lambda a: pl.BlockSpec(a.shape, lambda i: (0,) * a.ndim)
    return pl.pallas_call(
        _merge_kernel,
        grid=(n // tm,),
        in_specs=[row(D_MODEL), row(HEADS_W), row(HEADS_W), row(2 * D_MODEL), const(wua), const(wub), const(wo),
                  const(g), const(wrt), const(brt), const(slt)],
        out_specs=(row(D_MODEL), row(D_MODEL), row(LANES), pl.BlockSpec((1, LANES), lambda i: (0, 0))),
        out_shape=(jax.ShapeDtypeStruct((n, D_MODEL), F32), jax.ShapeDtypeStruct((n, D_MODEL), F32),
                   jax.ShapeDtypeStruct((n, LANES), F32), jax.ShapeDtypeStruct((1, LANES), F32)),
        scratch_shapes=[pltpu.VMEM((1, LANES), F32)],
        compiler_params=pltpu.CompilerParams(dimension_semantics=("arbitrary",), vmem_limit_bytes=VMEM_LIMIT),
        name="merge",
    )(x2, ya, yb, gt, wua, wub, wo, g, wrt, brt, slt)


def _row_copy(src, src_row, dst, dst_row, sem):
    return pltpu.make_async_copy(src.at[pl.ds(src_row, 1)], dst.at[pl.ds(dst_row, 1)], sem)


def _dispatch_kernel(dest_ref, u_ref, xs_in_ref, xs_ref, sem):
    del xs_in_ref
    base = pl.program_id(0) * DISPATCH_TILE

    def issue(t, c):
        for s in range(2):
            _row_copy(u_ref, base + t, xs_ref, dest_ref[0, 0, 2 * t + s], sem).start()
        return c

    def drain(t, c):
        for s in range(2):
            _row_copy(u_ref, base + t, xs_ref, dest_ref[0, 0, 2 * t + s], sem).wait()
        return c

    lax.fori_loop(0, DISPATCH_TILE, issue, 0)
    lax.fori_loop(0, DISPATCH_TILE, drain, 0)


def _dispatch(dest3, u2, xs0):
    n = u2.shape[0]
    return pl.pallas_call(
        _dispatch_kernel,
        grid=(n // DISPATCH_TILE,),
        in_specs=[pl.BlockSpec((1, 1, 2 * DISPATCH_TILE), lambda i: (i, 0, 0), memory_space=pltpu.SMEM),
                  pl.BlockSpec(memory_space=pl.ANY), pl.BlockSpec(memory_space=pl.ANY)],
        out_specs=pl.BlockSpec(memory_space=pl.ANY),
        out_shape=jax.ShapeDtypeStruct(xs0.shape, xs0.dtype),
        scratch_shapes=[pltpu.SemaphoreType.DMA],
        input_output_aliases={2: 0},
        compiler_params=pltpu.CompilerParams(dimension_semantics=("arbitrary",), has_side_effects=True),
        name="dispatch",
    )(dest3, u2, xs0)


def _ffn_kernel(te_ref, nu_ref, xs_ref, w1_ref, w3_ref, w2_ref, ys_ref):
    del te_ref

    @pl.when(pl.program_id(0) < nu_ref[0])
    def _():
        x = xs_ref[...].astype(BF16)
        hid = jax.nn.silu(_dot(x, w1_ref[0])) * _dot(x, w3_ref[0])
        ys_ref[...] = _dot(hid.astype(BF16), w2_ref[0])

    @pl.when(pl.program_id(0) >= nu_ref[0])
    def _():
        ys_ref[...] = jnp.zeros_like(ys_ref)


def _ffn(tile_expert, n_used, xs, w1, w3, w2):
    p = xs.shape[0]
    grid_spec = pltpu.PrefetchScalarGridSpec(
        num_scalar_prefetch=2,
        grid=(p // FFN_TILE,),
        in_specs=[pl.BlockSpec((FFN_TILE, D_MODEL), lambda t, te, nu: (t, 0)),
                  pl.BlockSpec((1, D_MODEL, D_EXPERT), lambda t, te, nu: (te[t], 0, 0)),
                  pl.BlockSpec((1, D_MODEL, D_EXPERT), lambda t, te, nu: (te[t], 0, 0)),
                  pl.BlockSpec((1, D_EXPERT, D_MODEL), lambda t, te, nu: (te[t], 0, 0))],
        out_specs=pl.BlockSpec((FFN_TILE, D_MODEL), lambda t, te, nu: (t, 0)),
    )
    return pl.pallas_call(
        _ffn_kernel,
        grid_spec=grid_spec,
        out_shape=jax.ShapeDtypeStruct((p, D_MODEL), F32),
        compiler_params=pltpu.CompilerParams(dimension_semantics=("arbitrary",), vmem_limit_bytes=VMEM_LIMIT),
        name="ffn",
    )(tile_expert, n_used, xs, w1, w3, w2)


def _final_kernel(dest_ref, h1_ref, r_ref, g_ref, ys_ref, o_ref, buf_ref, sem):
    def issue(t, c):
        for s in range(2):
            _row_copy(ys_ref, dest_ref[0, 0, 2 * t + s], buf_ref.at[s], t, sem).start()
        return c

    def drain(t, c):
        for s in range(2):
            _row_copy(ys_ref, dest_ref[0, 0, 2 * t + s], buf_ref.at[s], t, sem).wait()
        return c

    lax.fori_loop(0, FINAL_TILE, issue, 0)
    lax.fori_loop(0, FINAL_TILE, drain, 0)
    r = r_ref[...]
    moe = r[:, 2:3] * buf_ref[0] + r[:, 3:4] * buf_ref[1]
    o_ref[...] = _rmsnorm(h1_ref[...] + moe, g_ref[...])


def _final(dest3, h1, r, g, ys):
    n = h1.shape[0]
    tm = FINAL_TILE
    return pl.pallas_call(
        _final_kernel,
        grid=(n // tm,),
        in_specs=[pl.BlockSpec((1, 1, 2 * tm), lambda i: (i, 0, 0), memory_space=pltpu.SMEM),
                  pl.BlockSpec((tm, D_MODEL), lambda i: (i, 0)),
                  pl.BlockSpec((tm, LANES), lambda i: (i, 0)),
                  pl.BlockSpec((1, D_MODEL), lambda i: (0, 0)),
                  pl.BlockSpec(memory_space=pl.ANY)],
        out_specs=pl.BlockSpec((tm, D_MODEL), lambda i: (i, 0)),
        out_shape=jax.ShapeDtypeStruct((n, D_MODEL), F32),
        scratch_shapes=[pltpu.VMEM((2, tm, D_MODEL), F32), pltpu.SemaphoreType.DMA],
        compiler_params=pltpu.CompilerParams(dimension_semantics=("arbitrary",), vmem_limit_bytes=VMEM_LIMIT),
        name="final",
    )(dest3, h1, r, g, ys)


def _rope_tables(pos):
    half = ROPE_DIM // 2
    inv = ROPE_THETA ** (-jnp.arange(half, dtype=F32) / half)
    ang = pos.astype(F32)[:, None] * inv[None, :]
    cos, sin = jnp.cos(ang), jnp.sin(ang)
    d = np.arange(LANES) % HEAD_DIM
    f = d % half
    one, zero = jnp.ones_like(cos[:, f]), jnp.zeros_like(cos[:, f])
    cos_t = jnp.where(d < ROPE_DIM, cos[:, f], one)
    sa = jnp.where((d >= half) & (d < ROPE_DIM), sin[:, f], zero)
    sb = jnp.where(d < half, -sin[:, f], zero)
    return cos_t, sa, sb


def _proj_weights(w_in, w_uk):
    qa, ka, va, qb, ckv, krope, qi, ki, wi, ga, gb = jnp.split(w_in, list(np.cumsum(SPLITS)[:-1]), axis=-1)
    zeros = lambda width: jnp.zeros((D_MODEL, width), w_in.dtype)
    kv = jnp.concatenate([krope, zeros(LANES - ROPE_DIM), ckv], axis=1)
    w = jnp.concatenate([qa, ka, va, qb, qi, ki, ki, kv, ga, gb], axis=1).astype(BF16)
    assert w.shape[1] == PROJ_W
    wabs = jnp.zeros((N_HEADS, HEAD_DIM, N_HEADS, QF_W), F32)
    eye = jnp.eye(ROPE_DIM, dtype=F32)
    for h in range(N_HEADS):
        wabs = wabs.at[h, :ROPE_DIM, h, :ROPE_DIM].set(eye)
        wabs = wabs.at[h, ROPE_DIM:, h, LANES:].set(w_uk[h].T)
    wabs = wabs.reshape(HEADS_W, N_HEADS * QF_W).astype(BF16)
    wwi = jnp.concatenate([wi.T, jnp.zeros((16 - N_HEADS, D_MODEL), w_in.dtype)], axis=0).astype(BF16)
    wckv = ckv.T.astype(BF16)
    return w, wabs, wwi, wckv


def _tri(n, strict_lower):
    r = np.arange(n)
    m = (r[None, :] < r[:, None]) if strict_lower else (r[None, :] <= r[:, None])
    return jnp.asarray(m, BF16)


def kernel(x, meta_tokens, norm_mix_g, w_in, w_uk, w_uv, w_up_a, w_up_b, w_o, norm_ffn_g, w_group, b_group,
           w_router, b_router, w1, w3, w2, norm_final_g):
    batch, seq, d = x.shape
    assert d == D_MODEL and seq % LANES == 0 and norm_mix_g.shape[0] == 1
    n = batch * seq
    nq = seq // LANES
    nb = nq + 1
    tp = seq + LANES
    ktop = min(TOPK_MAX, seq // 4)
    x2 = x.reshape(n, d)

    w, wabs, wwi, wckv = _proj_weights(w_in[0], w_uk[0])
    g_mix = norm_mix_g[0][None, :]
    proj_tile = 512 if seq % 512 == 0 else LANES
    px = _proj(x2, g_mix, _rope_tables(N_META + jnp.arange(seq)), w, wabs, wwi, wckv, proj_tile, seq // proj_tile)
    meta = jnp.concatenate([meta_tokens.astype(x.dtype), jnp.zeros((LANES - N_META, d), x.dtype)], axis=0)
    pm = _proj(meta, g_mix, _rope_tables(jnp.arange(LANES)), w, wabs, wwi, wckv, LANES, 1)
    qkv_x, qf_x, qi_x, ki_x, kv_x, gt_x, wit_x, ckvt_x = px
    qkv_m, _, _, ki_m, kv_m, _, _, ckvt_m = pm

    def with_meta(xpart, mpart):
        wdt = xpart.shape[1]
        mb = jnp.broadcast_to(mpart[None], (batch, LANES, wdt))
        return jnp.concatenate([mb, xpart.reshape(batch, seq, wdt)], axis=1).reshape(batch * tp, wdt)

    kv_a = with_meta(qkv_x[:, HEADS_W:], qkv_m[:, HEADS_W:])
    su = jnp.concatenate([_tri(LANES, True), jnp.ones((LANES, LANES), BF16)], axis=1)
    ya = _sb(qkv_x, kv_a, su, batch, seq)

    kip = with_meta(ki_x, ki_m)
    kvp = with_meta(kv_x, kv_m)
    ckvt3 = jnp.concatenate([jnp.broadcast_to(ckvt_m[None, None], (batch, 1, KV_LATENT, LANES)),
                             ckvt_x.reshape(KV_LATENT, batch, nq, LANES).transpose(1, 2, 0, 3)], axis=1)
    ckvt3 = ckvt3.reshape(batch * nb, KV_LATENT, LANES)
    wuvt = jnp.swapaxes(w_uv[0], 1, 2).astype(BF16)
    yb = _dsa(qi_x, wit_x, qf_x, kip, kvp, ckvt3, _tri(LANES, False), wuvt, batch, seq, ktop)

    wrt = jnp.concatenate([w_router[0], w_group[0],
                           jnp.zeros((d, LANES - N_EXPERTS - N_GROUPS), F32)], axis=1).astype(F32)
    brt = jnp.concatenate([b_router[0], b_group[0], jnp.zeros((LANES - N_EXPERTS - N_GROUPS,), F32)])[None, :]
    h1, u2, r, counts = _merge(x2, ya, yb, gt_x, w_up_a[0].astype(BF16), w_up_b[0].astype(BF16),
                               w_o[0].astype(BF16), norm_ffn_g[0][None, :], wrt, brt.astype(F32),
                               _tri(MERGE_TILE, True))

    cnt = counts[0, :N_EXPERTS].astype(jnp.int32)
    padded = (cnt + FFN_TILE - 1) // FFN_TILE * FFN_TILE
    p_end = jnp.cumsum(padded)
    p_start = p_end - padded
    e_sel = r[:, 0:2].astype(jnp.int32)
    dest = p_start[e_sel] + r[:, 4:6].astype(jnp.int32)
    n_tiles = (2 * n) // FFN_TILE + N_EXPERTS
    tile_expert = jnp.minimum(jnp.searchsorted(p_end, jnp.arange(n_tiles, dtype=jnp.int32) * FFN_TILE,
                                               side='right'), N_EXPERTS - 1).astype(jnp.int32)
    n_used = (p_end[-1] // FFN_TILE).astype(jnp.int32)
    tile_expert = jnp.where(jnp.arange(n_tiles) < n_used, tile_expert, tile_expert[jnp.maximum(n_used - 1, 0)])

    xs0 = jnp.zeros((n_tiles * FFN_TILE, d), F32)
    xs = _dispatch(dest.reshape(n // DISPATCH_TILE, 1, 2 * DISPATCH_TILE), u2, xs0)
    ys = _ffn(tile_expert, n_used[None], xs, w1[0].astype(BF16), w3[0].astype(BF16), w2[0].astype(BF16))
    out = _final(dest.reshape(n // FINAL_TILE, 1, 2 * FINAL_TILE), h1, r, norm_final_g[None, :], ys)
    return out.reshape(batch, seq, d)
```

```python
import functools
import math

import numpy as np
import jax
import jax.numpy as jnp
from jax import lax
from jax.experimental import pallas as pl
from jax.experimental.pallas import tpu as pltpu

D_MODEL = 1024
CHUNK = 64
N_META = 16
HEAD_DIM = 64
ROPE_DIM = 16
ROPE_THETA = 500000.0
N_HEADS = 8
KV_LATENT = 128
IDX_DIM = 64
TOPK_MAX = 256
N_GROUPS = 4
EXP_PER_GROUP = 8
N_EXPERTS = 32
D_EXPERT = 512
NORM_EPS = 1e-6
HEADS_W = N_HEADS * HEAD_DIM
SPLITS = (HEADS_W, HEADS_W, HEADS_W, HEADS_W, KV_LATENT, ROPE_DIM, N_HEADS * IDX_DIM, IDX_DIM, N_HEADS,
          D_MODEL, D_MODEL)

LANES = 128
QF_W = 2 * LANES
INT_MIN = -2 ** 31
EXP_UNDERFLOW = -104.0
ATTN_SCALE = 1.0 / math.sqrt(HEAD_DIM)
VMEM_LIMIT = 56 * 1024 * 1024
DSA_SUB = 4
DSA_KEYS = DSA_SUB * LANES
FFN_TILE = 256
MERGE_TILE = 512
FINAL_TILE = 256
DISPATCH_TILE = 512

BF16 = jnp.bfloat16
F32 = jnp.float32

_C_QKV = (0, 3 * HEADS_W)
_C_QB = (_C_QKV[1], _C_QKV[1] + HEADS_W)
_C_QI = (_C_QB[1], _C_QB[1] + HEADS_W)
_C_KI = (_C_QI[1], _C_QI[1] + LANES)
_C_KV = (_C_KI[1], _C_KI[1] + QF_W)
_C_G = (_C_KV[1], _C_KV[1] + 2 * D_MODEL)
PROJ_W = _C_G[1]


def _dot(a, b):
    return jnp.dot(a, b, preferred_element_type=F32)


def _dot_t(a, b):
    return lax.dot_general(a, b, (((1,), (1,)), ((), ())), preferred_element_type=F32)


def _rmsnorm(x, g):
    return x * lax.rsqrt(jnp.mean(x * x, axis=-1, keepdims=True) + NORM_EPS) * g


def _rope(x, cos, sa, sb):
    w = x.shape[1]
    n = w // LANES
    if n > 1:
        cos, sa, sb = (jnp.concatenate([t] * n, axis=1) for t in (cos, sa, sb))
    return x * cos + pltpu.roll(x, 8, 1) * sa + pltpu.roll(x, w - 8, 1) * sb


def _proj_kernel(x_ref, g_ref, cos_ref, sa_ref, sb_ref, w_ref, wabs_ref, wwi_ref, wckv_ref,
                 qkv_ref, qf_ref, qi_ref, ki_ref, kv_ref, gt_ref, wit_ref, ckvt_ref):
    ub = _rmsnorm(x_ref[...], g_ref[...]).astype(BF16)
    cos, sa, sb = cos_ref[...], sa_ref[...], sb_ref[...]
    qkv_ref[...] = _dot(ub, w_ref[:, _C_QKV[0]:_C_QKV[1]]).astype(BF16)
    qb = _rope(_dot(ub, w_ref[:, _C_QB[0]:_C_QB[1]]), cos, sa, sb).astype(BF16)
    qf_ref[...] = _dot(qb, wabs_ref[...]).astype(BF16)
    qi_ref[...] = _rope(_dot(ub, w_ref[:, _C_QI[0]:_C_QI[1]]), cos, sa, sb).astype(BF16)
    ki_ref[...] = _rope(_dot(ub, w_ref[:, _C_KI[0]:_C_KI[1]]), cos, sa, sb).astype(BF16)
    kv = _dot(ub, w_ref[:, _C_KV[0]:_C_KV[1]])
    kv_ref[...] = jnp.concatenate([_rope(kv[:, :LANES], cos, sa, sb), kv[:, LANES:]], axis=1).astype(BF16)
    gt_ref[...] = jax.nn.sigmoid(_dot(ub, w_ref[:, _C_G[0]:_C_G[1]])).astype(BF16)
    wit_ref[...] = _dot_t(wwi_ref[...], ub)
    ckvt_ref[...] = _dot_t(wckv_ref[...], ub).astype(BF16)


def _proj(x2, g, tabs, w, wabs, wwi, wckv, tile, tab_tiles):
    n = x2.shape[0]
    row = lambda width: pl.BlockSpec((tile, width), lambda i: (i, 0))
    const = lambda shape: pl.BlockSpec(shape, lambda i: (0, 0))
    tab = pl.BlockSpec((tile, LANES), lambda i: (i % tab_tiles, 0))
    out_shape = (
        jax.ShapeDtypeStruct((n, 3 * HEADS_W), BF16),
        jax.ShapeDtypeStruct((n, N_HEADS * QF_W), BF16),
        jax.ShapeDtypeStruct((n, HEADS_W), BF16),
        jax.ShapeDtypeStruct((n, LANES), BF16),
        jax.ShapeDtypeStruct((n, QF_W), BF16),
        jax.ShapeDtypeStruct((n, 2 * D_MODEL), BF16),
        jax.ShapeDtypeStruct((16, n), F32),
        jax.ShapeDtypeStruct((KV_LATENT, n), BF16),
    )
    out_specs = (row(3 * HEADS_W), row(N_HEADS * QF_W), row(HEADS_W), row(LANES), row(QF_W), row(2 * D_MODEL),
                 pl.BlockSpec((16, tile), lambda i: (0, i)), pl.BlockSpec((KV_LATENT, tile), lambda i: (0, i)))
    return pl.pallas_call(
        _proj_kernel,
        grid=(n // tile,),
        in_specs=[row(D_MODEL), const((1, D_MODEL)), tab, tab, tab, const(w.shape), const(wabs.shape),
                  const(wwi.shape), const(wckv.shape)],
        out_specs=out_specs,
        out_shape=out_shape,
        compiler_params=pltpu.CompilerParams(dimension_semantics=("arbitrary",), vmem_limit_bytes=VMEM_LIMIT),
        name="proj",
    )(x2, g, *tabs, w, wabs, wwi, wckv)


def _sb_kernel(q_ref, k_ref, v_ref, su_ref, o_ref, acc_ref, carry_ref):
    ib = pl.program_id(1) + 1
    lane = lax.broadcasted_iota(jnp.int32, (LANES, LANES), 1)
    row = lax.broadcasted_iota(jnp.int32, (LANES, LANES), 0)
    low = lane < HEAD_DIM
    high = jnp.logical_not(low)
    su = su_ref[...]
    qpos = ib * LANES + row
    acc_ref[...] = jnp.zeros_like(acc_ref)
    carry_ref[...] = jnp.zeros_like(carry_ref)

    def body(st):
        j, _ = st
        rows = pl.ds(pl.multiple_of(j * LANES, LANES), LANES)
        kpos = j * LANES + lane
        mask = (kpos < qpos) & ((kpos < N_META) | (kpos >= LANES))
        top = None
        for p in range(N_HEADS // 2):
            cols = slice(p * LANES, (p + 1) * LANES)
            q2, kb, vb = q_ref[:, cols], k_ref[rows, cols], v_ref[rows, cols]
            zero = jnp.zeros_like(q2)
            pv = None
            for half, keep in enumerate((low, high)):
                h = 2 * p + half
                z = _dot_t(jnp.where(keep, q2, zero), kb) * ATTN_SCALE
                ls_pos = jnp.minimum(z, 0.0) - jnp.log1p(jnp.exp(-jnp.abs(z)))
                log_keep = jnp.where(mask, ls_pos - z, 0.0)
                hi = log_keep.astype(BF16)
                lo = (log_keep - hi.astype(F32)).astype(BF16)
                cs = _dot(hi, su) + _dot(lo, su)
                carry = carry_ref[h]
                a = jnp.where(mask, jnp.exp(ls_pos + cs[:, :LANES] + carry), 0.0)
                d = _dot(a.astype(BF16), jnp.where(keep, vb, zero))
                pv = d if pv is None else pv + d
                carry = carry + cs[:, LANES:]
                carry_ref[h] = carry
                top = carry if top is None else jnp.maximum(top, carry)
            acc_ref[p] += pv
        return j - 1, jnp.max(top)

    def cond(st):
        return (st[0] >= 0) & (st[1] > EXP_UNDERFLOW)

    lax.while_loop(cond, body, (ib, jnp.float32(0.0)))
    for p in range(N_HEADS // 2):
        o_ref[:, p * LANES:(p + 1) * LANES] = acc_ref[p].astype(BF16)


def _sb(qkv_x, kvp, su, batch, seq):
    nq = seq // LANES
    tp = seq + LANES
    return pl.pallas_call(
        _sb_kernel,
        grid=(batch, nq),
        in_specs=[pl.BlockSpec((LANES, HEADS_W), lambda b, i: (b * nq + i, 0)),
                  pl.BlockSpec((tp, HEADS_W), lambda b, i: (b, 0)),
                  pl.BlockSpec((tp, HEADS_W), lambda b, i: (b, 1)),
                  pl.BlockSpec(su.shape, lambda b, i: (0, 0))],
        out_specs=pl.BlockSpec((LANES, HEADS_W), lambda b, i: (b * nq + i, 0)),
        out_shape=jax.ShapeDtypeStruct((batch * seq, HEADS_W), BF16),
        scratch_shapes=[pltpu.VMEM((N_HEADS // 2, LANES, LANES), F32), pltpu.VMEM((N_HEADS, LANES, LANES), F32)],
        compiler_params=pltpu.CompilerParams(dimension_semantics=("arbitrary",) * 2, vmem_limit_bytes=VMEM_LIMIT),
        name="sb",
    )(qkv_x, kvp, kvp, su)


def _dsa_kernel(qi_ref, wit_ref, qf_ref, ki_ref, kv_ref, ckvt_ref, lt_ref, wuvt_ref, o_ref,
                sk_ref, acc_ref, m_ref, l_ref, qim_ref, ybt_ref, *, ktop):
    ib = pl.program_id(1) + 1
    nt = (ib + DSA_SUB) // DSA_SUB
    lane = lax.broadcasted_iota(jnp.int32, (LANES, LANES), 1)
    klane = lax.broadcasted_iota(jnp.int32, (DSA_KEYS, LANES), 1)
    krow = lax.broadcasted_iota(jnp.int32, (DSA_KEYS, LANES), 0)
    low = lane < HEAD_DIM
    for h in range(N_HEADS):
        blk = qi_ref[:, (h // 2) * LANES:(h // 2 + 1) * LANES]
        keep = low if h % 2 == 0 else jnp.logical_not(low)
        qim_ref[h] = jnp.where(keep, blk, jnp.zeros_like(blk))
    wit = wit_ref[...]
    cq = 1 + ((ib * LANES + klane - LANES) >> 6)

    def tile(t):
        return pl.ds(pl.multiple_of(t * DSA_KEYS, DSA_KEYS), DSA_KEYS)

    def scores(t, c):
        kb = ki_ref[tile(t), :]
        s = jnp.zeros((DSA_KEYS, LANES), F32)
        for h in range(N_HEADS):
            s = s + wit[h:h + 1, :] * jnp.maximum(_dot_t(kb, qim_ref[h]), 0.0)
        s = jnp.where(s == 0.0, 0.0, s)
        bits = lax.bitcast_convert_type(s, jnp.int32)
        key = bits ^ ((bits >> 31) & 0x7FFFFFFF)
        rk = t * DSA_KEYS + krow
        ck = jnp.where(rk < LANES, 0, 1 + ((rk - LANES) >> 6))
        adm = ((rk < N_META) | (rk >= LANES)) & (ck <= cq)
        sk_ref[tile(t), :] = jnp.where(adm, key, INT_MIN)
        return c

    lax.fori_loop(0, nt, scores, 0)

    def count(pred):
        def step(t, acc):
            hit = pred(sk_ref[tile(t), :]).astype(jnp.int32)
            for u in range(DSA_SUB):
                acc = acc + hit[u * LANES:(u + 1) * LANES]
            return acc
        acc = lax.fori_loop(0, nt, step, jnp.zeros((LANES, LANES), jnp.int32))
        return jnp.sum(acc, axis=0, keepdims=True)

    zero_row = jnp.zeros((1, LANES), jnp.int32)
    thr = jnp.where(count(lambda k: k >= zero_row) >= ktop, zero_row, jnp.full((1, LANES), INT_MIN, jnp.int32))

    def bit_step(it, thr):
        cand = thr | jnp.left_shift(jnp.int32(1), 30 - it)
        return jnp.where(count(lambda k: k >= cand) >= ktop, cand, thr)

    thr = lax.fori_loop(0, 31, bit_step, thr)
    need = (ktop - count(lambda k: k > thr)).astype(F32)

    acc_ref[...] = jnp.zeros_like(acc_ref)
    m_ref[...] = jnp.full_like(m_ref, -1e30)
    l_ref[...] = jnp.zeros_like(l_ref)
    lt = lt_ref[...]

    def attend(t, tie_carry):
        skt = sk_ref[tile(t), :]
        tie = (skt == thr) & (skt > INT_MIN)
        rank = _dot(lt, tie.astype(BF16)) + tie_carry
        sel = (skt > thr) | (tie & (rank <= need))
        kvb = kv_ref[tile(t), :]
        cb = ckvt_ref[t]
        for h in range(N_HEADS):
            lg = _dot_t(kvb, qf_ref[:, h * QF_W:(h + 1) * QF_W]) * ATTN_SCALE
            s = jnp.where(sel, lg, -1e30)
            m_old = m_ref[h:h + 1, :]
            m_new = jnp.maximum(m_old, jnp.max(s, axis=0, keepdims=True))
            alpha = jnp.exp(m_old - m_new)
            p = jnp.where(sel, jnp.exp(s - m_new), 0.0)
            l_ref[h:h + 1, :] = l_ref[h:h + 1, :] * alpha + jnp.sum(p, axis=0, keepdims=True)
            acc_ref[h] = acc_ref[h] * alpha + _dot(cb, p.astype(BF16))
            m_ref[h:h + 1, :] = m_new
        return rank[DSA_KEYS - 1:DSA_KEYS, :]

    lax.fori_loop(0, nt, attend, jnp.zeros((1, LANES), F32))

    for h in range(N_HEADS):
        o = (acc_ref[h] / l_ref[h:h + 1, :]).astype(BF16)
        ybt_ref[h * HEAD_DIM:(h + 1) * HEAD_DIM, :] = _dot(wuvt_ref[h], o)
    o_ref[...] = ybt_ref[...].T.astype(BF16)


def _dsa(qi, wit, qf, kip, kvp, ckvt3, lt, wuvt, batch, seq, ktop):
    nq = seq // LANES
    tpk = kip.shape[0] // batch
    ntk = tpk // DSA_KEYS
    return pl.pallas_call(
        functools.partial(_dsa_kernel, ktop=ktop),
        grid=(batch, nq),
        in_specs=[pl.BlockSpec((LANES, HEADS_W), lambda b, i: (b * nq + i, 0)),
                  pl.BlockSpec((16, LANES), lambda b, i: (0, b * nq + i)),
                  pl.BlockSpec((LANES, N_HEADS * QF_W), lambda b, i: (b * nq + i, 0)),
                  pl.BlockSpec((tpk, LANES), lambda b, i: (b, 0)),
                  pl.BlockSpec((tpk, QF_W), lambda b, i: (b, 0)),
                  pl.BlockSpec((ntk, KV_LATENT, DSA_KEYS), lambda b, i: (b, 0, 0)),
                  pl.BlockSpec(lt.shape, lambda b, i: (0, 0)),
                  pl.BlockSpec(wuvt.shape, lambda b, i: (0, 0, 0))],
        out_specs=pl.BlockSpec((LANES, HEADS_W), lambda b, i: (b * nq + i, 0)),
        out_shape=jax.ShapeDtypeStruct((batch * seq, HEADS_W), BF16),
        scratch_shapes=[pltpu.VMEM((tpk, LANES), jnp.int32),
                        pltpu.VMEM((N_HEADS, KV_LATENT, LANES), F32),
                        pltpu.VMEM((N_HEADS, LANES), F32),
                        pltpu.VMEM((N_HEADS, LANES), F32),
                        pltpu.VMEM((N_HEADS, LANES, LANES), BF16),
                        pltpu.VMEM((HEADS_W, LANES), F32)],
        compiler_params=pltpu.CompilerParams(dimension_semantics=("arbitrary",) * 2, vmem_limit_bytes=VMEM_LIMIT),
        name="dsa",
    )(qi, wit, qf, kip, kvp, ckvt3, lt, wuvt)


def _merge_kernel(x_ref, ya_ref, yb_ref, gt_ref, wua_ref, wub_ref, wo_ref, g_ref, wrt_ref, brt_ref, slt_ref,
                  h1_ref, u2_ref, r_ref, cnt_ref, carry_ref):
    @pl.when(pl.program_id(0) == 0)
    def _():
        carry_ref[...] = jnp.zeros_like(carry_ref)

    gt = gt_ref[...].astype(F32)
    z = gt[:, :D_MODEL] * _dot(ya_ref[...], wua_ref[...]) + gt[:, D_MODEL:] * _dot(yb_ref[...], wub_ref[...])
    h1 = x_ref[...] + _dot(z.astype(BF16), wo_ref[...])
    h1_ref[...] = h1
    u2 = _rmsnorm(h1, g_ref[...])
    u2_ref[...] = u2
    lg = jnp.dot(u2, wrt_ref[...], preferred_element_type=F32, precision=lax.Precision.HIGHEST) + brt_ref[...]

    tm = lg.shape[0]
    lane = lax.broadcasted_iota(jnp.int32, (tm, LANES), 1)
    big = jnp.int32(LANES)

    def softmax(mask):
        x = jnp.where(mask, lg, -jnp.inf)
        e = jnp.where(mask, jnp.exp(x - jnp.max(x, axis=-1, keepdims=True)), 0.0)
        return e / jnp.sum(e, axis=-1, keepdims=True)

    def top1(p, mask):
        best = jnp.max(jnp.where(mask, p, -1.0), axis=-1, keepdims=True)
        idx = jnp.min(jnp.where(mask & (p == best), lane, big), axis=-1, keepdims=True)
        return best, idx

    gmask = (lane >= N_EXPERTS) & (lane < N_EXPERTS + N_GROUPS)
    p_grp, g_lane = top1(softmax(gmask), gmask)
    first = (g_lane - N_EXPERTS) * EXP_PER_GROUP
    emask = (lane >= first) & (lane < first + EXP_PER_GROUP)
    ep = softmax(emask)
    p0, e0 = top1(ep, emask)
    rest = emask & (lane != e0)
    p1, e1 = top1(ep, rest)
    den = p0 + p1
    w0 = p_grp * p0 / den
    w1 = p_grp * p1 / den

    oh0 = lane == e0
    oh1 = lane == e1
    both = (oh0 | oh1).astype(BF16)
    before = _dot(slt_ref[...], both) + carry_ref[...]
    rank0 = jnp.sum(jnp.where(oh0, before, 0.0), axis=-1, keepdims=True)
    rank1 = jnp.sum(jnp.where(oh1, before, 0.0), axis=-1, keepdims=True)
    carry = carry_ref[...] + jnp.sum(both.astype(F32), axis=0, keepdims=True)
    carry_ref[...] = carry
    cnt_ref[...] = carry

    cols = (e0.astype(F32), e1.astype(F32), w0, w1, rank0, rank1)
    r = jnp.zeros((tm, LANES), F32)
    for c, v in enumerate(cols):
        r = jnp.where(lane == c, v, r)
    r_ref[...] = r


def _merge(x2, ya, yb, gt, wua, wub, wo, g, wrt, brt, slt):
    n = x2.shape[0]
    tm = MERGE_TILE
    row = lambda width: pl.BlockSpec((tm, width), lambda i: (i, 0))
    const = lambda a: pl.BlockSpec(a.shape, lambda i: (0,) * a.ndim)
    return pl.pallas_call(
        _merge_kernel,
        grid=(n // tm,),
        in_specs=[row(D_MODEL), row(HEADS_W), row(HEADS_W), row(2 * D_MODEL), const(wua), const(wub), const(wo),
                  const(g), const(wrt), const(brt), const(slt)],
        out_specs=(row(D_MODEL), row(D_MODEL), row(LANES), pl.BlockSpec((1, LANES), lambda i: (0, 0))),
        out_shape=(jax.ShapeDtypeStruct((n, D_MODEL), F32), jax.ShapeDtypeStruct((n, D_MODEL), F32),
                   jax.ShapeDtypeStruct((n, LANES), F32), jax.ShapeDtypeStruct((1, LANES), F32)),
        scratch_shapes=[pltpu.VMEM((1, LANES), F32)],
        compiler_params=pltpu.CompilerParams(dimension_semantics=("arbitrary",), vmem_limit_bytes=VMEM_LIMIT),
        name="merge",
    )(x2, ya, yb, gt, wua, wub, wo, g, wrt, brt, slt)


def _row_copy(src, src_row, dst, dst_row, sem):
    return pltpu.make_async_copy(src.at[pl.ds(src_row, 1)], dst.at[pl.ds(dst_row, 1)], sem)


def _dispatch_kernel(dest_ref, u_ref, xs_in_ref, xs_ref, sem):
    del xs_in_ref

    def issue(t, c):
        for s in range(2):
            _row_copy(u_ref, t, xs_ref, dest_ref[0, 0, 2 * t + s], sem).start()
        return c

    def drain(t, c):
        for s in range(2):
            _row_copy(u_ref, t, xs_ref, dest_ref[0, 0, 2 * t + s], sem).wait()
        return c

    lax.fori_loop(0, DISPATCH_TILE, issue, 0)
    lax.fori_loop(0, DISPATCH_TILE, drain, 0)


def _dispatch(dest3, u2, xs0):
    n = u2.shape[0]
    return pl.pallas_call(
        _dispatch_kernel,
        grid=(n // DISPATCH_TILE,),
        in_specs=[pl.BlockSpec((1, 1, 2 * DISPATCH_TILE), lambda i: (i, 0, 0), memory_space=pltpu.SMEM),
                  pl.BlockSpec((DISPATCH_TILE, D_MODEL), lambda i: (i, 0)), pl.BlockSpec(memory_space=pl.ANY)],
        out_specs=pl.BlockSpec(memory_space=pl.ANY),
        out_shape=jax.ShapeDtypeStruct(xs0.shape, xs0.dtype),
        scratch_shapes=[pltpu.SemaphoreType.DMA],
        input_output_aliases={2: 0},
        compiler_params=pltpu.CompilerParams(dimension_semantics=("arbitrary",), has_side_effects=True),
        name="dispatch",
    )(dest3, u2, xs0)


def _ffn_kernel(te_ref, nu_ref, xs_ref, w1_ref, w3_ref, w2_ref, ys_ref):
    del te_ref

    @pl.when(pl.program_id(0) < nu_ref[0])
    def _():
        x = xs_ref[...].astype(BF16)
        hid = jax.nn.silu(_dot(x, w1_ref[0])) * _dot(x, w3_ref[0])
        ys_ref[...] = _dot(hid.astype(BF16), w2_ref[0])

    @pl.when(pl.program_id(0) >= nu_ref[0])
    def _():
        ys_ref[...] = jnp.zeros_like(ys_ref)


def _ffn(tile_expert, n_used, xs, w1, w3, w2):
    p = xs.shape[0]
    grid_spec = pltpu.PrefetchScalarGridSpec(
        num_scalar_prefetch=2,
        grid=(p // FFN_TILE,),
        in_specs=[pl.BlockSpec((FFN_TILE, D_MODEL), lambda t, te, nu: (t, 0)),
                  pl.BlockSpec((1, D_MODEL, D_EXPERT), lambda t, te, nu: (te[t], 0, 0)),
                  pl.BlockSpec((1, D_MODEL, D_EXPERT), lambda t, te, nu: (te[t], 0, 0)),
                  pl.BlockSpec((1, D_EXPERT, D_MODEL), lambda t, te, nu: (te[t], 0, 0))],
        out_specs=pl.BlockSpec((FFN_TILE, D_MODEL), lambda t, te, nu: (t, 0)),
    )
    return pl.pallas_call(
        _ffn_kernel,
        grid_spec=grid_spec,
        out_shape=jax.ShapeDtypeStruct((p, D_MODEL), F32),
        compiler_params=pltpu.CompilerParams(dimension_semantics=("arbitrary",), vmem_limit_bytes=VMEM_LIMIT),
        name="ffn",
    )(tile_expert, n_used, xs, w1, w3, w2)


def _final_kernel(dest_ref, h1_ref, r_ref, g_ref, ys_ref, o_ref, buf_ref, sem):
    def issue(t, c):
        for s in range(2):
            _row_copy(ys_ref, dest_ref[0, 0, 2 * t + s], buf_ref.at[s], t, sem).start()
        return c

    def drain(t, c):
        for s in range(2):
            _row_copy(ys_ref, dest_ref[0, 0, 2 * t + s], buf_ref.at[s], t, sem).wait()
        return c

    lax.fori_loop(0, FINAL_TILE, issue, 0)
    lax.fori_loop(0, FINAL_TILE, drain, 0)
    r = r_ref[...]
    moe = r[:, 2:3] * buf_ref[0] + r[:, 3:4] * buf_ref[1]
    o_ref[...] = _rmsnorm(h1_ref[...] + moe, g_ref[...])


def _final(dest3, h1, r, g, ys):
    n = h1.shape[0]
    tm = FINAL_TILE
    return pl.pallas_call(
        _final_kernel,
        grid=(n // tm,),
        in_specs=[pl.BlockSpec((1, 1, 2 * tm), lambda i: (i, 0, 0), memory_space=pltpu.SMEM),
                  pl.BlockSpec((tm, D_MODEL), lambda i: (i, 0)),
                  pl.BlockSpec((tm, LANES), lambda i: (i, 0)),
                  pl.BlockSpec((1, D_MODEL), lambda i: (0, 0)),
                  pl.BlockSpec(memory_space=pl.ANY)],
        out_specs=pl.BlockSpec((tm, D_MODEL), lambda i: (i, 0)),
        out_shape=jax.ShapeDtypeStruct((n, D_MODEL), F32),
        scratch_shapes=[pltpu.VMEM((2, tm, D_MODEL), F32), pltpu.SemaphoreType.DMA],
        compiler_params=pltpu.CompilerParams(dimension_semantics=("arbitrary",), vmem_limit_bytes=VMEM_LIMIT),
        name="final",
    )(dest3, h1, r, g, ys)


def _rope_tables(pos):
    half = ROPE_DIM // 2
    inv = ROPE_THETA ** (-jnp.arange(half, dtype=F32) / half)
    ang = pos.astype(F32)[:, None] * inv[None, :]
    cos, sin = jnp.cos(ang), jnp.sin(ang)
    d = np.arange(LANES) % HEAD_DIM
    f = d % half
    one, zero = jnp.ones_like(cos[:, f]), jnp.zeros_like(cos[:, f])
    cos_t = jnp.where(d < ROPE_DIM, cos[:, f], one)
    sa = jnp.where((d >= half) & (d < ROPE_DIM), sin[:, f], zero)
    sb = jnp.where(d < half, -sin[:, f], zero)
    return cos_t, sa, sb


def _proj_weights(w_in, w_uk):
    qa, ka, va, qb, ckv, krope, qi, ki, wi, ga, gb = jnp.split(w_in, list(np.cumsum(SPLITS)[:-1]), axis=-1)
    zeros = lambda width: jnp.zeros((D_MODEL, width), w_in.dtype)
    kv = jnp.concatenate([krope, zeros(LANES - ROPE_DIM), ckv], axis=1)
    w = jnp.concatenate([qa, ka, va, qb, qi, ki, ki, kv, ga, gb], axis=1).astype(BF16)
    assert w.shape[1] == PROJ_W
    nope = HEAD_DIM - ROPE_DIM
    per_head = jnp.concatenate([
        jnp.concatenate([jnp.eye(ROPE_DIM, dtype=F32)[None].repeat(N_HEADS, 0),
                         jnp.zeros((N_HEADS, ROPE_DIM, QF_W - ROPE_DIM), F32)], axis=2),
        jnp.concatenate([jnp.zeros((N_HEADS, nope, LANES), F32), jnp.swapaxes(w_uk, 1, 2)], axis=2)], axis=1)
    head_eye = jnp.eye(N_HEADS, dtype=F32)
    wabs = (per_head[:, :, None, :] * head_eye[:, None, :, None]).reshape(HEADS_W, N_HEADS * QF_W).astype(BF16)
    wwi = jnp.concatenate([wi.T, jnp.zeros((16 - N_HEADS, D_MODEL), w_in.dtype)], axis=0).astype(BF16)
    wckv = ckv.T.astype(BF16)
    return w, wabs, wwi, wckv


def _tri(n, strict_lower):
    r = np.arange(n)
    m = (r[None, :] < r[:, None]) if strict_lower else (r[None, :] <= r[:, None])
    return jnp.asarray(m, BF16)


def kernel(x, meta_tokens, norm_mix_g, w_in, w_uk, w_uv, w_up_a, w_up_b, w_o, norm_ffn_g, w_group, b_group,
           w_router, b_router, w1, w3, w2, norm_final_g):
    batch, seq, d = x.shape
    assert d == D_MODEL and seq % LANES == 0 and norm_mix_g.shape[0] == 1
    n = batch * seq
    nq = seq // LANES
    nb = nq + 1
    tp = seq + LANES
    ktop = min(TOPK_MAX, seq // 4)
    x2 = x.reshape(n, d)

    w, wabs, wwi, wckv = _proj_weights(w_in[0], w_uk[0])
    g_mix = norm_mix_g[0][None, :]
    proj_tile = 512 if seq % 512 == 0 else LANES
    px = _proj(x2, g_mix, _rope_tables(N_META + jnp.arange(seq)), w, wabs, wwi, wckv, proj_tile, seq // proj_tile)
    meta = jnp.concatenate([meta_tokens.astype(x.dtype), jnp.zeros((LANES - N_META, d), x.dtype)], axis=0)
    pm = _proj(meta, g_mix, _rope_tables(jnp.arange(LANES)), w, wabs, wwi, wckv, LANES, 1)
    qkv_x, qf_x, qi_x, ki_x, kv_x, gt_x, wit_x, ckvt_x = px
    qkv_m, _, _, ki_m, kv_m, _, _, ckvt_m = pm

    def with_meta(xpart, mpart, rows):
        wdt = xpart.shape[1]
        parts = [jnp.broadcast_to(mpart[None], (batch, LANES, wdt)), xpart.reshape(batch, seq, wdt)]
        if rows > tp:
            parts.append(jnp.zeros((batch, rows - tp, wdt), xpart.dtype))
        return jnp.concatenate(parts, axis=1).reshape(batch * rows, wdt)

    kv_a = with_meta(qkv_x[:, HEADS_W:], qkv_m[:, HEADS_W:], tp)
    su = jnp.concatenate([_tri(LANES, True), jnp.ones((LANES, LANES), BF16)], axis=1)
    ya = _sb(qkv_x, kv_a, su, batch, seq)

    tpk = -(-tp // DSA_KEYS) * DSA_KEYS
    kip = with_meta(ki_x, ki_m, tpk)
    kvp = with_meta(kv_x, kv_m, tpk)
    ckvt = jnp.concatenate([jnp.broadcast_to(ckvt_m[:, None, :], (KV_LATENT, batch, LANES)),
                            ckvt_x.reshape(KV_LATENT, batch, seq),
                            jnp.zeros((KV_LATENT, batch, tpk - tp), BF16)], axis=2)
    ckvt3 = ckvt.reshape(KV_LATENT, batch * tpk // DSA_KEYS, DSA_KEYS).transpose(1, 0, 2)
    wuvt = jnp.swapaxes(w_uv[0], 1, 2).astype(BF16)
    yb = _dsa(qi_x, wit_x, qf_x, kip, kvp, ckvt3, _tri(DSA_KEYS, False), wuvt, batch, seq, ktop)

    wrt = jnp.concatenate([w_router[0], w_group[0],
                           jnp.zeros((d, LANES - N_EXPERTS - N_GROUPS), F32)], axis=1).astype(F32)
    brt = jnp.concatenate([b_router[0], b_group[0], jnp.zeros((LANES - N_EXPERTS - N_GROUPS,), F32)])[None, :]
    h1, u2, r, counts = _merge(x2, ya, yb, gt_x, w_up_a[0].astype(BF16), w_up_b[0].astype(BF16),
                               w_o[0].astype(BF16), norm_ffn_g[0][None, :], wrt, brt.astype(F32),
                               _tri(MERGE_TILE, True))

    cnt = counts[0, :N_EXPERTS].astype(jnp.int32)
    padded = (cnt + FFN_TILE - 1) // FFN_TILE * FFN_TILE
    p_end = jnp.cumsum(padded)
    p_start = p_end - padded
    e_sel = r[:, 0:2].astype(jnp.int32)
    dest = p_start[e_sel] + r[:, 4:6].astype(jnp.int32)
    n_tiles = (2 * n) // FFN_TILE + N_EXPERTS
    tile_expert = jnp.minimum(jnp.searchsorted(p_end, jnp.arange(n_tiles, dtype=jnp.int32) * FFN_TILE,
                                               side='right'), N_EXPERTS - 1).astype(jnp.int32)
    n_used = (p_end[-1] // FFN_TILE).astype(jnp.int32)
    tile_expert = jnp.where(jnp.arange(n_tiles) < n_used, tile_expert, tile_expert[jnp.maximum(n_used - 1, 0)])

    xs0 = jnp.zeros((n_tiles * FFN_TILE, d), F32)
    xs = _dispatch(dest.reshape(n // DISPATCH_TILE, 1, 2 * DISPATCH_TILE), u2, xs0)
    ys = _ffn(tile_expert, n_used[None], xs, w1[0].astype(BF16), w3[0].astype(BF16), w2[0].astype(BF16))
    out = _final(dest.reshape(n // FINAL_TILE, 1, 2 * FINAL_TILE), h1, r, norm_final_g[None, :], ys)
    return out.reshape(batch, seq, d)
```

```python
import functools
import math

import numpy as np
import jax
import jax.numpy as jnp
from jax import lax
from jax.experimental import pallas as pl
from jax.experimental.pallas import tpu as pltpu

D_MODEL = 1024
CHUNK = 64
N_META = 16
HEAD_DIM = 64
ROPE_DIM = 16
ROPE_THETA = 500000.0
N_HEADS = 8
KV_LATENT = 128
IDX_DIM = 64
TOPK_MAX = 256
N_GROUPS = 4
EXP_PER_GROUP = 8
N_EXPERTS = 32
D_EXPERT = 512
NORM_EPS = 1e-6
HEADS_W = N_HEADS * HEAD_DIM
SPLITS = (HEADS_W, HEADS_W, HEADS_W, HEADS_W, KV_LATENT, ROPE_DIM, N_HEADS * IDX_DIM, IDX_DIM, N_HEADS,
          D_MODEL, D_MODEL)

LANES = 128
QF_W = 2 * LANES
INT_MIN = -2 ** 31
EXP_UNDERFLOW = -104.0
ATTN_SCALE = 1.0 / math.sqrt(HEAD_DIM)
VMEM_LIMIT = 56 * 1024 * 1024
DSA_SUB = 4
DSA_KEYS = DSA_SUB * LANES
ATT_SUB = 2
ATT_KEYS = ATT_SUB * LANES
VAL_ROWS = KV_LATENT + 8
SAFE_LOGIT_BOUND = 40.0
FFN_TILE = 256
MERGE_TILE = 512
FINAL_TILE = 256
DISPATCH_TILE = 512

BF16 = jnp.bfloat16
F32 = jnp.float32

_C_QKV = (0, 3 * HEADS_W)
_C_QB = (_C_QKV[1], _C_QKV[1] + HEADS_W)
_C_QI = (_C_QB[1], _C_QB[1] + HEADS_W)
_C_KI = (_C_QI[1], _C_QI[1] + LANES)
_C_KV = (_C_KI[1], _C_KI[1] + QF_W)
_C_G = (_C_KV[1], _C_KV[1] + 2 * D_MODEL)
PROJ_W = _C_G[1]


def _dot(a, b):
    return jnp.dot(a, b, preferred_element_type=F32)


def _dot_t(a, b):
    return lax.dot_general(a, b, (((1,), (1,)), ((), ())), preferred_element_type=F32)


def _rmsnorm(x, g):
    return x * lax.rsqrt(jnp.mean(x * x, axis=-1, keepdims=True) + NORM_EPS) * g


def _rope(x, cos, sa, sb):
    w = x.shape[1]
    n = w // LANES
    if n > 1:
        cos, sa, sb = (jnp.concatenate([t] * n, axis=1) for t in (cos, sa, sb))
    return x * cos + pltpu.roll(x, 8, 1) * sa + pltpu.roll(x, w - 8, 1) * sb


def _proj_kernel(x_ref, g_ref, cos_ref, sa_ref, sb_ref, w_ref, wabs_ref, wwi_ref, wckv_ref,
                 qkv_ref, qf_ref, qi_ref, ki_ref, kv_ref, gt_ref, wit_ref, ckvt_ref):
    ub = _rmsnorm(x_ref[...], g_ref[...]).astype(BF16)
    cos, sa, sb = cos_ref[...], sa_ref[...], sb_ref[...]
    qkv_ref[...] = _dot(ub, w_ref[:, _C_QKV[0]:_C_QKV[1]]).astype(BF16)
    qb = _rope(_dot(ub, w_ref[:, _C_QB[0]:_C_QB[1]]), cos, sa, sb).astype(BF16)
    qf_ref[...] = _dot(qb, wabs_ref[...]).astype(BF16)
    qi_ref[...] = _rope(_dot(ub, w_ref[:, _C_QI[0]:_C_QI[1]]), cos, sa, sb).astype(BF16)
    ki_ref[...] = _rope(_dot(ub, w_ref[:, _C_KI[0]:_C_KI[1]]), cos, sa, sb).astype(BF16)
    kv = _dot(ub, w_ref[:, _C_KV[0]:_C_KV[1]])
    kv_ref[...] = jnp.concatenate([_rope(kv[:, :LANES], cos, sa, sb), kv[:, LANES:]], axis=1).astype(BF16)
    gt_ref[...] = jax.nn.sigmoid(_dot(ub, w_ref[:, _C_G[0]:_C_G[1]])).astype(BF16)
    wit_ref[...] = _dot_t(wwi_ref[...], ub)
    ckvt_ref[...] = _dot_t(wckv_ref[...], ub).astype(BF16)


def _proj(x2, g, tabs, w, wabs, wwi, wckv, tile, tab_tiles):
    n = x2.shape[0]
    row = lambda width: pl.BlockSpec((tile, width), lambda i: (i, 0))
    const = lambda shape: pl.BlockSpec(shape, lambda i: (0, 0))
    tab = pl.BlockSpec((tile, LANES), lambda i: (i % tab_tiles, 0))
    out_shape = (
        jax.ShapeDtypeStruct((n, 3 * HEADS_W), BF16),
        jax.ShapeDtypeStruct((n, N_HEADS * QF_W), BF16),
        jax.ShapeDtypeStruct((n, HEADS_W), BF16),
        jax.ShapeDtypeStruct((n, LANES), BF16),
        jax.ShapeDtypeStruct((n, QF_W), BF16),
        jax.ShapeDtypeStruct((n, 2 * D_MODEL), BF16),
        jax.ShapeDtypeStruct((16, n), F32),
        jax.ShapeDtypeStruct((KV_LATENT, n), BF16),
    )
    out_specs = (row(3 * HEADS_W), row(N_HEADS * QF_W), row(HEADS_W), row(LANES), row(QF_W), row(2 * D_MODEL),
                 pl.BlockSpec((16, tile), lambda i: (0, i)), pl.BlockSpec((KV_LATENT, tile), lambda i: (0, i)))
    return pl.pallas_call(
        _proj_kernel,
        grid=(n // tile,),
        in_specs=[row(D_MODEL), const((1, D_MODEL)), tab, tab, tab, const(w.shape), const(wabs.shape),
                  const(wwi.shape), const(wckv.shape)],
        out_specs=out_specs,
        out_shape=out_shape,
        compiler_params=pltpu.CompilerParams(dimension_semantics=("arbitrary",), vmem_limit_bytes=VMEM_LIMIT),
        name="proj",
    )(x2, g, *tabs, w, wabs, wwi, wckv)


def _sb_kernel(q_ref, k_ref, v_ref, su_ref, o_ref, acc_ref, carry_ref):
    ib = pl.program_id(1) + 1
    lane = lax.broadcasted_iota(jnp.int32, (LANES, LANES), 1)
    row = lax.broadcasted_iota(jnp.int32, (LANES, LANES), 0)
    low = lane < HEAD_DIM
    high = jnp.logical_not(low)
    su = su_ref[...]
    qpos = ib * LANES + row
    acc_ref[...] = jnp.zeros_like(acc_ref)
    carry_ref[...] = jnp.zeros_like(carry_ref)

    def body(st):
        j, _ = st
        rows = pl.ds(pl.multiple_of(j * LANES, LANES), LANES)
        kpos = j * LANES + lane
        mask = (kpos < qpos) & ((kpos < N_META) | (kpos >= LANES))
        top = None
        for p in range(N_HEADS // 2):
            cols = slice(p * LANES, (p + 1) * LANES)
            q2, kb, vb = q_ref[:, cols], k_ref[rows, cols], v_ref[rows, cols]
            zero = jnp.zeros_like(q2)
            pv = None
            for half, keep in enumerate((low, high)):
                h = 2 * p + half
                z = _dot_t(jnp.where(keep, q2, zero), kb)
                ls_pos = jnp.minimum(z, 0.0) - jnp.log1p(jnp.exp(-jnp.abs(z)))
                log_keep = jnp.where(mask, ls_pos - z, 0.0)
                hi = log_keep.astype(BF16)
                lo = (log_keep - hi.astype(F32)).astype(BF16)
                cs = _dot(hi, su) + _dot(lo, su)
                carry = carry_ref[h]
                a = jnp.where(mask, jnp.exp(ls_pos + cs[:, :LANES] + carry), 0.0)
                d = _dot(a.astype(BF16), jnp.where(keep, vb, zero))
                pv = d if pv is None else pv + d
                carry = carry + cs[:, LANES:]
                carry_ref[h] = carry
                top = carry if top is None else jnp.maximum(top, carry)
            acc_ref[p] += pv
        return j - 1, jnp.max(top)

    def cond(st):
        return (st[0] >= 0) & (st[1] > EXP_UNDERFLOW)

    lax.while_loop(cond, body, (ib, jnp.float32(0.0)))
    for p in range(N_HEADS // 2):
        o_ref[:, p * LANES:(p + 1) * LANES] = acc_ref[p].astype(BF16)


def _sb(qkv_x, kvp, su, batch, seq):
    nq = seq // LANES
    tp = seq + LANES
    return pl.pallas_call(
        _sb_kernel,
        grid=(batch, nq),
        in_specs=[pl.BlockSpec((LANES, HEADS_W), lambda b, i: (b * nq + i, 0)),
                  pl.BlockSpec((tp, HEADS_W), lambda b, i: (b, 0)),
                  pl.BlockSpec((tp, HEADS_W), lambda b, i: (b, 1)),
                  pl.BlockSpec(su.shape, lambda b, i: (0, 0))],
        out_specs=pl.BlockSpec((LANES, HEADS_W), lambda b, i: (b * nq + i, 0)),
        out_shape=jax.ShapeDtypeStruct((batch * seq, HEADS_W), BF16),
        scratch_shapes=[pltpu.VMEM((N_HEADS // 2, LANES, LANES), F32), pltpu.VMEM((N_HEADS, LANES, LANES), F32)],
        compiler_params=pltpu.CompilerParams(dimension_semantics=("arbitrary",) * 2, vmem_limit_bytes=VMEM_LIMIT),
        name="sb",
    )(qkv_x, kvp, kvp, su)


def _dsa_kernel(qi_ref, wit_ref, qf_ref, ki_ref, kv_ref, ckvt_ref, lt_ref, wuvt_ref, o_ref,
                sk_ref, acc_ref, m_ref, kvn_ref, qim_ref, qfp_ref, ybt_ref, *, ktop):
    ib = pl.program_id(1) + 1
    nt = (ib + DSA_SUB) // DSA_SUB
    lane = lax.broadcasted_iota(jnp.int32, (LANES, LANES), 1)
    klane = lax.broadcasted_iota(jnp.int32, (DSA_KEYS, LANES), 1)
    krow = lax.broadcasted_iota(jnp.int32, (DSA_KEYS, LANES), 0)
    low = lane < HEAD_DIM
    pairs = range(N_HEADS // 2)
    for p in pairs:
        blk = qi_ref[:, p * LANES:(p + 1) * LANES]
        zero = jnp.zeros_like(blk)
        qim_ref[p, :LANES, :] = jnp.where(low, blk, zero)
        qim_ref[p, LANES:, :] = jnp.where(low, zero, blk)
        qfp_ref[p, :LANES, :] = qf_ref[:, 2 * p * QF_W:(2 * p + 1) * QF_W]
        qfp_ref[p, LANES:, :] = qf_ref[:, (2 * p + 1) * QF_W:(2 * p + 2) * QF_W]
    wit = wit_ref[...]
    wit2 = [jnp.concatenate([wit[2 * p:2 * p + 1, :], wit[2 * p + 1:2 * p + 2, :]], axis=1) for p in pairs]
    cq = 1 + ((ib * LANES + klane - LANES) >> 6)

    def tile(t):
        return pl.ds(pl.multiple_of(t * DSA_KEYS, DSA_KEYS), DSA_KEYS)

    def scores(t, c):
        kb = ki_ref[tile(t), :]
        s = jnp.zeros((DSA_KEYS, LANES), F32)
        for p in pairs:
            part = wit2[p] * jnp.maximum(_dot_t(kb, qim_ref[p]), 0.0)
            s = s + part[:, :LANES] + part[:, LANES:]
        s = jnp.where(s == 0.0, 0.0, s)
        bits = lax.bitcast_convert_type(s, jnp.int32)
        key = bits ^ ((bits >> 31) & 0x7FFFFFFF)
        rk = t * DSA_KEYS + krow
        ck = jnp.where(rk < LANES, 0, 1 + ((rk - LANES) >> 6))
        adm = ((rk < N_META) | (rk >= LANES)) & (ck <= cq)
        sk_ref[tile(t), :] = jnp.where(adm, key, INT_MIN)
        return c

    lax.fori_loop(0, nt, scores, 0)

    def count(pred):
        def step(t, acc):
            hit = pred(sk_ref[tile(t), :]).astype(jnp.int32)
            for u in range(DSA_SUB):
                acc = acc + hit[u * LANES:(u + 1) * LANES]
            return acc
        acc = lax.fori_loop(0, nt, step, jnp.zeros((LANES, LANES), jnp.int32))
        return jnp.sum(acc, axis=0, keepdims=True)

    zero_row = jnp.zeros((1, LANES), jnp.int32)
    thr = jnp.where(count(lambda k: k >= zero_row) >= ktop, zero_row, jnp.full((1, LANES), INT_MIN, jnp.int32))

    def bit_step(it, thr):
        cand = thr | jnp.left_shift(jnp.int32(1), 30 - it)
        return jnp.where(count(lambda k: k >= cand) >= ktop, cand, thr)

    thr = lax.fori_loop(0, 31, bit_step, thr)
    excess = jnp.where(thr == INT_MIN, 0, count(lambda k: k >= thr) - ktop)

    @pl.when(pl.program_id(1) == 0)
    def _():
        def widest(t, m):
            kvf = kv_ref[tile(t), :].astype(F32)
            return jnp.maximum(m, jnp.max(jnp.sum(kvf * kvf, axis=1, keepdims=True)))
        kvn_ref[...] = jnp.full(kvn_ref.shape, lax.fori_loop(0, kv_ref.shape[0] // DSA_KEYS, widest, jnp.float32(0.0)))

    ones = jnp.ones((8, QF_W), BF16)
    bounds = []
    for p in pairs:
        qp = qfp_ref[p].astype(F32)
        qn2 = _dot_t(ones, (qp * qp).astype(BF16))[0:1, :]
        kvn = kvn_ref[0:1, :]
        bounds.append(jnp.sqrt(qn2 * jnp.concatenate([kvn, kvn], axis=1)) * 1.05 + 1e-6)
    bound_max = functools.reduce(jnp.maximum, bounds)
    fast = (jnp.max(excess) == 0) & (jnp.max(bound_max) <= SAFE_LOGIT_BOUND)
    acc_ref[...] = jnp.zeros_like(acc_ref)

    def twice(a):
        return jnp.concatenate([a, a], axis=1)

    def atile(t):
        return pl.ds(pl.multiple_of(t * ATT_KEYS, ATT_KEYS), ATT_KEYS)

    nta = (ib + ATT_SUB) // ATT_SUB

    @pl.when(fast)
    def _():
        def attend(t, c):
            skt = sk_ref[atile(t), :]
            sel = twice(jnp.where((skt >= thr) & (skt > INT_MIN), 1.0, 0.0).astype(BF16))
            kvb = kv_ref[atile(t), :]
            cb = ckvt_ref[t]
            for p in pairs:
                prob = jnp.exp(_dot_t(kvb, qfp_ref[p]) - bounds[p]).astype(BF16) * sel
                acc_ref[p] += _dot(cb, prob)
            return c

        lax.fori_loop(0, nta, attend, 0)

    @pl.when(jnp.logical_not(fast))
    def _():
        need = (ktop - count(lambda k: k > thr)).astype(F32)
        m_ref[...] = jnp.full_like(m_ref, -1e29)
        lt = lt_ref[...]

        def attend(t, tie_carry):
            skt = sk_ref[atile(t), :]
            tie = (skt == thr) & (skt > INT_MIN)
            rank = _dot(lt, jnp.where(tie, 1.0, 0.0).astype(BF16)) + tie_carry
            sel = twice((skt > thr) | (tie & (rank <= need)))
            kvb = kv_ref[atile(t), :]
            cb = ckvt_ref[t]
            for p in pairs:
                s = jnp.where(sel, _dot_t(kvb, qfp_ref[p]), -1e30)
                m_old = m_ref[p:p + 1, :]
                m_new = jnp.maximum(m_old, jnp.max(s, axis=0, keepdims=True))
                prob = jnp.exp(s - m_new).astype(BF16)
                acc_ref[p] = acc_ref[p] * jnp.exp(m_old - m_new) + _dot(cb, prob)
                m_ref[p:p + 1, :] = m_new
            return rank[ATT_KEYS - 1:ATT_KEYS, :]

        lax.fori_loop(0, nta, attend, jnp.zeros((1, LANES), F32))

    for h in range(N_HEADS):
        cols = slice((h % 2) * LANES, (h % 2 + 1) * LANES)
        o = (acc_ref[h // 2, :KV_LATENT, cols] / acc_ref[h // 2, KV_LATENT:KV_LATENT + 1, cols]).astype(BF16)
        ybt_ref[h * HEAD_DIM:(h + 1) * HEAD_DIM, :] = _dot(wuvt_ref[h], o)
    o_ref[...] = ybt_ref[...].T.astype(BF16)


def _dsa(qi, wit, qf, kip, kvp, ckvt3, lt, wuvt, batch, seq, ktop):
    nq = seq // LANES
    tpk = kip.shape[0] // batch
    ntk = tpk // ATT_KEYS
    return pl.pallas_call(
        functools.partial(_dsa_kernel, ktop=ktop),
        grid=(batch, nq),
        in_specs=[pl.BlockSpec((LANES, HEADS_W), lambda b, i: (b * nq + i, 0)),
                  pl.BlockSpec((16, LANES), lambda b, i: (0, b * nq + i)),
                  pl.BlockSpec((LANES, N_HEADS * QF_W), lambda b, i: (b * nq + i, 0)),
                  pl.BlockSpec((tpk, LANES), lambda b, i: (b, 0)),
                  pl.BlockSpec((tpk, QF_W), lambda b, i: (b, 0)),
                  pl.BlockSpec((ntk, VAL_ROWS, ATT_KEYS), lambda b, i: (b, 0, 0)),
                  pl.BlockSpec(lt.shape, lambda b, i: (0, 0)),
                  pl.BlockSpec(wuvt.shape, lambda b, i: (0, 0, 0))],
        out_specs=pl.BlockSpec((LANES, HEADS_W), lambda b, i: (b * nq + i, 0)),
        out_shape=jax.ShapeDtypeStruct((batch * seq, HEADS_W), BF16),
        scratch_shapes=[pltpu.VMEM((tpk, LANES), jnp.int32),
                        pltpu.VMEM((N_HEADS // 2, VAL_ROWS, 2 * LANES), F32),
                        pltpu.VMEM((8, 2 * LANES), F32),
                        pltpu.VMEM((8, LANES), F32),
                        pltpu.VMEM((N_HEADS // 2, 2 * LANES, LANES), BF16),
                        pltpu.VMEM((N_HEADS // 2, 2 * LANES, QF_W), BF16),
                        pltpu.VMEM((HEADS_W, LANES), F32)],
        compiler_params=pltpu.CompilerParams(dimension_semantics=("arbitrary",) * 2, vmem_limit_bytes=VMEM_LIMIT),
        name="dsa",
    )(qi, wit, qf, kip, kvp, ckvt3, lt, wuvt)


def _merge_kernel(x_ref, ya_ref, yb_ref, gt_ref, wua_ref, wub_ref, wo_ref, g_ref, wrt_ref, brt_ref, slt_ref,
                  h1_ref, u2_ref, r_ref, cnt_ref, carry_ref):
    @pl.when(pl.program_id(0) == 0)
    def _():
        carry_ref[...] = jnp.zeros_like(carry_ref)

    gt = gt_ref[...].astype(F32)
    z = gt[:, :D_MODEL] * _dot(ya_ref[...], wua_ref[...]) + gt[:, D_MODEL:] * _dot(yb_ref[...], wub_ref[...])
    h1 = x_ref[...] + _dot(z.astype(BF16), wo_ref[...])
    h1_ref[...] = h1
    u2 = _rmsnorm(h1, g_ref[...])
    u2_ref[...] = u2
    lg = jnp.dot(u2, wrt_ref[...], preferred_element_type=F32, precision=lax.Precision.HIGHEST) + brt_ref[...]

    tm = lg.shape[0]
    lane = lax.broadcasted_iota(jnp.int32, (tm, LANES), 1)
    big = jnp.int32(LANES)

    def softmax(mask):
        x = jnp.where(mask, lg, -jnp.inf)
        e = jnp.where(mask, jnp.exp(x - jnp.max(x, axis=-1, keepdims=True)), 0.0)
        return e / jnp.sum(e, axis=-1, keepdims=True)

    def top1(p, mask):
        best = jnp.max(jnp.where(mask, p, -1.0), axis=-1, keepdims=True)
        idx = jnp.min(jnp.where(mask & (p == best), lane, big), axis=-1, keepdims=True)
        return best, idx

    gmask = (lane >= N_EXPERTS) & (lane < N_EXPERTS + N_GROUPS)
    p_grp, g_lane = top1(softmax(gmask), gmask)
    first = (g_lane - N_EXPERTS) * EXP_PER_GROUP
    emask = (lane >= first) & (lane < first + EXP_PER_GROUP)
    ep = softmax(emask)
    p0, e0 = top1(ep, emask)
    rest = emask & (lane != e0)
    p1, e1 = top1(ep, rest)
    den = p0 + p1
    w0 = p_grp * p0 / den
    w1 = p_grp * p1 / den

    oh0 = lane == e0
    oh1 = lane == e1
    both = (oh0 | oh1).astype(BF16)
    before = _dot(slt_ref[...], both) + carry_ref[...]
    rank0 = jnp.sum(jnp.where(oh0, before, 0.0), axis=-1, keepdims=True)
    rank1 = jnp.sum(jnp.where(oh1, before, 0.0), axis=-1, keepdims=True)
    carry = carry_ref[...] + jnp.sum(both.astype(F32), axis=0, keepdims=True)
    carry_ref[...] = carry
    cnt_ref[...] = carry

    cols = (e0.astype(F32), e1.astype(F32), w0, w1, rank0, rank1)
    r = jnp.zeros((tm, LANES), F32)
    for c, v in enumerate(cols):
        r = jnp.where(lane == c, v, r)
    r_ref[...] = r


def _merge(x2, ya, yb, gt, wua, wub, wo, g, wrt, brt, slt):
    n = x2.shape[0]
    tm = MERGE_TILE
    row = lambda width: pl.BlockSpec((tm, width), lambda i: (i, 0))
    const = lambda a: pl.BlockSpec(a.shape, lambda i: (0,) * a.ndim)
    return pl.pallas_call(
        _merge_kernel,
        grid=(n // tm,),
        in_specs=[row(D_MODEL), row(HEADS_W), row(HEADS_W), row(2 * D_MODEL), const(wua), const(wub), const(wo),
                  const(g), const(wrt), const(brt), const(slt)],
        out_specs=(row(D_MODEL), row(D_MODEL), row(LANES), pl.BlockSpec((1, LANES), lambda i: (0, 0))),
        out_shape=(jax.ShapeDtypeStruct((n, D_MODEL), F32), jax.ShapeDtypeStruct((n, D_MODEL), F32),
                   jax.ShapeDtypeStruct((n, LANES), F32), jax.ShapeDtypeStruct((1, LANES), F32)),
        scratch_shapes=[pltpu.VMEM((1, LANES), F32)],
        compiler_params=pltpu.CompilerParams(dimension_semantics=("arbitrary",), vmem_limit_bytes=VMEM_LIMIT),
        name="merge",
    )(x2, ya, yb, gt, wua, wub, wo, g, wrt, brt, slt)


def _row_copy(src, src_row, dst, dst_row, sem):
    return pltpu.make_async_copy(src.at[pl.ds(src_row, 1)], dst.at[pl.ds(dst_row, 1)], sem)


def _dispatch_kernel(dest_ref, u_ref, xs_in_ref, xs_ref, sem):
    del xs_in_ref

    def issue(t, c):
        for s in range(2):
            _row_copy(u_ref, t, xs_ref, dest_ref[0, 0, 2 * t + s], sem).start()
        return c

    def drain(t, c):
        for s in range(2):
            _row_copy(u_ref, t, xs_ref, dest_ref[0, 0, 2 * t + s], sem).wait()
        return c

    lax.fori_loop(0, DISPATCH_TILE, issue, 0)
    lax.fori_loop(0, DISPATCH_TILE, drain, 0)


def _dispatch(dest3, u2, xs0):
    n = u2.shape[0]
    return pl.pallas_call(
        _dispatch_kernel,
        grid=(n // DISPATCH_TILE,),
        in_specs=[pl.BlockSpec((1, 1, 2 * DISPATCH_TILE), lambda i: (i, 0, 0), memory_space=pltpu.SMEM),
                  pl.BlockSpec((DISPATCH_TILE, D_MODEL), lambda i: (i, 0)), pl.BlockSpec(memory_space=pl.ANY)],
        out_specs=pl.BlockSpec(memory_space=pl.ANY),
        out_shape=jax.ShapeDtypeStruct(xs0.shape, xs0.dtype),
        scratch_shapes=[pltpu.SemaphoreType.DMA],
        input_output_aliases={2: 0},
        compiler_params=pltpu.CompilerParams(dimension_semantics=("arbitrary",), has_side_effects=True),
        name="dispatch",
    )(dest3, u2, xs0)


def _ffn_kernel(te_ref, nu_ref, xs_ref, w1_ref, w3_ref, w2_ref, ys_ref):
    del te_ref

    @pl.when(pl.program_id(0) < nu_ref[0])
    def _():
        x = xs_ref[...].astype(BF16)
        hid = jax.nn.silu(_dot(x, w1_ref[0])) * _dot(x, w3_ref[0])
        ys_ref[...] = _dot(hid.astype(BF16), w2_ref[0])

    @pl.when(pl.program_id(0) >= nu_ref[0])
    def _():
        ys_ref[...] = jnp.zeros_like(ys_ref)


def _ffn(tile_expert, n_used, xs, w1, w3, w2):
    p = xs.shape[0]
    grid_spec = pltpu.PrefetchScalarGridSpec(
        num_scalar_prefetch=2,
        grid=(p // FFN_TILE,),
        in_specs=[pl.BlockSpec((FFN_TILE, D_MODEL), lambda t, te, nu: (t, 0)),
                  pl.BlockSpec((1, D_MODEL, D_EXPERT), lambda t, te, nu: (te[t], 0, 0)),
                  pl.BlockSpec((1, D_MODEL, D_EXPERT), lambda t, te, nu: (te[t], 0, 0)),
                  pl.BlockSpec((1, D_EXPERT, D_MODEL), lambda t, te, nu: (te[t], 0, 0))],
        out_specs=pl.BlockSpec((FFN_TILE, D_MODEL), lambda t, te, nu: (t, 0)),
    )
    return pl.pallas_call(
        _ffn_kernel,
        grid_spec=grid_spec,
        out_shape=jax.ShapeDtypeStruct((p, D_MODEL), F32),
        compiler_params=pltpu.CompilerParams(dimension_semantics=("arbitrary",), vmem_limit_bytes=VMEM_LIMIT),
        name="ffn",
    )(tile_expert, n_used, xs, w1, w3, w2)


def _final_kernel(dest_ref, h1_ref, r_ref, g_ref, ys_ref, o_ref, buf_ref, sem):
    def issue(t, c):
        for s in range(2):
            _row_copy(ys_ref, dest_ref[0, 0, 2 * t + s], buf_ref.at[s], t, sem).start()
        return c

    def drain(t, c):
        for s in range(2):
            _row_copy(ys_ref, dest_ref[0, 0, 2 * t + s], buf_ref.at[s], t, sem).wait()
        return c

    lax.fori_loop(0, FINAL_TILE, issue, 0)
    lax.fori_loop(0, FINAL_TILE, drain, 0)
    r = r_ref[...]
    moe = r[:, 2:3] * buf_ref[0] + r[:, 3:4] * buf_ref[1]
    o_ref[...] = _rmsnorm(h1_ref[...] + moe, g_ref[...])


def _final(dest3, h1, r, g, ys):
    n = h1.shape[0]
    tm = FINAL_TILE
    return pl.pallas_call(
        _final_kernel,
        grid=(n // tm,),
        in_specs=[pl.BlockSpec((1, 1, 2 * tm), lambda i: (i, 0, 0), memory_space=pltpu.SMEM),
                  pl.BlockSpec((tm, D_MODEL), lambda i: (i, 0)),
                  pl.BlockSpec((tm, LANES), lambda i: (i, 0)),
                  pl.BlockSpec((1, D_MODEL), lambda i: (0, 0)),
                  pl.BlockSpec(memory_space=pl.ANY)],
        out_specs=pl.BlockSpec((tm, D_MODEL), lambda i: (i, 0)),
        out_shape=jax.ShapeDtypeStruct((n, D_MODEL), F32),
        scratch_shapes=[pltpu.VMEM((2, tm, D_MODEL), F32), pltpu.SemaphoreType.DMA],
        compiler_params=pltpu.CompilerParams(dimension_semantics=("arbitrary",), vmem_limit_bytes=VMEM_LIMIT),
        name="final",
    )(dest3, h1, r, g, ys)


def _rope_tables(pos):
    half = ROPE_DIM // 2
    inv = ROPE_THETA ** (-jnp.arange(half, dtype=F32) / half)
    ang = pos.astype(F32)[:, None] * inv[None, :]
    cos, sin = jnp.cos(ang), jnp.sin(ang)
    d = np.arange(LANES) % HEAD_DIM
    f = d % half
    one, zero = jnp.ones_like(cos[:, f]), jnp.zeros_like(cos[:, f])
    cos_t = jnp.where(d < ROPE_DIM, cos[:, f], one)
    sa = jnp.where((d >= half) & (d < ROPE_DIM), sin[:, f], zero)
    sb = jnp.where(d < half, -sin[:, f], zero)
    return cos_t, sa, sb


def _proj_weights(w_in, w_uk):
    qa, ka, va, qb, ckv, krope, qi, ki, wi, ga, gb = jnp.split(w_in, list(np.cumsum(SPLITS)[:-1]), axis=-1)
    zeros = lambda width: jnp.zeros((D_MODEL, width), w_in.dtype)
    kv = jnp.concatenate([krope, zeros(LANES - ROPE_DIM), ckv], axis=1)
    w = jnp.concatenate([qa * ATTN_SCALE, ka, va, qb, qi, ki, ki, kv, ga, gb], axis=1).astype(BF16)
    assert w.shape[1] == PROJ_W
    nope = HEAD_DIM - ROPE_DIM
    per_head = jnp.concatenate([
        jnp.concatenate([jnp.eye(ROPE_DIM, dtype=F32)[None].repeat(N_HEADS, 0),
                         jnp.zeros((N_HEADS, ROPE_DIM, QF_W - ROPE_DIM), F32)], axis=2),
        jnp.concatenate([jnp.zeros((N_HEADS, nope, LANES), F32), jnp.swapaxes(w_uk, 1, 2)], axis=2)], axis=1)
    head_eye = jnp.eye(N_HEADS, dtype=F32)
    wabs = (per_head[:, :, None, :] * (head_eye * ATTN_SCALE)[:, None, :, None])
    wabs = wabs.reshape(HEADS_W, N_HEADS * QF_W).astype(BF16)
    wwi = jnp.concatenate([wi.T, jnp.zeros((16 - N_HEADS, D_MODEL), w_in.dtype)], axis=0).astype(BF16)
    wckv = ckv.T.astype(BF16)
    return w, wabs, wwi, wckv


def _tri(n, strict_lower):
    r = np.arange(n)
    m = (r[None, :] < r[:, None]) if strict_lower else (r[None, :] <= r[:, None])
    return jnp.asarray(m, BF16)


def kernel(x, meta_tokens, norm_mix_g, w_in, w_uk, w_uv, w_up_a, w_up_b, w_o, norm_ffn_g, w_group, b_group,
           w_router, b_router, w1, w3, w2, norm_final_g):
    batch, seq, d = x.shape
    assert d == D_MODEL and seq % LANES == 0 and norm_mix_g.shape[0] == 1
    n = batch * seq
    nq = seq // LANES
    nb = nq + 1
    tp = seq + LANES
    ktop = min(TOPK_MAX, seq // 4)
    x2 = x.reshape(n, d)

    w, wabs, wwi, wckv = _proj_weights(w_in[0], w_uk[0])
    g_mix = norm_mix_g[0][None, :]
    proj_tile = 512 if seq % 512 == 0 else LANES
    px = _proj(x2, g_mix, _rope_tables(N_META + jnp.arange(seq)), w, wabs, wwi, wckv, proj_tile, seq // proj_tile)
    meta = jnp.concatenate([meta_tokens.astype(x.dtype), jnp.zeros((LANES - N_META, d), x.dtype)], axis=0)
    pm = _proj(meta, g_mix, _rope_tables(jnp.arange(LANES)), w, wabs, wwi, wckv, LANES, 1)
    qkv_x, qf_x, qi_x, ki_x, kv_x, gt_x, wit_x, ckvt_x = px
    qkv_m, _, _, ki_m, kv_m, _, _, ckvt_m = pm

    def with_meta(xpart, mpart, rows):
        wdt = xpart.shape[1]
        parts = [jnp.broadcast_to(mpart[None], (batch, LANES, wdt)), xpart.reshape(batch, seq, wdt)]
        if rows > tp:
            parts.append(jnp.zeros((batch, rows - tp, wdt), xpart.dtype))
        return jnp.concatenate(parts, axis=1).reshape(batch * rows, wdt)

    kv_a = with_meta(qkv_x[:, HEADS_W:], qkv_m[:, HEADS_W:], tp)
    su = jnp.concatenate([_tri(LANES, True), jnp.ones((LANES, LANES), BF16)], axis=1)
    ya = _sb(qkv_x, kv_a, su, batch, seq)

    tpk = -(-tp // DSA_KEYS) * DSA_KEYS
    kip = with_meta(ki_x, ki_m, tpk)
    kvp = with_meta(kv_x, kv_m, tpk)
    ckvt = jnp.concatenate([jnp.broadcast_to(ckvt_m[:, None, :], (KV_LATENT, batch, LANES)),
                            ckvt_x.reshape(KV_LATENT, batch, seq),
                            jnp.zeros((KV_LATENT, batch, tpk - tp), BF16)], axis=2)
    ones_row = jnp.concatenate([jnp.ones((1, batch, tpk), BF16),
                                jnp.zeros((VAL_ROWS - KV_LATENT - 1, batch, tpk), BF16)], axis=0)
    ckvt = jnp.concatenate([ckvt, ones_row], axis=0)
    ckvt3 = ckvt.reshape(VAL_ROWS, batch * tpk // ATT_KEYS, ATT_KEYS).transpose(1, 0, 2)
    wuvt = jnp.swapaxes(w_uv[0], 1, 2).astype(BF16)
    yb = _dsa(qi_x, wit_x, qf_x, kip, kvp, ckvt3, _tri(ATT_KEYS, False), wuvt, batch, seq, ktop)

    wrt = jnp.concatenate([w_router[0], w_group[0],
                           jnp.zeros((d, LANES - N_EXPERTS - N_GROUPS), F32)], axis=1).astype(F32)
    brt = jnp.concatenate([b_router[0], b_group[0], jnp.zeros((LANES - N_EXPERTS - N_GROUPS,), F32)])[None, :]
    h1, u2, r, counts = _merge(x2, ya, yb, gt_x, w_up_a[0].astype(BF16), w_up_b[0].astype(BF16),
                               w_o[0].astype(BF16), norm_ffn_g[0][None, :], wrt, brt.astype(F32),
                               _tri(MERGE_TILE, True))

    cnt = counts[0, :N_EXPERTS].astype(jnp.int32)
    padded = (cnt + FFN_TILE - 1) // FFN_TILE * FFN_TILE
    p_end = jnp.cumsum(padded)
    p_start = p_end - padded
    e_sel = r[:, 0:2].astype(jnp.int32)
    dest = p_start[e_sel] + r[:, 4:6].astype(jnp.int32)
    n_tiles = (2 * n) // FFN_TILE + N_EXPERTS
    tile_row = jnp.arange(n_tiles, dtype=jnp.int32) * FFN_TILE
    tile_expert = jnp.minimum(jnp.sum((p_end[None, :] <= tile_row[:, None]).astype(jnp.int32), axis=1),
                              N_EXPERTS - 1)
    n_used = (p_end[-1] // FFN_TILE).astype(jnp.int32)
    tile_expert = jnp.where(jnp.arange(n_tiles) < n_used, tile_expert, tile_expert[jnp.maximum(n_used - 1, 0)])

    xs0 = jnp.zeros((n_tiles * FFN_TILE, d), F32)
    xs = _dispatch(dest.reshape(n // DISPATCH_TILE, 1, 2 * DISPATCH_TILE), u2, xs0)
    ys = _ffn(tile_expert, n_used[None], xs, w1[0].astype(BF16), w3[0].astype(BF16), w2[0].astype(BF16))
    out = _final(dest.reshape(n // FINAL_TILE, 1, 2 * FINAL_TILE), h1, r, norm_final_g[None, :], ys)
    return out.reshape(batch, seq, d)
```

```python
import functools
import math

import numpy as np
import jax
import jax.numpy as jnp
from jax import lax
from jax.experimental import pallas as pl
from jax.experimental.pallas import tpu as pltpu

D_MODEL = 1024
CHUNK = 64
N_META = 16
HEAD_DIM = 64
ROPE_DIM = 16
ROPE_THETA = 500000.0
N_HEADS = 8
KV_LATENT = 128
IDX_DIM = 64
TOPK_MAX = 256
N_GROUPS = 4
EXP_PER_GROUP = 8
N_EXPERTS = 32
D_EXPERT = 512
NORM_EPS = 1e-6
HEADS_W = N_HEADS * HEAD_DIM
SPLITS = (HEADS_W, HEADS_W, HEADS_W, HEADS_W, KV_LATENT, ROPE_DIM, N_HEADS * IDX_DIM, IDX_DIM, N_HEADS,
          D_MODEL, D_MODEL)

LANES = 128
QF_W = 2 * LANES
INT_MIN = -2 ** 31
EXP_UNDERFLOW = -104.0
ATTN_SCALE = 1.0 / math.sqrt(HEAD_DIM)
VMEM_LIMIT = 56 * 1024 * 1024
DSA_SUB = 4
DSA_KEYS = DSA_SUB * LANES
ATT_SUB = 2
ATT_KEYS = ATT_SUB * LANES
VAL_ROWS = KV_LATENT + 8
SAFE_LOGIT_BOUND = 40.0
FFN_TILE = 256
MERGE_TILE = 512
FINAL_TILE = 256
DISPATCH_TILE = 512

BF16 = jnp.bfloat16
F32 = jnp.float32

_C_QKV = (0, 3 * HEADS_W)
_C_QB = (_C_QKV[1], _C_QKV[1] + HEADS_W)
_C_QI = (_C_QB[1], _C_QB[1] + HEADS_W)
_C_KI = (_C_QI[1], _C_QI[1] + LANES)
_C_KV = (_C_KI[1], _C_KI[1] + QF_W)
_C_G = (_C_KV[1], _C_KV[1] + 2 * D_MODEL)
PROJ_W = _C_G[1]


def _dot(a, b):
    return jnp.dot(a, b, preferred_element_type=F32)


def _dot_t(a, b):
    return lax.dot_general(a, b, (((1,), (1,)), ((), ())), preferred_element_type=F32)


def _rmsnorm(x, g):
    return x * lax.rsqrt(jnp.mean(x * x, axis=-1, keepdims=True) + NORM_EPS) * g


def _rope(x, cos, sa, sb):
    w = x.shape[1]
    n = w // LANES
    if n > 1:
        cos, sa, sb = (jnp.concatenate([t] * n, axis=1) for t in (cos, sa, sb))
    return x * cos + pltpu.roll(x, 8, 1) * sa + pltpu.roll(x, w - 8, 1) * sb


def _proj_kernel(x_ref, g_ref, cos_ref, sa_ref, sb_ref, w_ref, wabs_ref, wwi_ref, wckv_ref,
                 qkv_ref, qf_ref, qi_ref, ki_ref, kv_ref, gt_ref, wit_ref, ckvt_ref):
    ub = _rmsnorm(x_ref[...], g_ref[...]).astype(BF16)
    cos, sa, sb = cos_ref[...], sa_ref[...], sb_ref[...]
    qkv_ref[...] = _dot(ub, w_ref[:, _C_QKV[0]:_C_QKV[1]]).astype(BF16)
    qb = _rope(_dot(ub, w_ref[:, _C_QB[0]:_C_QB[1]]), cos, sa, sb).astype(BF16)
    qf_ref[...] = _dot(qb, wabs_ref[...]).astype(BF16)
    qi_ref[...] = _rope(_dot(ub, w_ref[:, _C_QI[0]:_C_QI[1]]), cos, sa, sb).astype(BF16)
    ki_ref[...] = _rope(_dot(ub, w_ref[:, _C_KI[0]:_C_KI[1]]), cos, sa, sb).astype(BF16)
    kv = _dot(ub, w_ref[:, _C_KV[0]:_C_KV[1]])
    kv_ref[...] = jnp.concatenate([_rope(kv[:, :LANES], cos, sa, sb), kv[:, LANES:]], axis=1).astype(BF16)
    gt_ref[...] = jax.nn.sigmoid(_dot(ub, w_ref[:, _C_G[0]:_C_G[1]])).astype(BF16)
    wit_ref[...] = _dot_t(wwi_ref[...], ub)
    ckvt_ref[...] = _dot_t(wckv_ref[...], ub).astype(BF16)


def _proj(x2, g, tabs, w, wabs, wwi, wckv, tile, tab_tiles):
    n = x2.shape[0]
    row = lambda width: pl.BlockSpec((tile, width), lambda i: (i, 0))
    const = lambda shape: pl.BlockSpec(shape, lambda i: (0, 0))
    tab = pl.BlockSpec((tile, LANES), lambda i: (i % tab_tiles, 0))
    out_shape = (
        jax.ShapeDtypeStruct((n, 3 * HEADS_W), BF16),
        jax.ShapeDtypeStruct((n, N_HEADS * QF_W), BF16),
        jax.ShapeDtypeStruct((n, HEADS_W), BF16),
        jax.ShapeDtypeStruct((n, LANES), BF16),
        jax.ShapeDtypeStruct((n, QF_W), BF16),
        jax.ShapeDtypeStruct((n, 2 * D_MODEL), BF16),
        jax.ShapeDtypeStruct((16, n), F32),
        jax.ShapeDtypeStruct((KV_LATENT, n), BF16),
    )
    out_specs = (row(3 * HEADS_W), row(N_HEADS * QF_W), row(HEADS_W), row(LANES), row(QF_W), row(2 * D_MODEL),
                 pl.BlockSpec((16, tile), lambda i: (0, i)), pl.BlockSpec((KV_LATENT, tile), lambda i: (0, i)))
    return pl.pallas_call(
        _proj_kernel,
        grid=(n // tile,),
        in_specs=[row(D_MODEL), const((1, D_MODEL)), tab, tab, tab, const(w.shape), const(wabs.shape),
                  const(wwi.shape), const(wckv.shape)],
        out_specs=out_specs,
        out_shape=out_shape,
        compiler_params=pltpu.CompilerParams(dimension_semantics=("arbitrary",), vmem_limit_bytes=VMEM_LIMIT),
        name="proj",
    )(x2, g, *tabs, w, wabs, wwi, wckv)


def _sb_kernel(q_ref, k_ref, v_ref, su_ref, o_ref, acc_ref, carry_ref):
    ib = pl.program_id(1) + 1
    lane = lax.broadcasted_iota(jnp.int32, (LANES, LANES), 1)
    row = lax.broadcasted_iota(jnp.int32, (LANES, LANES), 0)
    low = lane < HEAD_DIM
    high = jnp.logical_not(low)
    su = su_ref[...]
    qpos = ib * LANES + row
    acc_ref[...] = jnp.zeros_like(acc_ref)
    carry_ref[...] = jnp.zeros_like(carry_ref)

    def body(st):
        j, _ = st
        rows = pl.ds(pl.multiple_of(j * LANES, LANES), LANES)
        kpos = j * LANES + lane
        mask = ((kpos < qpos) & ((kpos < N_META) | (kpos >= LANES)))[None]
        zs = []
        for p in range(N_HEADS // 2):
            cols = slice(p * LANES, (p + 1) * LANES)
            q2, kb = q_ref[:, cols], k_ref[rows, cols]
            zero = jnp.zeros_like(q2)
            for keep in (low, high):
                zs.append(_dot_t(jnp.where(keep, q2, zero), kb))
        z = jnp.stack(zs)
        ls_pos = jnp.minimum(z, 0.0) - jnp.log1p(jnp.exp(-jnp.abs(z)))
        log_keep = jnp.where(mask, ls_pos - z, 0.0)
        hi = log_keep.astype(BF16)
        lo = (log_keep - hi.astype(F32)).astype(BF16)
        flat = (N_HEADS * LANES, LANES)
        cs = _dot(hi.reshape(flat), su) + _dot(lo.reshape(flat), su)
        carry = carry_ref[...]
        a = jnp.where(mask, jnp.exp(ls_pos + cs[:, :LANES].reshape(z.shape) + carry), 0.0).astype(BF16)
        carry = carry + cs[:, LANES:].reshape(z.shape)
        carry_ref[...] = carry
        for p in range(N_HEADS // 2):
            vb = v_ref[rows, p * LANES:(p + 1) * LANES]
            zero = jnp.zeros_like(vb)
            acc_ref[p] += _dot(a[2 * p], jnp.where(low, vb, zero)) + _dot(a[2 * p + 1], jnp.where(high, vb, zero))
        return j - 1, jnp.max(carry)

    def cond(st):
        return (st[0] >= 0) & (st[1] > EXP_UNDERFLOW)

    lax.while_loop(cond, body, (ib, jnp.float32(0.0)))
    for p in range(N_HEADS // 2):
        o_ref[:, p * LANES:(p + 1) * LANES] = acc_ref[p].astype(BF16)


def _sb(qkv_x, kvp, su, batch, seq):
    nq = seq // LANES
    tp = seq + LANES
    return pl.pallas_call(
        _sb_kernel,
        grid=(batch, nq),
        in_specs=[pl.BlockSpec((LANES, HEADS_W), lambda b, i: (b * nq + i, 0)),
                  pl.BlockSpec((tp, HEADS_W), lambda b, i: (b, 0)),
                  pl.BlockSpec((tp, HEADS_W), lambda b, i: (b, 1)),
                  pl.BlockSpec(su.shape, lambda b, i: (0, 0))],
        out_specs=pl.BlockSpec((LANES, HEADS_W), lambda b, i: (b * nq + i, 0)),
        out_shape=jax.ShapeDtypeStruct((batch * seq, HEADS_W), BF16),
        scratch_shapes=[pltpu.VMEM((N_HEADS // 2, LANES, LANES), F32), pltpu.VMEM((N_HEADS, LANES, LANES), F32)],
        compiler_params=pltpu.CompilerParams(dimension_semantics=("arbitrary",) * 2, vmem_limit_bytes=VMEM_LIMIT),
        name="sb",
    )(qkv_x, kvp, kvp, su)


def _dsa_kernel(qi_ref, wit_ref, qf_ref, ki_ref, kv_ref, ckvt_ref, lt_ref, wuvt_ref, o_ref,
                sk_ref, acc_ref, m_ref, kvn_ref, qim_ref, qfp_ref, ybt_ref, *, ktop):
    ib = pl.program_id(1) + 1
    nt = (ib + DSA_SUB) // DSA_SUB
    lane = lax.broadcasted_iota(jnp.int32, (LANES, LANES), 1)
    klane = lax.broadcasted_iota(jnp.int32, (DSA_KEYS, LANES), 1)
    krow = lax.broadcasted_iota(jnp.int32, (DSA_KEYS, LANES), 0)
    low = lane < HEAD_DIM
    pairs = range(N_HEADS // 2)
    for p in pairs:
        blk = qi_ref[:, p * LANES:(p + 1) * LANES]
        zero = jnp.zeros_like(blk)
        qim_ref[2 * p * LANES:(2 * p + 1) * LANES, :] = jnp.where(low, blk, zero)
        qim_ref[(2 * p + 1) * LANES:(2 * p + 2) * LANES, :] = jnp.where(low, zero, blk)
        qfp_ref[2 * p * LANES:(2 * p + 1) * LANES, :] = qf_ref[:, 2 * p * QF_W:(2 * p + 1) * QF_W]
        qfp_ref[(2 * p + 1) * LANES:(2 * p + 2) * LANES, :] = qf_ref[:, (2 * p + 1) * QF_W:(2 * p + 2) * QF_W]
    wit = wit_ref[...]
    wit_all = jnp.concatenate([wit[h:h + 1, :] for h in range(N_HEADS)], axis=1)
    cq = 1 + ((ib * LANES + klane - LANES) >> 6)

    def tile(t):
        return pl.ds(pl.multiple_of(t * DSA_KEYS, DSA_KEYS), DSA_KEYS)

    def scores(t, c):
        kb = ki_ref[tile(t), :]
        part = wit_all * jnp.maximum(_dot_t(kb, qim_ref[...]), 0.0)
        s = part[:, :LANES]
        for h in range(1, N_HEADS):
            s = s + part[:, h * LANES:(h + 1) * LANES]
        s = jnp.where(s == 0.0, 0.0, s)
        bits = lax.bitcast_convert_type(s, jnp.int32)
        key = bits ^ ((bits >> 31) & 0x7FFFFFFF)
        rk = t * DSA_KEYS + krow
        ck = jnp.where(rk < LANES, 0, 1 + ((rk - LANES) >> 6))
        adm = ((rk < N_META) | (rk >= LANES)) & (ck <= cq)
        sk_ref[tile(t), :] = jnp.where(adm, key, INT_MIN)
        return c

    lax.fori_loop(0, nt, scores, 0)

    def count(pred):
        def step(t, acc):
            hit = pred(sk_ref[tile(t), :]).astype(jnp.int32)
            for u in range(DSA_SUB):
                acc = acc + hit[u * LANES:(u + 1) * LANES]
            return acc
        acc = lax.fori_loop(0, nt, step, jnp.zeros((LANES, LANES), jnp.int32))
        return jnp.sum(acc, axis=0, keepdims=True)

    zero_row = jnp.zeros((1, LANES), jnp.int32)
    thr = jnp.where(count(lambda k: k >= zero_row) >= ktop, zero_row, jnp.full((1, LANES), INT_MIN, jnp.int32))

    def bit_step(it, thr):
        cand = thr | jnp.left_shift(jnp.int32(1), 30 - it)
        return jnp.where(count(lambda k: k >= cand) >= ktop, cand, thr)

    thr = lax.fori_loop(0, 31, bit_step, thr)
    excess = jnp.where(thr == INT_MIN, 0, count(lambda k: k >= thr) - ktop)

    @pl.when(pl.program_id(1) == 0)
    def _():
        def widest(t, m):
            kvf = kv_ref[tile(t), :].astype(F32)
            return jnp.maximum(m, jnp.max(jnp.sum(kvf * kvf, axis=1, keepdims=True)))
        kvn_ref[...] = jnp.full(kvn_ref.shape, lax.fori_loop(0, kv_ref.shape[0] // DSA_KEYS, widest, jnp.float32(0.0)))

    ones = jnp.ones((8, QF_W), BF16)
    bounds = []
    for p in pairs:
        qp = qfp_ref[2 * p * LANES:(2 * p + 2) * LANES, :].astype(F32)
        qn2 = _dot_t(ones, (qp * qp).astype(BF16))[0:1, :]
        kvn = kvn_ref[0:1, :]
        bounds.append(jnp.sqrt(qn2 * jnp.concatenate([kvn, kvn], axis=1)) * 1.05 + 1e-6)
    bound_max = functools.reduce(jnp.maximum, bounds)
    bound_all = jnp.concatenate(bounds, axis=1)
    fast = (jnp.max(excess) == 0) & (jnp.max(bound_max) <= SAFE_LOGIT_BOUND)
    acc_ref[...] = jnp.zeros_like(acc_ref)

    def twice(a):
        return jnp.concatenate([a, a], axis=1)

    def atile(t):
        return pl.ds(pl.multiple_of(t * ATT_KEYS, ATT_KEYS), ATT_KEYS)

    nta = (ib + ATT_SUB) // ATT_SUB

    @pl.when(fast)
    def _():
        def attend(t, c):
            skt = sk_ref[atile(t), :]
            sel = jnp.where((skt >= thr) & (skt > INT_MIN), 1.0, 0.0).astype(BF16)
            logits = _dot_t(kv_ref[atile(t), :], qfp_ref[...])
            prob = jnp.exp(logits - bound_all).astype(BF16) * jnp.concatenate([sel] * N_HEADS, axis=1)
            acc_ref[...] += _dot(ckvt_ref[t], prob)
            return c

        lax.fori_loop(0, nta, attend, 0)

    @pl.when(jnp.logical_not(fast))
    def _():
        need = (ktop - count(lambda k: k > thr)).astype(F32)
        m_ref[...] = jnp.full_like(m_ref, -1e29)
        lt = lt_ref[...]

        def attend(t, tie_carry):
            skt = sk_ref[atile(t), :]
            tie = (skt == thr) & (skt > INT_MIN)
            rank = _dot(lt, jnp.where(tie, 1.0, 0.0).astype(BF16)) + tie_carry
            sel = twice((skt > thr) | (tie & (rank <= need)))
            kvb = kv_ref[atile(t), :]
            cb = ckvt_ref[t]
            for p in pairs:
                pc = slice(2 * p * LANES, (2 * p + 2) * LANES)
                s = jnp.where(sel, _dot_t(kvb, qfp_ref[pc, :]), -1e30)
                m_old = m_ref[p:p + 1, :]
                m_new = jnp.maximum(m_old, jnp.max(s, axis=0, keepdims=True))
                prob = jnp.exp(s - m_new).astype(BF16)
                acc_ref[:, pc] = acc_ref[:, pc] * jnp.exp(m_old - m_new) + _dot(cb, prob)
                m_ref[p:p + 1, :] = m_new
            return rank[ATT_KEYS - 1:ATT_KEYS, :]

        lax.fori_loop(0, nta, attend, jnp.zeros((1, LANES), F32))

    for h in range(N_HEADS):
        cols = slice(h * LANES, (h + 1) * LANES)
        o = (acc_ref[:KV_LATENT, cols] / acc_ref[KV_LATENT:KV_LATENT + 1, cols]).astype(BF16)
        ybt_ref[h * HEAD_DIM:(h + 1) * HEAD_DIM, :] = _dot(wuvt_ref[h], o)
    o_ref[...] = ybt_ref[...].T.astype(BF16)


def _dsa(qi, wit, qf, kip, kvp, ckvt3, lt, wuvt, batch, seq, ktop):
    nq = seq // LANES
    tpk = kip.shape[0] // batch
    ntk = tpk // ATT_KEYS
    return pl.pallas_call(
        functools.partial(_dsa_kernel, ktop=ktop),
        grid=(batch, nq),
        in_specs=[pl.BlockSpec((LANES, HEADS_W), lambda b, i: (b * nq + i, 0)),
                  pl.BlockSpec((16, LANES), lambda b, i: (0, b * nq + i)),
                  pl.BlockSpec((LANES, N_HEADS * QF_W), lambda b, i: (b * nq + i, 0)),
                  pl.BlockSpec((tpk, LANES), lambda b, i: (b, 0)),
                  pl.BlockSpec((tpk, QF_W), lambda b, i: (b, 0)),
                  pl.BlockSpec((ntk, VAL_ROWS, ATT_KEYS), lambda b, i: (b, 0, 0)),
                  pl.BlockSpec(lt.shape, lambda b, i: (0, 0)),
                  pl.BlockSpec(wuvt.shape, lambda b, i: (0, 0, 0))],
        out_specs=pl.BlockSpec((LANES, HEADS_W), lambda b, i: (b * nq + i, 0)),
        out_shape=jax.ShapeDtypeStruct((batch * seq, HEADS_W), BF16),
        scratch_shapes=[pltpu.VMEM((tpk, LANES), jnp.int32),
                        pltpu.VMEM((VAL_ROWS, N_HEADS * LANES), F32),
                        pltpu.VMEM((8, 2 * LANES), F32),
                        pltpu.VMEM((8, LANES), F32),
                        pltpu.VMEM((N_HEADS * LANES, LANES), BF16),
                        pltpu.VMEM((N_HEADS * LANES, QF_W), BF16),
                        pltpu.VMEM((HEADS_W, LANES), F32)],
        compiler_params=pltpu.CompilerParams(dimension_semantics=("arbitrary",) * 2, vmem_limit_bytes=VMEM_LIMIT),
        name="dsa",
    )(qi, wit, qf, kip, kvp, ckvt3, lt, wuvt)


def _merge_kernel(x_ref, ya_ref, yb_ref, gt_ref, wua_ref, wub_ref, wo_ref, g_ref, wrt_ref, brt_ref, slt_ref,
                  h1_ref, u2_ref, r_ref, cnt_ref, carry_ref):
    @pl.when(pl.program_id(0) == 0)
    def _():
        carry_ref[...] = jnp.zeros_like(carry_ref)

    gt = gt_ref[...].astype(F32)
    z = gt[:, :D_MODEL] * _dot(ya_ref[...], wua_ref[...]) + gt[:, D_MODEL:] * _dot(yb_ref[...], wub_ref[...])
    h1 = x_ref[...] + _dot(z.astype(BF16), wo_ref[...])
    h1_ref[...] = h1
    u2 = _rmsnorm(h1, g_ref[...])
    u2_ref[...] = u2
    u_hi = u2.astype(BF16)
    u_lo = (u2 - u_hi.astype(F32)).astype(BF16)
    hi_terms = _dot(u_hi, wrt_ref[...])
    lg = hi_terms[:, :LANES] + hi_terms[:, LANES:] + _dot(u_lo, wrt_ref[:, :LANES]) + brt_ref[...]

    tm = lg.shape[0]
    lane = lax.broadcasted_iota(jnp.int32, (tm, LANES), 1)
    big = jnp.int32(LANES)

    def softmax(mask):
        x = jnp.where(mask, lg, -jnp.inf)
        e = jnp.where(mask, jnp.exp(x - jnp.max(x, axis=-1, keepdims=True)), 0.0)
        return e / jnp.sum(e, axis=-1, keepdims=True)

    def top1(p, mask):
        best = jnp.max(jnp.where(mask, p, -1.0), axis=-1, keepdims=True)
        idx = jnp.min(jnp.where(mask & (p == best), lane, big), axis=-1, keepdims=True)
        return best, idx

    gmask = (lane >= N_EXPERTS) & (lane < N_EXPERTS + N_GROUPS)
    p_grp, g_lane = top1(softmax(gmask), gmask)
    first = (g_lane - N_EXPERTS) * EXP_PER_GROUP
    emask = (lane >= first) & (lane < first + EXP_PER_GROUP)
    ep = softmax(emask)
    p0, e0 = top1(ep, emask)
    rest = emask & (lane != e0)
    p1, e1 = top1(ep, rest)
    den = p0 + p1
    w0 = p_grp * p0 / den
    w1 = p_grp * p1 / den

    oh0 = lane == e0
    oh1 = lane == e1
    both = (oh0 | oh1).astype(BF16)
    before = _dot(slt_ref[...], both) + carry_ref[...]
    rank0 = jnp.sum(jnp.where(oh0, before, 0.0), axis=-1, keepdims=True)
    rank1 = jnp.sum(jnp.where(oh1, before, 0.0), axis=-1, keepdims=True)
    carry = carry_ref[...] + jnp.sum(both.astype(F32), axis=0, keepdims=True)
    carry_ref[...] = carry
    cnt_ref[...] = carry

    cols = (e0.astype(F32), e1.astype(F32), w0, w1, rank0, rank1)
    r = jnp.zeros((tm, LANES), F32)
    for c, v in enumerate(cols):
        r = jnp.where(lane == c, v, r)
    r_ref[...] = r


def _merge(x2, ya, yb, gt, wua, wub, wo, g, wrt, brt, slt):
    n = x2.shape[0]
    tm = MERGE_TILE
    row = lambda width: pl.BlockSpec((tm, width), lambda i: (i, 0))
    const = lambda a: pl.BlockSpec(a.shape, lambda i: (0,) * a.ndim)
    return pl.pallas_call(
        _merge_kernel,
        grid=(n // tm,),
        in_specs=[row(D_MODEL), row(HEADS_W), row(HEADS_W), row(2 * D_MODEL), const(wua), const(wub), const(wo),
                  const(g), const(wrt), const(brt), const(slt)],
        out_specs=(row(D_MODEL), row(D_MODEL), row(LANES), pl.BlockSpec((1, LANES), lambda i: (0, 0))),
        out_shape=(jax.ShapeDtypeStruct((n, D_MODEL), F32), jax.ShapeDtypeStruct((n, D_MODEL), F32),
                   jax.ShapeDtypeStruct((n, LANES), F32), jax.ShapeDtypeStruct((1, LANES), F32)),
        scratch_shapes=[pltpu.VMEM((1, LANES), F32)],
        compiler_params=pltpu.CompilerParams(dimension_semantics=("arbitrary",), vmem_limit_bytes=VMEM_LIMIT),
        name="merge",
    )(x2, ya, yb, gt, wua, wub, wo, g, wrt, brt, slt)


def _row_copy(src, src_row, dst, dst_row, sem):
    return pltpu.make_async_copy(src.at[pl.ds(src_row, 1)], dst.at[pl.ds(dst_row, 1)], sem)


def _dispatch_kernel(dest_ref, u_ref, xs_in_ref, xs_ref, sem):
    del xs_in_ref

    def issue(t, c):
        for s in range(2):
            _row_copy(u_ref, t, xs_ref, dest_ref[0, 0, 2 * t + s], sem).start()
        return c

    def drain(t, c):
        for s in range(2):
            _row_copy(u_ref, t, xs_ref, dest_ref[0, 0, 2 * t + s], sem).wait()
        return c

    lax.fori_loop(0, DISPATCH_TILE, issue, 0)
    lax.fori_loop(0, DISPATCH_TILE, drain, 0)


def _dispatch(dest3, u2, xs0):
    n = u2.shape[0]
    return pl.pallas_call(
        _dispatch_kernel,
        grid=(n // DISPATCH_TILE,),
        in_specs=[pl.BlockSpec((1, 1, 2 * DISPATCH_TILE), lambda i: (i, 0, 0), memory_space=pltpu.SMEM),
                  pl.BlockSpec((DISPATCH_TILE, D_MODEL), lambda i: (i, 0)), pl.BlockSpec(memory_space=pl.ANY)],
        out_specs=pl.BlockSpec(memory_space=pl.ANY),
        out_shape=jax.ShapeDtypeStruct(xs0.shape, xs0.dtype),
        scratch_shapes=[pltpu.SemaphoreType.DMA],
        input_output_aliases={2: 0},
        compiler_params=pltpu.CompilerParams(dimension_semantics=("arbitrary",), has_side_effects=True),
        name="dispatch",
    )(dest3, u2, xs0)


def _ffn_kernel(te_ref, nu_ref, xs_ref, w1_ref, w3_ref, w2_ref, ys_ref):
    del te_ref

    @pl.when(pl.program_id(0) < nu_ref[0])
    def _():
        x = xs_ref[...].astype(BF16)
        hid = jax.nn.silu(_dot(x, w1_ref[0])) * _dot(x, w3_ref[0])
        ys_ref[...] = _dot(hid.astype(BF16), w2_ref[0])

    @pl.when(pl.program_id(0) >= nu_ref[0])
    def _():
        ys_ref[...] = jnp.zeros_like(ys_ref)


def _ffn(tile_expert, n_used, xs, w1, w3, w2):
    p = xs.shape[0]
    grid_spec = pltpu.PrefetchScalarGridSpec(
        num_scalar_prefetch=2,
        grid=(p // FFN_TILE,),
        in_specs=[pl.BlockSpec((FFN_TILE, D_MODEL), lambda t, te, nu: (t, 0)),
                  pl.BlockSpec((1, D_MODEL, D_EXPERT), lambda t, te, nu: (te[t], 0, 0)),
                  pl.BlockSpec((1, D_MODEL, D_EXPERT), lambda t, te, nu: (te[t], 0, 0)),
                  pl.BlockSpec((1, D_EXPERT, D_MODEL), lambda t, te, nu: (te[t], 0, 0))],
        out_specs=pl.BlockSpec((FFN_TILE, D_MODEL), lambda t, te, nu: (t, 0)),
    )
    return pl.pallas_call(
        _ffn_kernel,
        grid_spec=grid_spec,
        out_shape=jax.ShapeDtypeStruct((p, D_MODEL), F32),
        compiler_params=pltpu.CompilerParams(dimension_semantics=("arbitrary",), vmem_limit_bytes=VMEM_LIMIT),
        name="ffn",
    )(tile_expert, n_used, xs, w1, w3, w2)


def _final_kernel(dest_ref, h1_ref, r_ref, g_ref, ys_ref, o_ref, buf_ref, sem):
    def issue(t, c):
        for s in range(2):
            _row_copy(ys_ref, dest_ref[0, 0, 2 * t + s], buf_ref.at[s], t, sem).start()
        return c

    def drain(t, c):
        for s in range(2):
            _row_copy(ys_ref, dest_ref[0, 0, 2 * t + s], buf_ref.at[s], t, sem).wait()
        return c

    lax.fori_loop(0, FINAL_TILE, issue, 0)
    lax.fori_loop(0, FINAL_TILE, drain, 0)
    r = r_ref[...]
    moe = r[:, 2:3] * buf_ref[0] + r[:, 3:4] * buf_ref[1]
    o_ref[...] = _rmsnorm(h1_ref[...] + moe, g_ref[...])


def _final(dest3, h1, r, g, ys):
    n = h1.shape[0]
    tm = FINAL_TILE
    return pl.pallas_call(
        _final_kernel,
        grid=(n // tm,),
        in_specs=[pl.BlockSpec((1, 1, 2 * tm), lambda i: (i, 0, 0), memory_space=pltpu.SMEM),
                  pl.BlockSpec((tm, D_MODEL), lambda i: (i, 0)),
                  pl.BlockSpec((tm, LANES), lambda i: (i, 0)),
                  pl.BlockSpec((1, D_MODEL), lambda i: (0, 0)),
                  pl.BlockSpec(memory_space=pl.ANY)],
        out_specs=pl.BlockSpec((tm, D_MODEL), lambda i: (i, 0)),
        out_shape=jax.ShapeDtypeStruct((n, D_MODEL), F32),
        scratch_shapes=[pltpu.VMEM((2, tm, D_MODEL), F32), pltpu.SemaphoreType.DMA],
        compiler_params=pltpu.CompilerParams(dimension_semantics=("arbitrary",), vmem_limit_bytes=VMEM_LIMIT),
        name="final",
    )(dest3, h1, r, g, ys)


def _rope_tables(pos):
    half = ROPE_DIM // 2
    inv = np.float32(ROPE_THETA) ** (-np.arange(half, dtype=np.float32) / np.float32(half))
    ang = np.asarray(pos, np.float32)[:, None] * inv[None, :]
    cos, sin = np.cos(ang).astype(np.float32), np.sin(ang).astype(np.float32)
    d = np.arange(LANES) % HEAD_DIM
    f = d % half
    cos_t = np.where(d < ROPE_DIM, cos[:, f], np.float32(1))
    sa = np.where((d >= half) & (d < ROPE_DIM), sin[:, f], np.float32(0))
    sb = np.where(d < half, -sin[:, f], np.float32(0))
    return tuple(jnp.asarray(t, F32) for t in (cos_t, sa, sb))


def _proj_weights(w_in, w_uk):
    qa, ka, va, qb, ckv, krope, qi, ki, wi, ga, gb = jnp.split(w_in, list(np.cumsum(SPLITS)[:-1]), axis=-1)
    zeros = lambda width: jnp.zeros((D_MODEL, width), w_in.dtype)
    kv = jnp.concatenate([krope, zeros(LANES - ROPE_DIM), ckv], axis=1)
    w = jnp.concatenate([qa * ATTN_SCALE, ka, va, qb, qi, ki, ki, kv, ga, gb], axis=1).astype(BF16)
    assert w.shape[1] == PROJ_W
    nope = HEAD_DIM - ROPE_DIM
    per_head = jnp.concatenate([
        jnp.concatenate([jnp.eye(ROPE_DIM, dtype=F32)[None].repeat(N_HEADS, 0),
                         jnp.zeros((N_HEADS, ROPE_DIM, QF_W - ROPE_DIM), F32)], axis=2),
        jnp.concatenate([jnp.zeros((N_HEADS, nope, LANES), F32), jnp.swapaxes(w_uk, 1, 2)], axis=2)], axis=1)
    head_eye = jnp.eye(N_HEADS, dtype=F32)
    wabs = (per_head[:, :, None, :] * (head_eye * ATTN_SCALE)[:, None, :, None])
    wabs = wabs.reshape(HEADS_W, N_HEADS * QF_W).astype(BF16)
    wwi = jnp.concatenate([wi.T, jnp.zeros((16 - N_HEADS, D_MODEL), w_in.dtype)], axis=0).astype(BF16)
    wckv = ckv.T.astype(BF16)
    return w, wabs, wwi, wckv


def _tri(n, strict_lower):
    r = np.arange(n)
    m = (r[None, :] < r[:, None]) if strict_lower else (r[None, :] <= r[:, None])
    return jnp.asarray(m, BF16)


def kernel(x, meta_tokens, norm_mix_g, w_in, w_uk, w_uv, w_up_a, w_up_b, w_o, norm_ffn_g, w_group, b_group,
           w_router, b_router, w1, w3, w2, norm_final_g):
    batch, seq, d = x.shape
    assert d == D_MODEL and seq % LANES == 0 and norm_mix_g.shape[0] == 1
    n = batch * seq
    nq = seq // LANES
    nb = nq + 1
    tp = seq + LANES
    ktop = min(TOPK_MAX, seq // 4)
    x2 = x.reshape(n, d)

    w, wabs, wwi, wckv = _proj_weights(w_in[0], w_uk[0])
    g_mix = norm_mix_g[0][None, :]
    proj_tile = 512 if seq % 512 == 0 else LANES
    px = _proj(x2, g_mix, _rope_tables(N_META + np.arange(seq)), w, wabs, wwi, wckv, proj_tile, seq // proj_tile)
    meta = jnp.concatenate([meta_tokens.astype(x.dtype), jnp.zeros((LANES - N_META, d), x.dtype)], axis=0)
    pm = _proj(meta, g_mix, _rope_tables(np.arange(LANES)), w, wabs, wwi, wckv, LANES, 1)
    qkv_x, qf_x, qi_x, ki_x, kv_x, gt_x, wit_x, ckvt_x = px
    qkv_m, _, _, ki_m, kv_m, _, _, ckvt_m = pm

    def with_meta(xpart, mpart, rows):
        wdt = xpart.shape[1]
        parts = [jnp.broadcast_to(mpart[None], (batch, LANES, wdt)), xpart.reshape(batch, seq, wdt)]
        if rows > tp:
            parts.append(jnp.zeros((batch, rows - tp, wdt), xpart.dtype))
        return jnp.concatenate(parts, axis=1).reshape(batch * rows, wdt)

    kv_a = with_meta(qkv_x[:, HEADS_W:], qkv_m[:, HEADS_W:], tp)
    su = jnp.concatenate([_tri(LANES, True), jnp.ones((LANES, LANES), BF16)], axis=1)
    ya = _sb(qkv_x, kv_a, su, batch, seq)

    tpk = -(-tp // DSA_KEYS) * DSA_KEYS
    kip = with_meta(ki_x, ki_m, tpk)
    kvp = with_meta(kv_x, kv_m, tpk)
    ckvt = jnp.concatenate([jnp.broadcast_to(ckvt_m[:, None, :], (KV_LATENT, batch, LANES)),
                            ckvt_x.reshape(KV_LATENT, batch, seq),
                            jnp.zeros((KV_LATENT, batch, tpk - tp), BF16)], axis=2)
    ones_row = jnp.concatenate([jnp.ones((1, batch, tpk), BF16),
                                jnp.zeros((VAL_ROWS - KV_LATENT - 1, batch, tpk), BF16)], axis=0)
    ckvt = jnp.concatenate([ckvt, ones_row], axis=0)
    ckvt3 = ckvt.reshape(VAL_ROWS, batch * tpk // ATT_KEYS, ATT_KEYS).transpose(1, 0, 2)
    wuvt = jnp.swapaxes(w_uv[0], 1, 2).astype(BF16)
    yb = _dsa(qi_x, wit_x, qf_x, kip, kvp, ckvt3, _tri(ATT_KEYS, False), wuvt, batch, seq, ktop)

    wrt = jnp.concatenate([w_router[0], w_group[0],
                           jnp.zeros((d, LANES - N_EXPERTS - N_GROUPS), F32)], axis=1).astype(F32)
    wrt_hi = wrt.astype(BF16)
    wrt = jnp.concatenate([wrt_hi, (wrt - wrt_hi.astype(F32)).astype(BF16)], axis=1)
    brt = jnp.concatenate([b_router[0], b_group[0], jnp.zeros((LANES - N_EXPERTS - N_GROUPS,), F32)])[None, :]
    h1, u2, r, counts = _merge(x2, ya, yb, gt_x, w_up_a[0].astype(BF16), w_up_b[0].astype(BF16),
                               w_o[0].astype(BF16), norm_ffn_g[0][None, :], wrt, brt.astype(F32),
                               _tri(MERGE_TILE, True))

    cnt = counts[0, :N_EXPERTS].astype(jnp.int32)
    padded = (cnt + FFN_TILE - 1) // FFN_TILE * FFN_TILE
    p_end = jnp.cumsum(padded)
    p_start = p_end - padded
    e_sel = r[:, 0:2].astype(jnp.int32)
    dest = p_start[e_sel] + r[:, 4:6].astype(jnp.int32)
    n_tiles = (2 * n) // FFN_TILE + N_EXPERTS
    tile_row = jnp.arange(n_tiles, dtype=jnp.int32) * FFN_TILE
    tile_expert = jnp.minimum(jnp.sum((p_end[None, :] <= tile_row[:, None]).astype(jnp.int32), axis=1),
                              N_EXPERTS - 1)
    n_used = (p_end[-1] // FFN_TILE).astype(jnp.int32)
    tile_expert = jnp.where(jnp.arange(n_tiles) < n_used, tile_expert, tile_expert[jnp.maximum(n_used - 1, 0)])

    xs0 = jnp.zeros((n_tiles * FFN_TILE, d), F32)
    xs = _dispatch(dest.reshape(n // DISPATCH_TILE, 1, 2 * DISPATCH_TILE), u2, xs0)
    ys = _ffn(tile_expert, n_used[None], xs, w1[0].astype(BF16), w3[0].astype(BF16), w2[0].astype(BF16))
    out = _final(dest.reshape(n // FINAL_TILE, 1, 2 * FINAL_TILE), h1, r, norm_final_g[None, :], ys)
    return out.reshape(batch, seq, d)
```

```python
import functools
import math

import numpy as np
import jax
import jax.numpy as jnp
from jax import lax
from jax.experimental import pallas as pl
from jax.experimental.pallas import tpu as pltpu

D_MODEL = 1024
CHUNK = 64
N_META = 16
HEAD_DIM = 64
ROPE_DIM = 16
ROPE_THETA = 500000.0
N_HEADS = 8
KV_LATENT = 128
IDX_DIM = 64
TOPK_MAX = 256
N_GROUPS = 4
EXP_PER_GROUP = 8
N_EXPERTS = 32
D_EXPERT = 512
NORM_EPS = 1e-6
HEADS_W = N_HEADS * HEAD_DIM
SPLITS = (HEADS_W, HEADS_W, HEADS_W, HEADS_W, KV_LATENT, ROPE_DIM, N_HEADS * IDX_DIM, IDX_DIM, N_HEADS,
          D_MODEL, D_MODEL)

LANES = 128
QF_W = 2 * LANES
INT_MIN = -2 ** 31
EXP_UNDERFLOW = -104.0
ATTN_SCALE = 1.0 / math.sqrt(HEAD_DIM)
VMEM_LIMIT = 56 * 1024 * 1024
DSA_SUB = 4
DSA_KEYS = DSA_SUB * LANES
ATT_SUB = 2
ATT_KEYS = ATT_SUB * LANES
VAL_ROWS = KV_LATENT + 8
SAFE_LOGIT_BOUND = 40.0
FFN_TILE = 256
MERGE_TILE = 512
FINAL_TILE = 256
DISPATCH_TILE = 512
ROW_DMA_UNROLL = 8

BF16 = jnp.bfloat16
F32 = jnp.float32

_C_QKV = (0, 3 * HEADS_W)
_C_QB = (_C_QKV[1], _C_QKV[1] + HEADS_W)
_C_QI = (_C_QB[1], _C_QB[1] + HEADS_W)
_C_KI = (_C_QI[1], _C_QI[1] + LANES)
_C_KV = (_C_KI[1], _C_KI[1] + QF_W)
_C_G = (_C_KV[1], _C_KV[1] + 2 * D_MODEL)
PROJ_W = _C_G[1]


def _dot(a, b):
    return jnp.dot(a, b, preferred_element_type=F32)


def _dot_t(a, b):
    return lax.dot_general(a, b, (((1,), (1,)), ((), ())), preferred_element_type=F32)


def _rmsnorm(x, g):
    return x * lax.rsqrt(jnp.mean(x * x, axis=-1, keepdims=True) + NORM_EPS) * g


def _rope(x, cos, sa, sb):
    w = x.shape[1]
    n = w // LANES
    if n > 1:
        cos, sa, sb = (jnp.concatenate([t] * n, axis=1) for t in (cos, sa, sb))
    return x * cos + pltpu.roll(x, 8, 1) * sa + pltpu.roll(x, w - 8, 1) * sb


def _proj_kernel(x_ref, g_ref, cos_ref, sa_ref, sb_ref, w_ref, wabs_ref, wwi_ref, wckv_ref,
                 qa_ref, kva_ref, qf_ref, qi_ref, ki_ref, kv_ref, gt_ref, wit_ref, ckvt_ref):
    ub = _rmsnorm(x_ref[...], g_ref[...]).astype(BF16)
    cos, sa, sb = cos_ref[...], sa_ref[...], sb_ref[...]
    qa_ref[...] = _dot(ub, w_ref[:, _C_QKV[0]:_C_QKV[0] + HEADS_W]).astype(BF16)
    kva_ref[...] = _dot(ub, w_ref[:, _C_QKV[0] + HEADS_W:_C_QKV[1]]).astype(BF16)
    qb = _rope(_dot(ub, w_ref[:, _C_QB[0]:_C_QB[1]]), cos, sa, sb).astype(BF16)
    qf_ref[...] = _dot(qb, wabs_ref[...]).astype(BF16)
    qi_ref[...] = _rope(_dot(ub, w_ref[:, _C_QI[0]:_C_QI[1]]), cos, sa, sb).astype(BF16)
    ki_ref[...] = _rope(_dot(ub, w_ref[:, _C_KI[0]:_C_KI[1]]), cos, sa, sb).astype(BF16)
    kv = _dot(ub, w_ref[:, _C_KV[0]:_C_KV[1]])
    kv_ref[...] = jnp.concatenate([_rope(kv[:, :LANES], cos, sa, sb), kv[:, LANES:]], axis=1).astype(BF16)
    gt_ref[...] = jax.nn.sigmoid(_dot(ub, w_ref[:, _C_G[0]:_C_G[1]])).astype(BF16)
    wit_ref[...] = _dot_t(wwi_ref[...], ub)
    ckvt_ref[...] = _dot_t(wckv_ref[...], ub).astype(BF16)


def _proj(x2, g, tabs, w, wabs, wwi, wckv, tile, tab_tiles):
    n = x2.shape[0]
    row = lambda width: pl.BlockSpec((tile, width), lambda i: (i, 0))
    const = lambda shape: pl.BlockSpec(shape, lambda i: (0, 0))
    tab = pl.BlockSpec((tile, LANES), lambda i: (i % tab_tiles, 0))
    out_shape = (
        jax.ShapeDtypeStruct((n, HEADS_W), BF16),
        jax.ShapeDtypeStruct((n, 2 * HEADS_W), BF16),
        jax.ShapeDtypeStruct((n, N_HEADS * QF_W), BF16),
        jax.ShapeDtypeStruct((n, HEADS_W), BF16),
        jax.ShapeDtypeStruct((n, LANES), BF16),
        jax.ShapeDtypeStruct((n, QF_W), BF16),
        jax.ShapeDtypeStruct((n, 2 * D_MODEL), BF16),
        jax.ShapeDtypeStruct((16, n), F32),
        jax.ShapeDtypeStruct((KV_LATENT, n), BF16),
    )
    out_specs = (row(HEADS_W), row(2 * HEADS_W), row(N_HEADS * QF_W), row(HEADS_W), row(LANES), row(QF_W),
                 row(2 * D_MODEL),
                 pl.BlockSpec((16, tile), lambda i: (0, i)), pl.BlockSpec((KV_LATENT, tile), lambda i: (0, i)))
    return pl.pallas_call(
        _proj_kernel,
        grid=(n // tile,),
        in_specs=[row(D_MODEL), const((1, D_MODEL)), tab, tab, tab, const(w.shape), const(wabs.shape),
                  const(wwi.shape), const(wckv.shape)],
        out_specs=out_specs,
        out_shape=out_shape,
        compiler_params=pltpu.CompilerParams(dimension_semantics=("arbitrary",), vmem_limit_bytes=VMEM_LIMIT),
        name="proj",
    )(x2, g, *tabs, w, wabs, wwi, wckv)


def _sb_kernel(q_ref, k_ref, v_ref, su_ref, o_ref, acc_ref, carry_ref):
    ib = pl.program_id(1) + 1
    lane = lax.broadcasted_iota(jnp.int32, (LANES, LANES), 1)
    row = lax.broadcasted_iota(jnp.int32, (LANES, LANES), 0)
    low = lane < HEAD_DIM
    high = jnp.logical_not(low)
    su = su_ref[...]
    qpos = ib * LANES + row
    acc_ref[...] = jnp.zeros_like(acc_ref)
    carry_ref[...] = jnp.zeros_like(carry_ref)

    def body(st):
        j, _ = st
        rows = pl.ds(pl.multiple_of(j * LANES, LANES), LANES)
        kpos = j * LANES + lane
        mask = ((kpos < qpos) & ((kpos < N_META) | (kpos >= LANES)))[None]
        zs = []
        for p in range(N_HEADS // 2):
            cols = slice(p * LANES, (p + 1) * LANES)
            q2, kb = q_ref[:, cols], k_ref[rows, cols]
            zero = jnp.zeros_like(q2)
            for keep in (low, high):
                zs.append(_dot_t(jnp.where(keep, q2, zero), kb))
        z = jnp.stack(zs)
        ls_pos = jnp.minimum(z, 0.0) - jnp.log(1.0 + jnp.exp(-jnp.abs(z)))
        log_keep = jnp.where(mask, ls_pos - z, 0.0)
        hi = log_keep.astype(BF16)
        lo = (log_keep - hi.astype(F32)).astype(BF16)
        flat = (N_HEADS * LANES, LANES)
        cs = _dot(hi.reshape(flat), su) + _dot(lo.reshape(flat), su)
        carry = carry_ref[...]
        a = jnp.where(mask, jnp.exp(ls_pos + cs[:, :LANES].reshape(z.shape) + carry), 0.0).astype(BF16)
        carry = carry + cs[:, LANES:].reshape(z.shape)
        carry_ref[...] = carry
        for p in range(N_HEADS // 2):
            vb = v_ref[rows, p * LANES:(p + 1) * LANES]
            zero = jnp.zeros_like(vb)
            acc_ref[p] += _dot(a[2 * p], jnp.where(low, vb, zero)) + _dot(a[2 * p + 1], jnp.where(high, vb, zero))
        return j - 1, jnp.max(carry)

    def cond(st):
        return (st[0] >= 0) & (st[1] > EXP_UNDERFLOW)

    lax.while_loop(cond, body, (ib, jnp.float32(0.0)))
    for p in range(N_HEADS // 2):
        o_ref[:, p * LANES:(p + 1) * LANES] = acc_ref[p].astype(BF16)


def _sb(qkv_x, kvp, su, batch, seq):
    nq = seq // LANES
    tp = seq + LANES
    return pl.pallas_call(
        _sb_kernel,
        grid=(batch, nq),
        in_specs=[pl.BlockSpec((LANES, HEADS_W), lambda b, i: (b * nq + i, 0)),
                  pl.BlockSpec((tp, HEADS_W), lambda b, i: (b, 0)),
                  pl.BlockSpec((tp, HEADS_W), lambda b, i: (b, 1)),
                  pl.BlockSpec(su.shape, lambda b, i: (0, 0))],
        out_specs=pl.BlockSpec((LANES, HEADS_W), lambda b, i: (b * nq + i, 0)),
        out_shape=jax.ShapeDtypeStruct((batch * seq, HEADS_W), BF16),
        scratch_shapes=[pltpu.VMEM((N_HEADS // 2, LANES, LANES), F32), pltpu.VMEM((N_HEADS, LANES, LANES), F32)],
        compiler_params=pltpu.CompilerParams(dimension_semantics=("arbitrary",) * 2, vmem_limit_bytes=VMEM_LIMIT),
        name="sb",
    )(qkv_x, kvp, kvp, su)


def _dsa_kernel(qi_ref, wit_ref, qf_ref, ki_ref, kv_ref, ckvt_ref, lt_ref, wuvt_ref, o_ref,
                sk_ref, acc_ref, m_ref, kvn_ref, qim_ref, qfp_ref, ybt_ref, *, ktop):
    ib = pl.program_id(1) + 1
    nt = (ib + DSA_SUB) // DSA_SUB
    lane = lax.broadcasted_iota(jnp.int32, (LANES, LANES), 1)
    klane = lax.broadcasted_iota(jnp.int32, (DSA_KEYS, LANES), 1)
    krow = lax.broadcasted_iota(jnp.int32, (DSA_KEYS, LANES), 0)
    low = lane < HEAD_DIM
    pairs = range(N_HEADS // 2)
    for p in pairs:
        blk = qi_ref[:, p * LANES:(p + 1) * LANES]
        zero = jnp.zeros_like(blk)
        qim_ref[2 * p * LANES:(2 * p + 1) * LANES, :] = jnp.where(low, blk, zero)
        qim_ref[(2 * p + 1) * LANES:(2 * p + 2) * LANES, :] = jnp.where(low, zero, blk)
        qfp_ref[2 * p * LANES:(2 * p + 1) * LANES, :] = qf_ref[:, 2 * p * QF_W:(2 * p + 1) * QF_W]
        qfp_ref[(2 * p + 1) * LANES:(2 * p + 2) * LANES, :] = qf_ref[:, (2 * p + 1) * QF_W:(2 * p + 2) * QF_W]
    wit = wit_ref[...]
    wit_all = jnp.concatenate([wit[h:h + 1, :] for h in range(N_HEADS)], axis=1)
    cq = 1 + ((ib * LANES + klane - LANES) >> 6)

    def tile(t):
        return pl.ds(pl.multiple_of(t * DSA_KEYS, DSA_KEYS), DSA_KEYS)

    def scores(t, c):
        kb = ki_ref[tile(t), :]
        part = wit_all * jnp.maximum(_dot_t(kb, qim_ref[...]), 0.0)
        s = part[:, :LANES]
        for h in range(1, N_HEADS):
            s = s + part[:, h * LANES:(h + 1) * LANES]
        s = jnp.where(s == 0.0, 0.0, s)
        bits = lax.bitcast_convert_type(s, jnp.int32)
        key = bits ^ ((bits >> 31) & 0x7FFFFFFF)
        rk = t * DSA_KEYS + krow
        ck = jnp.where(rk < LANES, 0, 1 + ((rk - LANES) >> 6))
        adm = ((rk < N_META) | (rk >= LANES)) & (ck <= cq)
        sk_ref[tile(t), :] = jnp.where(adm, key, INT_MIN)
        return c

    lax.fori_loop(0, nt, scores, 0)

    def count(pred):
        def step(t, acc):
            hit = pred(sk_ref[tile(t), :]).astype(jnp.int32)
            for u in range(DSA_SUB):
                acc = acc + hit[u * LANES:(u + 1) * LANES]
            return acc
        acc = lax.fori_loop(0, nt, step, jnp.zeros((LANES, LANES), jnp.int32))
        return jnp.sum(acc, axis=0, keepdims=True)

    zero_row = jnp.zeros((1, LANES), jnp.int32)
    thr = jnp.where(count(lambda k: k >= zero_row) >= ktop, zero_row, jnp.full((1, LANES), INT_MIN, jnp.int32))

    def bit_step(it, thr):
        cand = thr | jnp.left_shift(jnp.int32(1), 30 - it)
        return jnp.where(count(lambda k: k >= cand) >= ktop, cand, thr)

    thr = lax.fori_loop(0, 31, bit_step, thr)
    excess = jnp.where(thr == INT_MIN, 0, count(lambda k: k >= thr) - ktop)

    @pl.when(pl.program_id(1) == 0)
    def _():
        def widest(t, m):
            kvf = kv_ref[tile(t), :].astype(F32)
            return jnp.maximum(m, jnp.max(jnp.sum(kvf * kvf, axis=1, keepdims=True)))
        kvn_ref[...] = jnp.full(kvn_ref.shape, lax.fori_loop(0, kv_ref.shape[0] // DSA_KEYS, widest, jnp.float32(0.0)))

    ones = jnp.ones((8, QF_W), BF16)
    bounds = []
    for p in pairs:
        qp = qfp_ref[2 * p * LANES:(2 * p + 2) * LANES, :].astype(F32)
        qn2 = _dot_t(ones, (qp * qp).astype(BF16))[0:1, :]
        kvn = kvn_ref[0:1, :]
        bounds.append(jnp.sqrt(qn2 * jnp.concatenate([kvn, kvn], axis=1)) * 1.05 + 1e-6)
    bound_max = functools.reduce(jnp.maximum, bounds)
    bound_all = jnp.concatenate(bounds, axis=1)
    fast = (jnp.max(excess) == 0) & (jnp.max(bound_max) <= SAFE_LOGIT_BOUND)
    acc_ref[...] = jnp.zeros_like(acc_ref)

    def twice(a):
        return jnp.concatenate([a, a], axis=1)

    def atile(t):
        return pl.ds(pl.multiple_of(t * ATT_KEYS, ATT_KEYS), ATT_KEYS)

    nta = (ib + ATT_SUB) // ATT_SUB

    @pl.when(fast)
    def _():
        def attend(t, c):
            skt = sk_ref[atile(t), :]
            sel = jnp.where((skt >= thr) & (skt > INT_MIN), 1.0, 0.0).astype(BF16)
            logits = _dot_t(kv_ref[atile(t), :], qfp_ref[...])
            prob = jnp.exp(logits - bound_all).astype(BF16) * jnp.concatenate([sel] * N_HEADS, axis=1)
            acc_ref[...] += _dot(ckvt_ref[t], prob)
            return c

        lax.fori_loop(0, nta, attend, 0)

    @pl.when(jnp.logical_not(fast))
    def _():
        need = (ktop - count(lambda k: k > thr)).astype(F32)
        m_ref[...] = jnp.full_like(m_ref, -1e29)
        lt = lt_ref[...]

        def attend(t, tie_carry):
            skt = sk_ref[atile(t), :]
            tie = (skt == thr) & (skt > INT_MIN)
            rank = _dot(lt, jnp.where(tie, 1.0, 0.0).astype(BF16)) + tie_carry
            sel = twice((skt > thr) | (tie & (rank <= need)))
            kvb = kv_ref[atile(t), :]
            cb = ckvt_ref[t]
            for p in pairs:
                pc = slice(2 * p * LANES, (2 * p + 2) * LANES)
                s = jnp.where(sel, _dot_t(kvb, qfp_ref[pc, :]), -1e30)
                m_old = m_ref[p:p + 1, :]
                m_new = jnp.maximum(m_old, jnp.max(s, axis=0, keepdims=True))
                prob = jnp.exp(s - m_new).astype(BF16)
                acc_ref[:, pc] = acc_ref[:, pc] * jnp.exp(m_old - m_new) + _dot(cb, prob)
                m_ref[p:p + 1, :] = m_new
            return rank[ATT_KEYS - 1:ATT_KEYS, :]

        lax.fori_loop(0, nta, attend, jnp.zeros((1, LANES), F32))

    for h in range(N_HEADS):
        cols = slice(h * LANES, (h + 1) * LANES)
        o = (acc_ref[:KV_LATENT, cols] / acc_ref[KV_LATENT:KV_LATENT + 1, cols]).astype(BF16)
        ybt_ref[h * HEAD_DIM:(h + 1) * HEAD_DIM, :] = _dot(wuvt_ref[h], o)
    o_ref[...] = ybt_ref[...].T.astype(BF16)


def _dsa(qi, wit, qf, kip, kvp, ckvt3, lt, wuvt, batch, seq, ktop):
    nq = seq // LANES
    tpk = kip.shape[0] // batch
    ntk = tpk // ATT_KEYS
    return pl.pallas_call(
        functools.partial(_dsa_kernel, ktop=ktop),
        grid=(batch, nq),
        in_specs=[pl.BlockSpec((LANES, HEADS_W), lambda b, i: (b * nq + i, 0)),
                  pl.BlockSpec((16, LANES), lambda b, i: (0, b * nq + i)),
                  pl.BlockSpec((LANES, N_HEADS * QF_W), lambda b, i: (b * nq + i, 0)),
                  pl.BlockSpec((tpk, LANES), lambda b, i: (b, 0)),
                  pl.BlockSpec((tpk, QF_W), lambda b, i: (b, 0)),
                  pl.BlockSpec((ntk, VAL_ROWS, ATT_KEYS), lambda b, i: (b, 0, 0)),
                  pl.BlockSpec(lt.shape, lambda b, i: (0, 0)),
                  pl.BlockSpec(wuvt.shape, lambda b, i: (0, 0, 0))],
        out_specs=pl.BlockSpec((LANES, HEADS_W), lambda b, i: (b * nq + i, 0)),
        out_shape=jax.ShapeDtypeStruct((batch * seq, HEADS_W), BF16),
        scratch_shapes=[pltpu.VMEM((tpk, LANES), jnp.int32),
                        pltpu.VMEM((VAL_ROWS, N_HEADS * LANES), F32),
                        pltpu.VMEM((8, 2 * LANES), F32),
                        pltpu.VMEM((8, LANES), F32),
                        pltpu.VMEM((N_HEADS * LANES, LANES), BF16),
                        pltpu.VMEM((N_HEADS * LANES, QF_W), BF16),
                        pltpu.VMEM((HEADS_W, LANES), F32)],
        compiler_params=pltpu.CompilerParams(dimension_semantics=("arbitrary",) * 2, vmem_limit_bytes=VMEM_LIMIT),
        name="dsa",
    )(qi, wit, qf, kip, kvp, ckvt3, lt, wuvt)


def _merge_kernel(x_ref, ya_ref, yb_ref, gt_ref, wua_ref, wub_ref, wo_ref, g_ref, wrt_ref, brt_ref, slt_ref,
                  h1_ref, u2_ref, r_ref, cnt_ref, carry_ref):
    @pl.when(pl.program_id(0) == 0)
    def _():
        carry_ref[...] = jnp.zeros_like(carry_ref)

    gt = gt_ref[...].astype(F32)
    z = gt[:, :D_MODEL] * _dot(ya_ref[...], wua_ref[...]) + gt[:, D_MODEL:] * _dot(yb_ref[...], wub_ref[...])
    h1 = x_ref[...] + _dot(z.astype(BF16), wo_ref[...])
    h1_ref[...] = h1
    u2 = _rmsnorm(h1, g_ref[...])
    u2_ref[...] = u2
    u_hi = u2.astype(BF16)
    u_lo = (u2 - u_hi.astype(F32)).astype(BF16)
    hi_terms = _dot(u_hi, wrt_ref[...])
    lg = hi_terms[:, :LANES] + hi_terms[:, LANES:] + _dot(u_lo, wrt_ref[:, :LANES]) + brt_ref[...]

    tm = lg.shape[0]
    lane = lax.broadcasted_iota(jnp.int32, (tm, LANES), 1)
    big = jnp.int32(LANES)

    def softmax(mask):
        x = jnp.where(mask, lg, -jnp.inf)
        e = jnp.where(mask, jnp.exp(x - jnp.max(x, axis=-1, keepdims=True)), 0.0)
        return e / jnp.sum(e, axis=-1, keepdims=True)

    def top1(p, mask):
        best = jnp.max(jnp.where(mask, p, -1.0), axis=-1, keepdims=True)
        idx = jnp.min(jnp.where(mask & (p == best), lane, big), axis=-1, keepdims=True)
        return best, idx

    gmask = (lane >= N_EXPERTS) & (lane < N_EXPERTS + N_GROUPS)
    p_grp, g_lane = top1(softmax(gmask), gmask)
    first = (g_lane - N_EXPERTS) * EXP_PER_GROUP
    emask = (lane >= first) & (lane < first + EXP_PER_GROUP)
    ep = softmax(emask)
    p0, e0 = top1(ep, emask)
    rest = emask & (lane != e0)
    p1, e1 = top1(ep, rest)
    den = p0 + p1
    w0 = p_grp * p0 / den
    w1 = p_grp * p1 / den

    oh0 = lane == e0
    oh1 = lane == e1
    both = (oh0 | oh1).astype(BF16)
    before = _dot(slt_ref[...], both) + carry_ref[...]
    rank0 = jnp.sum(jnp.where(oh0, before, 0.0), axis=-1, keepdims=True)
    rank1 = jnp.sum(jnp.where(oh1, before, 0.0), axis=-1, keepdims=True)
    carry = carry_ref[...] + jnp.sum(both.astype(F32), axis=0, keepdims=True)
    carry_ref[...] = carry
    cnt_ref[...] = carry

    cols = (e0.astype(F32), e1.astype(F32), w0, w1, rank0, rank1)
    r = jnp.zeros((tm, LANES), F32)
    for c, v in enumerate(cols):
        r = jnp.where(lane == c, v, r)
    r_ref[...] = r


def _merge(x2, ya, yb, gt, wua, wub, wo, g, wrt, brt, slt):
    n = x2.shape[0]
    tm = MERGE_TILE
    row = lambda width: pl.BlockSpec((tm, width), lambda i: (i, 0))
    const = lambda a: pl.BlockSpec(a.shape, lambda i: (0,) * a.ndim)
    return pl.pallas_call(
        _merge_kernel,
        grid=(n // tm,),
        in_specs=[row(D_MODEL), row(HEADS_W), row(HEADS_W), row(2 * D_MODEL), const(wua), const(wub), const(wo),
                  const(g), const(wrt), const(brt), const(slt)],
        out_specs=(row(D_MODEL), row(D_MODEL), row(LANES), pl.BlockSpec((1, LANES), lambda i: (0, 0))),
        out_shape=(jax.ShapeDtypeStruct((n, D_MODEL), F32), jax.ShapeDtypeStruct((n, D_MODEL), F32),
                   jax.ShapeDtypeStruct((n, LANES), F32), jax.ShapeDtypeStruct((1, LANES), F32)),
        scratch_shapes=[pltpu.VMEM((1, LANES), F32)],
        compiler_params=pltpu.CompilerParams(dimension_semantics=("arbitrary",), vmem_limit_bytes=VMEM_LIMIT),
        name="merge",
    )(x2, ya, yb, gt, wua, wub, wo, g, wrt, brt, slt)


def _row_copy(src, src_row, dst, dst_row, sem):
    return pltpu.make_async_copy(src.at[pl.ds(src_row, 1)], dst.at[pl.ds(dst_row, 1)], sem)


def _dispatch_kernel(dest_ref, u_ref, xs_in_ref, xs_ref, sem):
    del xs_in_ref

    def issue(t, c):
        for s in range(2):
            _row_copy(u_ref, t, xs_ref, dest_ref[0, 0, 2 * t + s], sem).start(priority=s)
        return c

    def drain(t, c):
        for s in range(2):
            _row_copy(u_ref, t, xs_ref, dest_ref[0, 0, 2 * t + s], sem).wait()
        return c

    lax.fori_loop(0, DISPATCH_TILE, issue, 0, unroll=ROW_DMA_UNROLL)
    lax.fori_loop(0, DISPATCH_TILE, drain, 0, unroll=ROW_DMA_UNROLL)


def _dispatch(dest3, u2, xs0):
    n = u2.shape[0]
    return pl.pallas_call(
        _dispatch_kernel,
        grid=(n // DISPATCH_TILE,),
        in_specs=[pl.BlockSpec((1, 1, 2 * DISPATCH_TILE), lambda i: (i, 0, 0), memory_space=pltpu.SMEM),
                  pl.BlockSpec((DISPATCH_TILE, D_MODEL), lambda i: (i, 0)), pl.BlockSpec(memory_space=pl.ANY)],
        out_specs=pl.BlockSpec(memory_space=pl.ANY),
        out_shape=jax.ShapeDtypeStruct(xs0.shape, xs0.dtype),
        scratch_shapes=[pltpu.SemaphoreType.DMA],
        input_output_aliases={2: 0},
        compiler_params=pltpu.CompilerParams(dimension_semantics=("arbitrary",), has_side_effects=True),
        name="dispatch",
    )(dest3, u2, xs0)


def _ffn_kernel(te_ref, nu_ref, xs_ref, w1_ref, w3_ref, w2_ref, ys_ref, w1b_ref, w3b_ref, w2b_ref):
    t = pl.program_id(0)

    @pl.when((t == 0) | (te_ref[t] != te_ref[jnp.maximum(t - 1, 0)]))
    def _():
        w1b_ref[...] = w1_ref[0].astype(BF16)
        w3b_ref[...] = w3_ref[0].astype(BF16)
        w2b_ref[...] = w2_ref[0].astype(BF16)

    @pl.when(t < nu_ref[0])
    def _():
        x = xs_ref[...].astype(BF16)
        hid = jax.nn.silu(_dot(x, w1b_ref[...])) * _dot(x, w3b_ref[...])
        ys_ref[...] = _dot(hid.astype(BF16), w2b_ref[...])

    @pl.when(t >= nu_ref[0])
    def _():
        ys_ref[...] = jnp.zeros_like(ys_ref)


def _ffn(tile_expert, n_used, xs, w1, w3, w2):
    p = xs.shape[0]
    grid_spec = pltpu.PrefetchScalarGridSpec(
        num_scalar_prefetch=2,
        grid=(p // FFN_TILE,),
        in_specs=[pl.BlockSpec((FFN_TILE, D_MODEL), lambda t, te, nu: (t, 0)),
                  pl.BlockSpec((1, D_MODEL, D_EXPERT), lambda t, te, nu: (te[t], 0, 0)),
                  pl.BlockSpec((1, D_MODEL, D_EXPERT), lambda t, te, nu: (te[t], 0, 0)),
                  pl.BlockSpec((1, D_EXPERT, D_MODEL), lambda t, te, nu: (te[t], 0, 0))],
        out_specs=pl.BlockSpec((FFN_TILE, D_MODEL), lambda t, te, nu: (t, 0)),
        scratch_shapes=[pltpu.VMEM((D_MODEL, D_EXPERT), BF16), pltpu.VMEM((D_MODEL, D_EXPERT), BF16),
                        pltpu.VMEM((D_EXPERT, D_MODEL), BF16)],
    )
    return pl.pallas_call(
        _ffn_kernel,
        grid_spec=grid_spec,
        out_shape=jax.ShapeDtypeStruct((p, D_MODEL), F32),
        compiler_params=pltpu.CompilerParams(dimension_semantics=("arbitrary",), vmem_limit_bytes=VMEM_LIMIT),
        name="ffn",
    )(tile_expert, n_used, xs, w1, w3, w2)


def _final_kernel(dest_ref, h1_ref, r_ref, g_ref, ys_ref, o_ref, buf_ref, sem):
    def issue(t, c):
        for s in range(2):
            _row_copy(ys_ref, dest_ref[0, 0, 2 * t + s], buf_ref.at[s], t, sem).start(priority=s)
        return c

    def drain(t, c):
        for s in range(2):
            _row_copy(ys_ref, dest_ref[0, 0, 2 * t + s], buf_ref.at[s], t, sem).wait()
        return c

    lax.fori_loop(0, FINAL_TILE, issue, 0, unroll=ROW_DMA_UNROLL)
    lax.fori_loop(0, FINAL_TILE, drain, 0, unroll=ROW_DMA_UNROLL)
    r = r_ref[...]
    moe = r[:, 2:3] * buf_ref[0] + r[:, 3:4] * buf_ref[1]
    o_ref[...] = _rmsnorm(h1_ref[...] + moe, g_ref[...])


def _final(dest3, h1, r, g, ys):
    n = h1.shape[0]
    tm = FINAL_TILE
    return pl.pallas_call(
        _final_kernel,
        grid=(n // tm,),
        in_specs=[pl.BlockSpec((1, 1, 2 * tm), lambda i: (i, 0, 0), memory_space=pltpu.SMEM),
                  pl.BlockSpec((tm, D_MODEL), lambda i: (i, 0)),
                  pl.BlockSpec((tm, LANES), lambda i: (i, 0)),
                  pl.BlockSpec((1, D_MODEL), lambda i: (0, 0)),
                  pl.BlockSpec(memory_space=pl.ANY)],
        out_specs=pl.BlockSpec((tm, D_MODEL), lambda i: (i, 0)),
        out_shape=jax.ShapeDtypeStruct((n, D_MODEL), F32),
        scratch_shapes=[pltpu.VMEM((2, tm, D_MODEL), F32), pltpu.SemaphoreType.DMA],
        compiler_params=pltpu.CompilerParams(dimension_semantics=("arbitrary",), vmem_limit_bytes=VMEM_LIMIT),
        name="final",
    )(dest3, h1, r, g, ys)


def _rope_tables(pos):
    half = ROPE_DIM // 2
    inv = np.float32(ROPE_THETA) ** (-np.arange(half, dtype=np.float32) / np.float32(half))
    ang = np.asarray(pos, np.float32)[:, None] * inv[None, :]
    cos, sin = np.cos(ang).astype(np.float32), np.sin(ang).astype(np.float32)
    d = np.arange(LANES) % HEAD_DIM
    f = d % half
    cos_t = np.where(d < ROPE_DIM, cos[:, f], np.float32(1))
    sa = np.where((d >= half) & (d < ROPE_DIM), sin[:, f], np.float32(0))
    sb = np.where(d < half, -sin[:, f], np.float32(0))
    return tuple(jnp.asarray(t, F32) for t in (cos_t, sa, sb))


def _proj_weights(w_in, w_uk):
    qa, ka, va, qb, ckv, krope, qi, ki, wi, ga, gb = jnp.split(w_in, list(np.cumsum(SPLITS)[:-1]), axis=-1)
    zeros = lambda width: jnp.zeros((D_MODEL, width), w_in.dtype)
    kv = jnp.concatenate([krope, zeros(LANES - ROPE_DIM), ckv], axis=1)
    w = jnp.concatenate([qa * ATTN_SCALE, ka, va, qb, qi, ki, ki, kv, ga, gb], axis=1).astype(BF16)
    assert w.shape[1] == PROJ_W
    nope = HEAD_DIM - ROPE_DIM
    per_head = jnp.concatenate([
        jnp.concatenate([jnp.eye(ROPE_DIM, dtype=F32)[None].repeat(N_HEADS, 0),
                         jnp.zeros((N_HEADS, ROPE_DIM, QF_W - ROPE_DIM), F32)], axis=2),
        jnp.concatenate([jnp.zeros((N_HEADS, nope, LANES), F32), jnp.swapaxes(w_uk, 1, 2)], axis=2)], axis=1)
    head_eye = jnp.eye(N_HEADS, dtype=F32)
    wabs = (per_head[:, :, None, :] * (head_eye * ATTN_SCALE)[:, None, :, None])
    wabs = wabs.reshape(HEADS_W, N_HEADS * QF_W).astype(BF16)
    wwi = jnp.concatenate([wi.T, jnp.zeros((16 - N_HEADS, D_MODEL), w_in.dtype)], axis=0).astype(BF16)
    wckv = ckv.T.astype(BF16)
    return w, wabs, wwi, wckv


def _tri(n, strict_lower):
    r = np.arange(n)
    m = (r[None, :] < r[:, None]) if strict_lower else (r[None, :] <= r[:, None])
    return jnp.asarray(m, BF16)


def kernel(x, meta_tokens, norm_mix_g, w_in, w_uk, w_uv, w_up_a, w_up_b, w_o, norm_ffn_g, w_group, b_group,
           w_router, b_router, w1, w3, w2, norm_final_g):
    batch, seq, d = x.shape
    assert d == D_MODEL and seq % LANES == 0 and norm_mix_g.shape[0] == 1
    n = batch * seq
    nq = seq // LANES
    nb = nq + 1
    tp = seq + LANES
    ktop = min(TOPK_MAX, seq // 4)
    x2 = x.reshape(n, d)

    w, wabs, wwi, wckv = _proj_weights(w_in[0], w_uk[0])
    g_mix = norm_mix_g[0][None, :]
    proj_tile = 512 if seq % 512 == 0 else LANES
    px = _proj(x2, g_mix, _rope_tables(N_META + np.arange(seq)), w, wabs, wwi, wckv, proj_tile, seq // proj_tile)
    meta = jnp.concatenate([meta_tokens.astype(x.dtype), jnp.zeros((LANES - N_META, d), x.dtype)], axis=0)
    pm = _proj(meta, g_mix, _rope_tables(np.arange(LANES)), w, wabs, wwi, wckv, LANES, 1)
    qa_x, kva_x, qf_x, qi_x, ki_x, kv_x, gt_x, wit_x, ckvt_x = px
    _, kva_m, _, _, ki_m, kv_m, _, _, ckvt_m = pm

    def with_meta(xpart, mpart, rows):
        wdt = xpart.shape[1]
        parts = [jnp.broadcast_to(mpart[None], (batch, LANES, wdt)), xpart.reshape(batch, seq, wdt)]
        if rows > tp:
            parts.append(jnp.zeros((batch, rows - tp, wdt), xpart.dtype))
        return jnp.concatenate(parts, axis=1).reshape(batch * rows, wdt)

    kv_a = with_meta(kva_x, kva_m, tp)
    su = jnp.concatenate([_tri(LANES, True), jnp.ones((LANES, LANES), BF16)], axis=1)
    ya = _sb(qa_x, kv_a, su, batch, seq)

    tpk = -(-tp // DSA_KEYS) * DSA_KEYS
    kip = with_meta(ki_x, ki_m, tpk)
    kvp = with_meta(kv_x, kv_m, tpk)
    ckvt = jnp.concatenate([jnp.broadcast_to(ckvt_m[:, None, :], (KV_LATENT, batch, LANES)),
                            ckvt_x.reshape(KV_LATENT, batch, seq),
                            jnp.zeros((KV_LATENT, batch, tpk - tp), BF16)], axis=2)
    ones_row = jnp.concatenate([jnp.ones((1, batch, tpk), BF16),
                                jnp.zeros((VAL_ROWS - KV_LATENT - 1, batch, tpk), BF16)], axis=0)
    ckvt = jnp.concatenate([ckvt, ones_row], axis=0)
    ckvt3 = ckvt.reshape(VAL_ROWS, batch * tpk // ATT_KEYS, ATT_KEYS).transpose(1, 0, 2)
    wuvt = jnp.swapaxes(w_uv[0], 1, 2).astype(BF16)
    yb = _dsa(qi_x, wit_x, qf_x, kip, kvp, ckvt3, _tri(ATT_KEYS, False), wuvt, batch, seq, ktop)

    wrt = jnp.concatenate([w_router[0], w_group[0],
                           jnp.zeros((d, LANES - N_EXPERTS - N_GROUPS), F32)], axis=1).astype(F32)
    wrt_hi = wrt.astype(BF16)
    wrt = jnp.concatenate([wrt_hi, (wrt - wrt_hi.astype(F32)).astype(BF16)], axis=1)
    brt = jnp.concatenate([b_router[0], b_group[0], jnp.zeros((LANES - N_EXPERTS - N_GROUPS,), F32)])[None, :]
    h1, u2, r, counts = _merge(x2, ya, yb, gt_x, w_up_a[0].astype(BF16), w_up_b[0].astype(BF16),
                               w_o[0].astype(BF16), norm_ffn_g[0][None, :], wrt, brt.astype(F32),
                               _tri(MERGE_TILE, True))

    cnt = counts[0, :N_EXPERTS].astype(jnp.int32)
    padded = (cnt + FFN_TILE - 1) // FFN_TILE * FFN_TILE
    p_end = jnp.cumsum(padded)
    p_start = p_end - padded
    e_sel = r[:, 0:2].astype(jnp.int32)
    first_slot = jnp.sum(jnp.where(e_sel[..., None] == jnp.arange(N_EXPERTS), p_start, 0), axis=-1)
    dest = first_slot + r[:, 4:6].astype(jnp.int32)
    n_tiles = (2 * n) // FFN_TILE + N_EXPERTS
    tile_row = jnp.arange(n_tiles, dtype=jnp.int32) * FFN_TILE
    tile_expert = jnp.minimum(jnp.sum((p_end[None, :] <= tile_row[:, None]).astype(jnp.int32), axis=1),
                              N_EXPERTS - 1)
    n_used = (p_end[-1] // FFN_TILE).astype(jnp.int32)
    tile_expert = jnp.where(jnp.arange(n_tiles) < n_used, tile_expert, tile_expert[jnp.maximum(n_used - 1, 0)])

    xs0 = jnp.zeros((n_tiles * FFN_TILE, d), F32)
    xs = _dispatch(dest.reshape(n // DISPATCH_TILE, 1, 2 * DISPATCH_TILE), u2, xs0)
    ys = _ffn(tile_expert, n_used[None], xs, w1[0], w3[0], w2[0])
    out = _final(dest.reshape(n // FINAL_TILE, 1, 2 * FINAL_TILE), h1, r, norm_final_g[None, :], ys)
    return out.reshape(batch, seq, d)
```

```python
import functools
import math

import numpy as np
import jax
import jax.numpy as jnp
from jax import lax
from jax.experimental import pallas as pl
from jax.experimental.pallas import tpu as pltpu

D_MODEL = 1024
CHUNK = 64
N_META = 16
HEAD_DIM = 64
ROPE_DIM = 16
ROPE_THETA = 500000.0
N_HEADS = 8
KV_LATENT = 128
IDX_DIM = 64
TOPK_MAX = 256
N_GROUPS = 4
EXP_PER_GROUP = 8
N_EXPERTS = 32
D_EXPERT = 512
NORM_EPS = 1e-6
HEADS_W = N_HEADS * HEAD_DIM
SPLITS = (HEADS_W, HEADS_W, HEADS_W, HEADS_W, KV_LATENT, ROPE_DIM, N_HEADS * IDX_DIM, IDX_DIM, N_HEADS,
          D_MODEL, D_MODEL)

LANES = 128
QF_W = 2 * LANES
INT_MIN = -2 ** 31
EXP_UNDERFLOW = -104.0
ATTN_SCALE = 1.0 / math.sqrt(HEAD_DIM)
VMEM_LIMIT = 56 * 1024 * 1024
DSA_SUB = 4
DSA_KEYS = DSA_SUB * LANES
ATT_SUB = 2
ATT_KEYS = ATT_SUB * LANES
VAL_ROWS = KV_LATENT + 8
SAFE_LOGIT_BOUND = 40.0
FFN_TILE = 256
MERGE_TILE = 512
FINAL_TILE = 256
DISPATCH_TILE = 512
ROW_DMA_UNROLL = 8

BF16 = jnp.bfloat16
F32 = jnp.float32
I16 = jnp.int16
I16_MIN = np.int16(-2 ** 15)

_C_QKV = (0, 3 * HEADS_W)
_C_QB = (_C_QKV[1], _C_QKV[1] + HEADS_W)
_C_QI = (_C_QB[1], _C_QB[1] + HEADS_W)
_C_KI = (_C_QI[1], _C_QI[1] + LANES)
_C_KV = (_C_KI[1], _C_KI[1] + QF_W)
_C_G = (_C_KV[1], _C_KV[1] + 2 * D_MODEL)
PROJ_W = _C_G[1]


def _dot(a, b):
    return jnp.dot(a, b, preferred_element_type=F32)


def _dot_t(a, b):
    return lax.dot_general(a, b, (((1,), (1,)), ((), ())), preferred_element_type=F32)


def _rmsnorm(x, g):
    return x * lax.rsqrt(jnp.mean(x * x, axis=-1, keepdims=True) + NORM_EPS) * g


def _rope(x, cos, sa, sb):
    w = x.shape[1]
    n = w // LANES
    if n > 1:
        cos, sa, sb = (jnp.concatenate([t] * n, axis=1) for t in (cos, sa, sb))
    return x * cos + pltpu.roll(x, 8, 1) * sa + pltpu.roll(x, w - 8, 1) * sb


def _proj_kernel(x_ref, g_ref, cos_ref, sa_ref, sb_ref, w_ref, wabs_ref, wwi_ref, wckv_ref,
                 qa_ref, kva_ref, qf_ref, qi_ref, ki_ref, kv_ref, gt_ref, wit_ref, ckvt_ref):
    ub = _rmsnorm(x_ref[...], g_ref[...]).astype(BF16)
    cos, sa, sb = cos_ref[...], sa_ref[...], sb_ref[...]
    qa_ref[...] = _dot(ub, w_ref[:, _C_QKV[0]:_C_QKV[0] + HEADS_W]).astype(BF16)
    kva_ref[...] = _dot(ub, w_ref[:, _C_QKV[0] + HEADS_W:_C_QKV[1]]).astype(BF16)
    qb = _rope(_dot(ub, w_ref[:, _C_QB[0]:_C_QB[1]]), cos, sa, sb).astype(BF16)
    qf_ref[...] = _dot(qb, wabs_ref[...]).astype(BF16)
    qi_ref[...] = _rope(_dot(ub, w_ref[:, _C_QI[0]:_C_QI[1]]), cos, sa, sb).astype(BF16)
    ki_ref[...] = _rope(_dot(ub, w_ref[:, _C_KI[0]:_C_KI[1]]), cos, sa, sb).astype(BF16)
    kv = _dot(ub, w_ref[:, _C_KV[0]:_C_KV[1]])
    kv_ref[...] = jnp.concatenate([_rope(kv[:, :LANES], cos, sa, sb), kv[:, LANES:]], axis=1).astype(BF16)
    gt_ref[...] = jax.nn.sigmoid(_dot(ub, w_ref[:, _C_G[0]:_C_G[1]])).astype(BF16)
    wit_ref[...] = _dot_t(wwi_ref[...], ub)
    ckvt_ref[...] = _dot_t(wckv_ref[...], ub).astype(BF16)


def _proj(x2, g, tabs, w, wabs, wwi, wckv, tile, tab_tiles):
    n = x2.shape[0]
    row = lambda width: pl.BlockSpec((tile, width), lambda i: (i, 0))
    const = lambda shape: pl.BlockSpec(shape, lambda i: (0, 0))
    tab = pl.BlockSpec((tile, LANES), lambda i: (i % tab_tiles, 0))
    out_shape = (
        jax.ShapeDtypeStruct((n, HEADS_W), BF16),
        jax.ShapeDtypeStruct((n, 2 * HEADS_W), BF16),
        jax.ShapeDtypeStruct((n, N_HEADS * QF_W), BF16),
        jax.ShapeDtypeStruct((n, HEADS_W), BF16),
        jax.ShapeDtypeStruct((n, LANES), BF16),
        jax.ShapeDtypeStruct((n, QF_W), BF16),
        jax.ShapeDtypeStruct((n, 2 * D_MODEL), BF16),
        jax.ShapeDtypeStruct((16, n), F32),
        jax.ShapeDtypeStruct((KV_LATENT, n), BF16),
    )
    out_specs = (row(HEADS_W), row(2 * HEADS_W), row(N_HEADS * QF_W), row(HEADS_W), row(LANES), row(QF_W),
                 row(2 * D_MODEL),
                 pl.BlockSpec((16, tile), lambda i: (0, i)), pl.BlockSpec((KV_LATENT, tile), lambda i: (0, i)))
    return pl.pallas_call(
        _proj_kernel,
        grid=(n // tile,),
        in_specs=[row(D_MODEL), const((1, D_MODEL)), tab, tab, tab, const(w.shape), const(wabs.shape),
                  const(wwi.shape), const(wckv.shape)],
        out_specs=out_specs,
        out_shape=out_shape,
        compiler_params=pltpu.CompilerParams(dimension_semantics=("arbitrary",), vmem_limit_bytes=VMEM_LIMIT),
        name="proj",
    )(x2, g, *tabs, w, wabs, wwi, wckv)


def _sb_kernel(q_ref, k_ref, v_ref, su_ref, o_ref, acc_ref, carry_ref):
    ib = pl.program_id(1) + 1
    lane = lax.broadcasted_iota(jnp.int32, (LANES, LANES), 1)
    row = lax.broadcasted_iota(jnp.int32, (LANES, LANES), 0)
    low = lane < HEAD_DIM
    high = jnp.logical_not(low)
    su = su_ref[...]
    qpos = ib * LANES + row
    acc_ref[...] = jnp.zeros_like(acc_ref)
    carry_ref[...] = jnp.zeros_like(carry_ref)

    def body(st):
        j, _ = st
        rows = pl.ds(pl.multiple_of(j * LANES, LANES), LANES)
        kpos = j * LANES + lane
        mask = ((kpos < qpos) & ((kpos < N_META) | (kpos >= LANES)))[None]
        zs = []
        for p in range(N_HEADS // 2):
            cols = slice(p * LANES, (p + 1) * LANES)
            q2, kb = q_ref[:, cols], k_ref[rows, cols]
            zero = jnp.zeros_like(q2)
            for keep in (low, high):
                zs.append(_dot_t(jnp.where(keep, q2, zero), kb))
        z = jnp.stack(zs)
        ls_pos = jnp.minimum(z, 0.0) - jnp.log(1.0 + jnp.exp(-jnp.abs(z)))
        log_keep = jnp.where(mask, ls_pos - z, 0.0)
        hi = log_keep.astype(BF16)
        lo = (log_keep - hi.astype(F32)).astype(BF16)
        flat = (N_HEADS * LANES, LANES)
        cs = _dot(hi.reshape(flat), su) + _dot(lo.reshape(flat), su)
        carry = carry_ref[...]
        a = jnp.where(mask, jnp.exp(ls_pos + cs[:, :LANES].reshape(z.shape) + carry), 0.0).astype(BF16)
        carry = carry + cs[:, LANES:].reshape(z.shape)
        carry_ref[...] = carry
        for p in range(N_HEADS // 2):
            vb = v_ref[rows, p * LANES:(p + 1) * LANES]
            zero = jnp.zeros_like(vb)
            acc_ref[p] += _dot(a[2 * p], jnp.where(low, vb, zero)) + _dot(a[2 * p + 1], jnp.where(high, vb, zero))
        return j - 1, jnp.max(carry)

    def cond(st):
        return (st[0] >= 0) & (st[1] > EXP_UNDERFLOW)

    lax.while_loop(cond, body, (ib, jnp.float32(0.0)))
    for p in range(N_HEADS // 2):
        o_ref[:, p * LANES:(p + 1) * LANES] = acc_ref[p].astype(BF16)


def _sb(qkv_x, kvp, su, batch, seq):
    nq = seq // LANES
    tp = seq + LANES
    return pl.pallas_call(
        _sb_kernel,
        grid=(batch, nq),
        in_specs=[pl.BlockSpec((LANES, HEADS_W), lambda b, i: (b * nq + i, 0)),
                  pl.BlockSpec((tp, HEADS_W), lambda b, i: (b, 0)),
                  pl.BlockSpec((tp, HEADS_W), lambda b, i: (b, 1)),
                  pl.BlockSpec(su.shape, lambda b, i: (0, 0))],
        out_specs=pl.BlockSpec((LANES, HEADS_W), lambda b, i: (b * nq + i, 0)),
        out_shape=jax.ShapeDtypeStruct((batch * seq, HEADS_W), BF16),
        scratch_shapes=[pltpu.VMEM((N_HEADS // 2, LANES, LANES), F32), pltpu.VMEM((N_HEADS, LANES, LANES), F32)],
        compiler_params=pltpu.CompilerParams(dimension_semantics=("arbitrary",) * 2, vmem_limit_bytes=VMEM_LIMIT),
        name="sb",
    )(qkv_x, kvp, kvp, su)


def _dsa_kernel(qi_ref, wit_ref, qf_ref, ki_ref, kv_ref, ckvt_ref, lt_ref, wuvt_ref, o_ref,
                sk_ref, hi_ref, lo_ref, acc_ref, m_ref, kvn_ref, qim_ref, qfp_ref, ybt_ref, *, ktop):
    ib = pl.program_id(1) + 1
    nt = (ib + DSA_SUB) // DSA_SUB
    lane = lax.broadcasted_iota(jnp.int32, (LANES, LANES), 1)
    klane = lax.broadcasted_iota(jnp.int32, (DSA_KEYS, LANES), 1)
    krow = lax.broadcasted_iota(jnp.int32, (DSA_KEYS, LANES), 0)
    low = lane < HEAD_DIM
    pairs = range(N_HEADS // 2)
    for p in pairs:
        blk = qi_ref[:, p * LANES:(p + 1) * LANES]
        zero = jnp.zeros_like(blk)
        qim_ref[2 * p * LANES:(2 * p + 1) * LANES, :] = jnp.where(low, blk, zero)
        qim_ref[(2 * p + 1) * LANES:(2 * p + 2) * LANES, :] = jnp.where(low, zero, blk)
        qfp_ref[2 * p * LANES:(2 * p + 1) * LANES, :] = qf_ref[:, 2 * p * QF_W:(2 * p + 1) * QF_W]
        qfp_ref[(2 * p + 1) * LANES:(2 * p + 2) * LANES, :] = qf_ref[:, (2 * p + 1) * QF_W:(2 * p + 2) * QF_W]
    wit = wit_ref[...]
    wit_all = jnp.concatenate([wit[h:h + 1, :] for h in range(N_HEADS)], axis=1)
    cq = 1 + ((ib * LANES + klane - LANES) >> 6)

    def tile(t):
        return pl.ds(pl.multiple_of(t * DSA_KEYS, DSA_KEYS), DSA_KEYS)

    def scores(t, c):
        kb = ki_ref[tile(t), :]
        part = wit_all * jnp.maximum(_dot_t(kb, qim_ref[...]), 0.0)
        s = part[:, :LANES]
        for h in range(1, N_HEADS):
            s = s + part[:, h * LANES:(h + 1) * LANES]
        s = jnp.where(s == 0.0, 0.0, s)
        bits = lax.bitcast_convert_type(s, jnp.int32)
        key = bits ^ ((bits >> 31) & 0x7FFFFFFF)
        rk = t * DSA_KEYS + krow
        ck = jnp.where(rk < LANES, 0, 1 + ((rk - LANES) >> 6))
        adm = ((rk < N_META) | (rk >= LANES)) & (ck <= cq)
        key = jnp.where(adm, key, INT_MIN)
        sk_ref[tile(t), :] = key
        hi_ref[tile(t), :] = (key >> 16).astype(I16)
        lo_ref[tile(t), :] = key.astype(I16) ^ I16_MIN
        return c

    lax.fori_loop(0, nt, scores, 0)

    def count(pred):
        def step(t, acc):
            hit = pred(sk_ref[tile(t), :]).astype(jnp.int32)
            for u in range(DSA_SUB):
                acc = acc + hit[u * LANES:(u + 1) * LANES]
            return acc
        acc = lax.fori_loop(0, nt, step, jnp.zeros((LANES, LANES), jnp.int32))
        return jnp.sum(acc, axis=0, keepdims=True)

    def count16(ref, pred):
        one, zero = jnp.ones((DSA_KEYS, LANES), I16), jnp.zeros((DSA_KEYS, LANES), I16)

        def step(t, acc):
            hit = jnp.where(pred(ref[tile(t), :]), one, zero)
            for u in range(DSA_SUB):
                acc = acc + hit[u * LANES:(u + 1) * LANES]
            return acc
        acc = lax.fori_loop(0, nt, step, jnp.zeros((LANES, LANES), I16))
        return jnp.sum(acc.astype(jnp.int32), axis=0, keepdims=True)

    def search16(ref, target, count_at_lowest):
        zero_row = jnp.zeros((1, LANES), jnp.int32)
        c = count16(ref, lambda v: v >= zero_row.astype(I16))
        ok = c >= target
        state = (jnp.where(ok, zero_row, -2 ** 15), jnp.where(ok, c, count_at_lowest))

        def bit_step(it, st):
            best, best_count = st
            cand = best | jnp.left_shift(jnp.int32(1), 14 - it)
            c = count16(ref, lambda v: v >= cand.astype(I16))
            ok = c >= target
            return jnp.where(ok, cand, best), jnp.where(ok, c, best_count)

        return lax.fori_loop(0, 15, bit_step, state)

    thr_hi, count_hi = search16(hi_ref, ktop, nt * DSA_KEYS)
    thr_hi16 = thr_hi.astype(I16)
    above = count16(hi_ref, lambda v: v > thr_hi16)

    def bucket_only(t, c):
        lo_ref[tile(t), :] = jnp.where(hi_ref[tile(t), :] == thr_hi16, lo_ref[tile(t), :], I16_MIN)
        return c

    lax.fori_loop(0, nt, bucket_only, 0)
    thr_lo, count_lo = search16(lo_ref, ktop - above, count_hi - above)
    thr = thr_hi * 65536 + (thr_lo + 2 ** 15)
    excess = jnp.where(thr == INT_MIN, 0, above + count_lo - ktop)

    @pl.when(pl.program_id(1) == 0)
    def _():
        def widest(t, m):
            kvf = kv_ref[tile(t), :].astype(F32)
            return jnp.maximum(m, jnp.max(jnp.sum(kvf * kvf, axis=1, keepdims=True)))
        kvn_ref[...] = jnp.full(kvn_ref.shape, lax.fori_loop(0, kv_ref.shape[0] // DSA_KEYS, widest, jnp.float32(0.0)))

    ones = jnp.ones((8, QF_W), BF16)
    bounds = []
    for p in pairs:
        qp = qfp_ref[2 * p * LANES:(2 * p + 2) * LANES, :].astype(F32)
        qn2 = _dot_t(ones, (qp * qp).astype(BF16))[0:1, :]
        kvn = kvn_ref[0:1, :]
        bounds.append(jnp.sqrt(qn2 * jnp.concatenate([kvn, kvn], axis=1)) * 1.05 + 1e-6)
    bound_max = functools.reduce(jnp.maximum, bounds)
    bound_all = jnp.concatenate(bounds, axis=1)
    fast = (jnp.max(excess) == 0) & (jnp.max(bound_max) <= SAFE_LOGIT_BOUND)
    acc_ref[...] = jnp.zeros_like(acc_ref)

    def twice(a):
        return jnp.concatenate([a, a], axis=1)

    def atile(t):
        return pl.ds(pl.multiple_of(t * ATT_KEYS, ATT_KEYS), ATT_KEYS)

    nta = (ib + ATT_SUB) // ATT_SUB

    @pl.when(fast)
    def _():
        def attend(t, c):
            skt = sk_ref[atile(t), :]
            sel = jnp.where((skt >= thr) & (skt > INT_MIN), 1.0, 0.0).astype(BF16)
            logits = _dot_t(kv_ref[atile(t), :], qfp_ref[...])
            prob = jnp.exp(logits - bound_all).astype(BF16) * jnp.concatenate([sel] * N_HEADS, axis=1)
            acc_ref[...] += _dot(ckvt_ref[t], prob)
            return c

        lax.fori_loop(0, nta, attend, 0)

    @pl.when(jnp.logical_not(fast))
    def _():
        need = (ktop - count(lambda k: k > thr)).astype(F32)
        m_ref[...] = jnp.full_like(m_ref, -1e29)
        lt = lt_ref[...]

        def attend(t, tie_carry):
            skt = sk_ref[atile(t), :]
            tie = (skt == thr) & (skt > INT_MIN)
            rank = _dot(lt, jnp.where(tie, 1.0, 0.0).astype(BF16)) + tie_carry
            sel = twice((skt > thr) | (tie & (rank <= need)))
            kvb = kv_ref[atile(t), :]
            cb = ckvt_ref[t]
            for p in pairs:
                pc = slice(2 * p * LANES, (2 * p + 2) * LANES)
                s = jnp.where(sel, _dot_t(kvb, qfp_ref[pc, :]), -1e30)
                m_old = m_ref[p:p + 1, :]
                m_new = jnp.maximum(m_old, jnp.max(s, axis=0, keepdims=True))
                prob = jnp.exp(s - m_new).astype(BF16)
                acc_ref[:, pc] = acc_ref[:, pc] * jnp.exp(m_old - m_new) + _dot(cb, prob)
                m_ref[p:p + 1, :] = m_new
            return rank[ATT_KEYS - 1:ATT_KEYS, :]

        lax.fori_loop(0, nta, attend, jnp.zeros((1, LANES), F32))

    for h in range(N_HEADS):
        cols = slice(h * LANES, (h + 1) * LANES)
        o = (acc_ref[:KV_LATENT, cols] / acc_ref[KV_LATENT:KV_LATENT + 1, cols]).astype(BF16)
        ybt_ref[h * HEAD_DIM:(h + 1) * HEAD_DIM, :] = _dot(wuvt_ref[h], o)
    o_ref[...] = ybt_ref[...].T.astype(BF16)


def _dsa(qi, wit, qf, kip, kvp, ckvt3, lt, wuvt, batch, seq, ktop):
    nq = seq // LANES
    tpk = kip.shape[0] // batch
    ntk = tpk // ATT_KEYS
    return pl.pallas_call(
        functools.partial(_dsa_kernel, ktop=ktop),
        grid=(batch, nq),
        in_specs=[pl.BlockSpec((LANES, HEADS_W), lambda b, i: (b * nq + i, 0)),
                  pl.BlockSpec((16, LANES), lambda b, i: (0, b * nq + i)),
                  pl.BlockSpec((LANES, N_HEADS * QF_W), lambda b, i: (b * nq + i, 0)),
                  pl.BlockSpec((tpk, LANES), lambda b, i: (b, 0)),
                  pl.BlockSpec((tpk, QF_W), lambda b, i: (b, 0)),
                  pl.BlockSpec((ntk, VAL_ROWS, ATT_KEYS), lambda b, i: (b, 0, 0)),
                  pl.BlockSpec(lt.shape, lambda b, i: (0, 0)),
                  pl.BlockSpec(wuvt.shape, lambda b, i: (0, 0, 0))],
        out_specs=pl.BlockSpec((LANES, HEADS_W), lambda b, i: (b * nq + i, 0)),
        out_shape=jax.ShapeDtypeStruct((batch * seq, HEADS_W), BF16),
        scratch_shapes=[pltpu.VMEM((tpk, LANES), jnp.int32),
                        pltpu.VMEM((tpk, LANES), I16),
                        pltpu.VMEM((tpk, LANES), I16),
                        pltpu.VMEM((VAL_ROWS, N_HEADS * LANES), F32),
                        pltpu.VMEM((8, 2 * LANES), F32),
                        pltpu.VMEM((8, LANES), F32),
                        pltpu.VMEM((N_HEADS * LANES, LANES), BF16),
                        pltpu.VMEM((N_HEADS * LANES, QF_W), BF16),
                        pltpu.VMEM((HEADS_W, LANES), F32)],
        compiler_params=pltpu.CompilerParams(dimension_semantics=("arbitrary",) * 2, vmem_limit_bytes=VMEM_LIMIT),
        name="dsa",
    )(qi, wit, qf, kip, kvp, ckvt3, lt, wuvt)


def _merge_kernel(x_ref, ya_ref, yb_ref, gt_ref, wua_ref, wub_ref, wo_ref, g_ref, wrt_ref, brt_ref, slt_ref,
                  h1_ref, u2_ref, r_ref, cnt_ref, carry_ref):
    @pl.when(pl.program_id(0) == 0)
    def _():
        carry_ref[...] = jnp.zeros_like(carry_ref)

    gt = gt_ref[...].astype(F32)
    z = gt[:, :D_MODEL] * _dot(ya_ref[...], wua_ref[...]) + gt[:, D_MODEL:] * _dot(yb_ref[...], wub_ref[...])
    h1 = x_ref[...] + _dot(z.astype(BF16), wo_ref[...])
    h1_ref[...] = h1
    u2 = _rmsnorm(h1, g_ref[...])
    u2_ref[...] = u2
    u_hi = u2.astype(BF16)
    u_lo = (u2 - u_hi.astype(F32)).astype(BF16)
    hi_terms = _dot(u_hi, wrt_ref[...])
    lg = hi_terms[:, :LANES] + hi_terms[:, LANES:] + _dot(u_lo, wrt_ref[:, :LANES]) + brt_ref[...]

    tm = lg.shape[0]
    lane = lax.broadcasted_iota(jnp.int32, (tm, LANES), 1)
    big = jnp.int32(LANES)

    def softmax(mask):
        x = jnp.where(mask, lg, -jnp.inf)
        e = jnp.where(mask, jnp.exp(x - jnp.max(x, axis=-1, keepdims=True)), 0.0)
        return e / jnp.sum(e, axis=-1, keepdims=True)

    def top1(p, mask):
        best = jnp.max(jnp.where(mask, p, -1.0), axis=-1, keepdims=True)
        idx = jnp.min(jnp.where(mask & (p == best), lane, big), axis=-1, keepdims=True)
        return best, idx

    gmask = (lane >= N_EXPERTS) & (lane < N_EXPERTS + N_GROUPS)
    p_grp, g_lane = top1(softmax(gmask), gmask)
    first = (g_lane - N_EXPERTS) * EXP_PER_GROUP
    emask = (lane >= first) & (lane < first + EXP_PER_GROUP)
    ep = softmax(emask)
    p0, e0 = top1(ep, emask)
    rest = emask & (lane != e0)
    p1, e1 = top1(ep, rest)
    den = p0 + p1
    w0 = p_grp * p0 / den
    w1 = p_grp * p1 / den

    oh0 = lane == e0
    oh1 = lane == e1
    both = (oh0 | oh1).astype(BF16)
    before = _dot(slt_ref[...], both) + carry_ref[...]
    rank0 = jnp.sum(jnp.where(oh0, before, 0.0), axis=-1, keepdims=True)
    rank1 = jnp.sum(jnp.where(oh1, before, 0.0), axis=-1, keepdims=True)
    carry = carry_ref[...] + jnp.sum(both.astype(F32), axis=0, keepdims=True)
    carry_ref[...] = carry
    cnt_ref[...] = carry

    cols = (e0.astype(F32), e1.astype(F32), w0, w1, rank0, rank1)
    r = jnp.zeros((tm, LANES), F32)
    for c, v in enumerate(cols):
        r = jnp.where(lane == c, v, r)
    r_ref[...] = r


def _merge(x2, ya, yb, gt, wua, wub, wo, g, wrt, brt, slt):
    n = x2.shape[0]
    tm = MERGE_TILE
    row = lambda width: pl.BlockSpec((tm, width), lambda i: (i, 0))
    const = lambda a: pl.BlockSpec(a.shape, lambda i: (0,) * a.ndim)
    return pl.pallas_call(
        _merge_kernel,
        grid=(n // tm,),
        in_specs=[row(D_MODEL), row(HEADS_W), row(HEADS_W), row(2 * D_MODEL), const(wua), const(wub), const(wo),
                  const(g), const(wrt), const(brt), const(slt)],
        out_specs=(row(D_MODEL), row(D_MODEL), row(LANES), pl.BlockSpec((1, LANES), lambda i: (0, 0))),
        out_shape=(jax.ShapeDtypeStruct((n, D_MODEL), F32), jax.ShapeDtypeStruct((n, D_MODEL), F32),
                   jax.ShapeDtypeStruct((n, LANES), F32), jax.ShapeDtypeStruct((1, LANES), F32)),
        scratch_shapes=[pltpu.VMEM((1, LANES), F32)],
        compiler_params=pltpu.CompilerParams(dimension_semantics=("arbitrary",), vmem_limit_bytes=VMEM_LIMIT),
        name="merge",
    )(x2, ya, yb, gt, wua, wub, wo, g, wrt, brt, slt)


def _row_copy(src, src_row, dst, dst_row, sem):
    return pltpu.make_async_copy(src.at[pl.ds(src_row, 1)], dst.at[pl.ds(dst_row, 1)], sem)


def _dispatch_kernel(dest_ref, u_ref, xs_in_ref, xs_ref, sem):
    del xs_in_ref

    def issue(t, c):
        for s in range(2):
            _row_copy(u_ref, t, xs_ref, dest_ref[0, 0, 2 * t + s], sem).start(priority=s)
        return c

    def drain(t, c):
        for s in range(2):
            _row_copy(u_ref, t, xs_ref, dest_ref[0, 0, 2 * t + s], sem).wait()
        return c

    lax.fori_loop(0, DISPATCH_TILE, issue, 0, unroll=ROW_DMA_UNROLL)
    lax.fori_loop(0, DISPATCH_TILE, drain, 0, unroll=ROW_DMA_UNROLL)


def _dispatch(dest3, u2, xs0):
    n = u2.shape[0]
    return pl.pallas_call(
        _dispatch_kernel,
        grid=(n // DISPATCH_TILE,),
        in_specs=[pl.BlockSpec((1, 1, 2 * DISPATCH_TILE), lambda i: (i, 0, 0), memory_space=pltpu.SMEM),
                  pl.BlockSpec((DISPATCH_TILE, D_MODEL), lambda i: (i, 0)), pl.BlockSpec(memory_space=pl.ANY)],
        out_specs=pl.BlockSpec(memory_space=pl.ANY),
        out_shape=jax.ShapeDtypeStruct(xs0.shape, xs0.dtype),
        scratch_shapes=[pltpu.SemaphoreType.DMA],
        input_output_aliases={2: 0},
        compiler_params=pltpu.CompilerParams(dimension_semantics=("arbitrary",), has_side_effects=True),
        name="dispatch",
    )(dest3, u2, xs0)


def _ffn_kernel(te_ref, nu_ref, xs_ref, w1_ref, w3_ref, w2_ref, ys_ref, w1b_ref, w3b_ref, w2b_ref):
    t = pl.program_id(0)

    @pl.when((t == 0) | (te_ref[t] != te_ref[jnp.maximum(t - 1, 0)]))
    def _():
        w1b_ref[...] = w1_ref[0].astype(BF16)
        w3b_ref[...] = w3_ref[0].astype(BF16)
        w2b_ref[...] = w2_ref[0].astype(BF16)

    @pl.when(t < nu_ref[0])
    def _():
        x = xs_ref[...].astype(BF16)
        hid = jax.nn.silu(_dot(x, w1b_ref[...])) * _dot(x, w3b_ref[...])
        ys_ref[...] = _dot(hid.astype(BF16), w2b_ref[...])

    @pl.when(t >= nu_ref[0])
    def _():
        ys_ref[...] = jnp.zeros_like(ys_ref)


def _ffn(tile_expert, n_used, xs, w1, w3, w2):
    p = xs.shape[0]
    grid_spec = pltpu.PrefetchScalarGridSpec(
        num_scalar_prefetch=2,
        grid=(p // FFN_TILE,),
        in_specs=[pl.BlockSpec((FFN_TILE, D_MODEL), lambda t, te, nu: (t, 0)),
                  pl.BlockSpec((1, D_MODEL, D_EXPERT), lambda t, te, nu: (te[t], 0, 0)),
                  pl.BlockSpec((1, D_MODEL, D_EXPERT), lambda t, te, nu: (te[t], 0, 0)),
                  pl.BlockSpec((1, D_EXPERT, D_MODEL), lambda t, te, nu: (te[t], 0, 0))],
        out_specs=pl.BlockSpec((FFN_TILE, D_MODEL), lambda t, te, nu: (t, 0)),
        scratch_shapes=[pltpu.VMEM((D_MODEL, D_EXPERT), BF16), pltpu.VMEM((D_MODEL, D_EXPERT), BF16),
                        pltpu.VMEM((D_EXPERT, D_MODEL), BF16)],
    )
    return pl.pallas_call(
        _ffn_kernel,
        grid_spec=grid_spec,
        out_shape=jax.ShapeDtypeStruct((p, D_MODEL), F32),
        compiler_params=pltpu.CompilerParams(dimension_semantics=("arbitrary",), vmem_limit_bytes=VMEM_LIMIT),
        name="ffn",
    )(tile_expert, n_used, xs, w1, w3, w2)


def _final_kernel(dest_ref, h1_ref, r_ref, g_ref, ys_ref, o_ref, buf_ref, sem):
    def issue(t, c):
        for s in range(2):
            _row_copy(ys_ref, dest_ref[0, 0, 2 * t + s], buf_ref.at[s], t, sem).start(priority=s)
        return c

    def drain(t, c):
        for s in range(2):
            _row_copy(ys_ref, dest_ref[0, 0, 2 * t + s], buf_ref.at[s], t, sem).wait()
        return c

    lax.fori_loop(0, FINAL_TILE, issue, 0, unroll=ROW_DMA_UNROLL)
    lax.fori_loop(0, FINAL_TILE, drain, 0, unroll=ROW_DMA_UNROLL)
    r = r_ref[...]
    moe = r[:, 2:3] * buf_ref[0] + r[:, 3:4] * buf_ref[1]
    o_ref[...] = _rmsnorm(h1_ref[...] + moe, g_ref[...])


def _final(dest3, h1, r, g, ys):
    n = h1.shape[0]
    tm = FINAL_TILE
    return pl.pallas_call(
        _final_kernel,
        grid=(n // tm,),
        in_specs=[pl.BlockSpec((1, 1, 2 * tm), lambda i: (i, 0, 0), memory_space=pltpu.SMEM),
                  pl.BlockSpec((tm, D_MODEL), lambda i: (i, 0)),
                  pl.BlockSpec((tm, LANES), lambda i: (i, 0)),
                  pl.BlockSpec((1, D_MODEL), lambda i: (0, 0)),
                  pl.BlockSpec(memory_space=pl.ANY)],
        out_specs=pl.BlockSpec((tm, D_MODEL), lambda i: (i, 0)),
        out_shape=jax.ShapeDtypeStruct((n, D_MODEL), F32),
        scratch_shapes=[pltpu.VMEM((2, tm, D_MODEL), F32), pltpu.SemaphoreType.DMA],
        compiler_params=pltpu.CompilerParams(dimension_semantics=("arbitrary",), vmem_limit_bytes=VMEM_LIMIT),
        name="final",
    )(dest3, h1, r, g, ys)


def _rope_tables(pos):
    half = ROPE_DIM // 2
    inv = np.float32(ROPE_THETA) ** (-np.arange(half, dtype=np.float32) / np.float32(half))
    ang = np.asarray(pos, np.float32)[:, None] * inv[None, :]
    cos, sin = np.cos(ang).astype(np.float32), np.sin(ang).astype(np.float32)
    d = np.arange(LANES) % HEAD_DIM
    f = d % half
    cos_t = np.where(d < ROPE_DIM, cos[:, f], np.float32(1))
    sa = np.where((d >= half) & (d < ROPE_DIM), sin[:, f], np.float32(0))
    sb = np.where(d < half, -sin[:, f], np.float32(0))
    return tuple(jnp.asarray(t, F32) for t in (cos_t, sa, sb))


def _proj_weights(w_in, w_uk):
    qa, ka, va, qb, ckv, krope, qi, ki, wi, ga, gb = jnp.split(w_in, list(np.cumsum(SPLITS)[:-1]), axis=-1)
    zeros = lambda width: jnp.zeros((D_MODEL, width), w_in.dtype)
    kv = jnp.concatenate([krope, zeros(LANES - ROPE_DIM), ckv], axis=1)
    w = jnp.concatenate([qa * ATTN_SCALE, ka, va, qb, qi, ki, ki, kv, ga, gb], axis=1).astype(BF16)
    assert w.shape[1] == PROJ_W
    nope = HEAD_DIM - ROPE_DIM
    per_head = jnp.concatenate([
        jnp.concatenate([jnp.eye(ROPE_DIM, dtype=F32)[None].repeat(N_HEADS, 0),
                         jnp.zeros((N_HEADS, ROPE_DIM, QF_W - ROPE_DIM), F32)], axis=2),
        jnp.concatenate([jnp.zeros((N_HEADS, nope, LANES), F32), jnp.swapaxes(w_uk, 1, 2)], axis=2)], axis=1)
    head_eye = jnp.eye(N_HEADS, dtype=F32)
    wabs = (per_head[:, :, None, :] * (head_eye * ATTN_SCALE)[:, None, :, None])
    wabs = wabs.reshape(HEADS_W, N_HEADS * QF_W).astype(BF16)
    wwi = jnp.concatenate([wi.T, jnp.zeros((16 - N_HEADS, D_MODEL), w_in.dtype)], axis=0).astype(BF16)
    wckv = ckv.T.astype(BF16)
    return w, wabs, wwi, wckv


def _tri(n, strict_lower):
    r = np.arange(n)
    m = (r[None, :] < r[:, None]) if strict_lower else (r[None, :] <= r[:, None])
    return jnp.asarray(m, BF16)


def kernel(x, meta_tokens, norm_mix_g, w_in, w_uk, w_uv, w_up_a, w_up_b, w_o, norm_ffn_g, w_group, b_group,
           w_router, b_router, w1, w3, w2, norm_final_g):
    batch, seq, d = x.shape
    assert d == D_MODEL and seq % LANES == 0 and norm_mix_g.shape[0] == 1
    n = batch * seq
    nq = seq // LANES
    tp = seq + LANES
    ktop = min(TOPK_MAX, seq // 4)
    x2 = x.reshape(n, d)

    w, wabs, wwi, wckv = _proj_weights(w_in[0], w_uk[0])
    g_mix = norm_mix_g[0][None, :]
    proj_tile = 512 if seq % 512 == 0 else LANES
    px = _proj(x2, g_mix, _rope_tables(N_META + np.arange(seq)), w, wabs, wwi, wckv, proj_tile, seq // proj_tile)
    meta = jnp.concatenate([meta_tokens.astype(x.dtype), jnp.zeros((LANES - N_META, d), x.dtype)], axis=0)
    pm = _proj(meta, g_mix, _rope_tables(np.arange(LANES)), w, wabs, wwi, wckv, LANES, 1)
    qa_x, kva_x, qf_x, qi_x, ki_x, kv_x, gt_x, wit_x, ckvt_x = px
    _, kva_m, _, _, ki_m, kv_m, _, _, ckvt_m = pm

    def with_meta(xpart, mpart, rows):
        wdt = xpart.shape[1]
        parts = [jnp.broadcast_to(mpart[None], (batch, LANES, wdt)), xpart.reshape(batch, seq, wdt)]
        if rows > tp:
            parts.append(jnp.zeros((batch, rows - tp, wdt), xpart.dtype))
        return jnp.concatenate(parts, axis=1).reshape(batch * rows, wdt)

    kv_a = with_meta(kva_x, kva_m, tp)
    su = jnp.concatenate([_tri(LANES, True), jnp.ones((LANES, LANES), BF16)], axis=1)
    ya = _sb(qa_x, kv_a, su, batch, seq)

    tpk = -(-tp // DSA_KEYS) * DSA_KEYS
    kip = with_meta(ki_x, ki_m, tpk)
    kvp = with_meta(kv_x, kv_m, tpk)
    ckvt = jnp.concatenate([jnp.broadcast_to(ckvt_m[:, None, :], (KV_LATENT, batch, LANES)),
                            ckvt_x.reshape(KV_LATENT, batch, seq),
                            jnp.zeros((KV_LATENT, batch, tpk - tp), BF16)], axis=2)
    ones_row = jnp.concatenate([jnp.ones((1, batch, tpk), BF16),
                                jnp.zeros((VAL_ROWS - KV_LATENT - 1, batch, tpk), BF16)], axis=0)
    ckvt = jnp.concatenate([ckvt, ones_row], axis=0)
    ckvt3 = ckvt.reshape(VAL_ROWS, batch * tpk // ATT_KEYS, ATT_KEYS).transpose(1, 0, 2)
    wuvt = jnp.swapaxes(w_uv[0], 1, 2).astype(BF16)
    yb = _dsa(qi_x, wit_x, qf_x, kip, kvp, ckvt3, _tri(ATT_KEYS, False), wuvt, batch, seq, ktop)

    wrt = jnp.concatenate([w_router[0], w_group[0],
                           jnp.zeros((d, LANES - N_EXPERTS - N_GROUPS), F32)], axis=1).astype(F32)
    wrt_hi = wrt.astype(BF16)
    wrt = jnp.concatenate([wrt_hi, (wrt - wrt_hi.astype(F32)).astype(BF16)], axis=1)
    brt = jnp.concatenate([b_router[0], b_group[0], jnp.zeros((LANES - N_EXPERTS - N_GROUPS,), F32)])[None, :]
    h1, u2, r, counts = _merge(x2, ya, yb, gt_x, w_up_a[0].astype(BF16), w_up_b[0].astype(BF16),
                               w_o[0].astype(BF16), norm_ffn_g[0][None, :], wrt, brt.astype(F32),
                               _tri(MERGE_TILE, True))

    cnt = counts[0, :N_EXPERTS].astype(jnp.int32)
    padded = (cnt + FFN_TILE - 1) // FFN_TILE * FFN_TILE
    p_end = jnp.cumsum(padded)
    p_start = p_end - padded
    e_sel = r[:, 0:2].astype(jnp.int32)
    first_slot = jnp.sum(jnp.where(e_sel[..., None] == jnp.arange(N_EXPERTS), p_start, 0), axis=-1)
    dest = first_slot + r[:, 4:6].astype(jnp.int32)
    n_tiles = (2 * n) // FFN_TILE + N_EXPERTS
    tile_row = jnp.arange(n_tiles, dtype=jnp.int32) * FFN_TILE
    tile_expert = jnp.minimum(jnp.sum((p_end[None, :] <= tile_row[:, None]).astype(jnp.int32), axis=1),
                              N_EXPERTS - 1)
    n_used = (p_end[-1] // FFN_TILE).astype(jnp.int32)
    tile_expert = jnp.where(jnp.arange(n_tiles) < n_used, tile_expert, tile_expert[jnp.maximum(n_used - 1, 0)])

    xs0 = jnp.zeros((n_tiles * FFN_TILE, d), F32)
    xs = _dispatch(dest.reshape(n // DISPATCH_TILE, 1, 2 * DISPATCH_TILE), u2, xs0)
    ys = _ffn(tile_expert, n_used[None], xs, w1[0], w3[0], w2[0])
    out = _final(dest.reshape(n // FINAL_TILE, 1, 2 * FINAL_TILE), h1, r, norm_final_g[None, :], ys)
    return out.reshape(batch, seq, d)
```

```python
import functools
import math

import numpy as np
import jax
import jax.numpy as jnp
from jax import lax
from jax.experimental import pallas as pl
from jax.experimental.pallas import tpu as pltpu

D_MODEL = 1024
CHUNK = 64
N_META = 16
HEAD_DIM = 64
ROPE_DIM = 16
ROPE_THETA = 500000.0
N_HEADS = 8
KV_LATENT = 128
IDX_DIM = 64
TOPK_MAX = 256
N_GROUPS = 4
EXP_PER_GROUP = 8
N_EXPERTS = 32
D_EXPERT = 512
NORM_EPS = 1e-6
HEADS_W = N_HEADS * HEAD_DIM
SPLITS = (HEADS_W, HEADS_W, HEADS_W, HEADS_W, KV_LATENT, ROPE_DIM, N_HEADS * IDX_DIM, IDX_DIM, N_HEADS,
          D_MODEL, D_MODEL)

LANES = 128
QF_W = 2 * LANES
INT_MIN = -2 ** 31
EXP_UNDERFLOW = -104.0
ATTN_SCALE = 1.0 / math.sqrt(HEAD_DIM)
VMEM_LIMIT = 56 * 1024 * 1024
DSA_SUB = 4
DSA_KEYS = DSA_SUB * LANES
ATT_SUB = 4
ATT_KEYS = ATT_SUB * LANES
CNT_SUB = 2
CNT_KEYS = CNT_SUB * LANES
VAL_ROWS = KV_LATENT + 8
SAFE_LOGIT_BOUND = 40.0
FFN_TILE = 512
MERGE_TILE = 1024
FINAL_TILE = 1024
DISPATCH_TILE = 1024
ROW_DMA_UNROLL = 32

BF16 = jnp.bfloat16
F32 = jnp.float32

_C_QKV = (0, 3 * HEADS_W)
_C_QB = (_C_QKV[1], _C_QKV[1] + HEADS_W)
_C_QI = (_C_QB[1], _C_QB[1] + HEADS_W)
_C_KI = (_C_QI[1], _C_QI[1] + LANES)
_C_KV = (_C_KI[1], _C_KI[1] + QF_W)
_C_G = (_C_KV[1], _C_KV[1] + 2 * D_MODEL)
PROJ_W = _C_G[1]


def _dot(a, b):
    return jnp.dot(a, b, preferred_element_type=F32)


def _dot_t(a, b):
    return lax.dot_general(a, b, (((1,), (1,)), ((), ())), preferred_element_type=F32)


def _rmsnorm(x, g):
    return x * lax.rsqrt(jnp.mean(x * x, axis=-1, keepdims=True) + NORM_EPS) * g


def _rope(x, cos, sa, sb):
    w = x.shape[1]
    n = w // LANES
    if n > 1:
        cos, sa, sb = (jnp.concatenate([t] * n, axis=1) for t in (cos, sa, sb))
    return x * cos + pltpu.roll(x, 8, 1) * sa + pltpu.roll(x, w - 8, 1) * sb


def _proj_kernel(x_ref, g_ref, cos_ref, sa_ref, sb_ref, w_ref, wabs_ref, wwi_ref, wckv_ref,
                 qa_ref, kva_ref, qf_ref, qi_ref, ki_ref, kv_ref, gt_ref, wit_ref, ckvt_ref):
    ub = _rmsnorm(x_ref[...], g_ref[...]).astype(BF16)
    cos, sa, sb = cos_ref[...], sa_ref[...], sb_ref[...]
    qa_ref[...] = _dot(ub, w_ref[:, _C_QKV[0]:_C_QKV[0] + HEADS_W]).astype(BF16)
    kva_ref[...] = _dot(ub, w_ref[:, _C_QKV[0] + HEADS_W:_C_QKV[1]]).astype(BF16)
    qb = _rope(_dot(ub, w_ref[:, _C_QB[0]:_C_QB[1]]), cos, sa, sb).astype(BF16)
    qf_ref[...] = _dot(qb, wabs_ref[...]).astype(BF16)
    qi_ref[...] = _rope(_dot(ub, w_ref[:, _C_QI[0]:_C_QI[1]]), cos, sa, sb).astype(BF16)
    ki_ref[...] = _rope(_dot(ub, w_ref[:, _C_KI[0]:_C_KI[1]]), cos, sa, sb).astype(BF16)
    kv = _dot(ub, w_ref[:, _C_KV[0]:_C_KV[1]])
    kv_ref[...] = jnp.concatenate([_rope(kv[:, :LANES], cos, sa, sb), kv[:, LANES:]], axis=1).astype(BF16)
    gt_ref[...] = jax.nn.sigmoid(_dot(ub, w_ref[:, _C_G[0]:_C_G[1]])).astype(BF16)
    wit_ref[...] = _dot_t(wwi_ref[...], ub)
    ckvt_ref[...] = _dot_t(wckv_ref[...], ub).astype(BF16)


def _proj(x2, g, tabs, w, wabs, wwi, wckv, tile, tab_tiles):
    n = x2.shape[0]
    row = lambda width: pl.BlockSpec((tile, width), lambda i: (i, 0))
    const = lambda shape: pl.BlockSpec(shape, lambda i: (0, 0))
    tab = pl.BlockSpec((tile, LANES), lambda i: (i % tab_tiles, 0))
    out_shape = (
        jax.ShapeDtypeStruct((n, HEADS_W), BF16),
        jax.ShapeDtypeStruct((n, 2 * HEADS_W), BF16),
        jax.ShapeDtypeStruct((n, N_HEADS * QF_W), BF16),
        jax.ShapeDtypeStruct((n, HEADS_W), BF16),
        jax.ShapeDtypeStruct((n, LANES), BF16),
        jax.ShapeDtypeStruct((n, QF_W), BF16),
        jax.ShapeDtypeStruct((n, 2 * D_MODEL), BF16),
        jax.ShapeDtypeStruct((16, n), F32),
        jax.ShapeDtypeStruct((KV_LATENT, n), BF16),
    )
    out_specs = (row(HEADS_W), row(2 * HEADS_W), row(N_HEADS * QF_W), row(HEADS_W), row(LANES), row(QF_W),
                 row(2 * D_MODEL),
                 pl.BlockSpec((16, tile), lambda i: (0, i)), pl.BlockSpec((KV_LATENT, tile), lambda i: (0, i)))
    return pl.pallas_call(
        _proj_kernel,
        grid=(n // tile,),
        in_specs=[row(D_MODEL), const((1, D_MODEL)), tab, tab, tab, const(w.shape), const(wabs.shape),
                  const(wwi.shape), const(wckv.shape)],
        out_specs=out_specs,
        out_shape=out_shape,
        compiler_params=pltpu.CompilerParams(dimension_semantics=("arbitrary",), vmem_limit_bytes=VMEM_LIMIT),
        name="proj",
    )(x2, g, *tabs, w, wabs, wwi, wckv)


def _sb_kernel(q_ref, k_ref, v_ref, su_ref, o_ref, acc_ref, carry_ref):
    ib = pl.program_id(1) + 1
    lane = lax.broadcasted_iota(jnp.int32, (LANES, LANES), 1)
    row = lax.broadcasted_iota(jnp.int32, (LANES, LANES), 0)
    low = lane < HEAD_DIM
    high = jnp.logical_not(low)
    su = su_ref[...]
    qpos = ib * LANES + row
    acc_ref[...] = jnp.zeros_like(acc_ref)
    carry_ref[...] = jnp.zeros_like(carry_ref)

    def body(st):
        j, _ = st
        rows = pl.ds(pl.multiple_of(j * LANES, LANES), LANES)
        kpos = j * LANES + lane
        mask = ((kpos < qpos) & ((kpos < N_META) | (kpos >= LANES)))[None]
        zs = []
        for p in range(N_HEADS // 2):
            cols = slice(p * LANES, (p + 1) * LANES)
            q2, kb = q_ref[:, cols], k_ref[rows, cols]
            zero = jnp.zeros_like(q2)
            for keep in (low, high):
                zs.append(_dot_t(jnp.where(keep, q2, zero), kb))
        z = jnp.stack(zs)
        ls_pos = jnp.minimum(z, 0.0) - jnp.log(1.0 + jnp.exp(-jnp.abs(z)))
        log_keep = jnp.where(mask, ls_pos - z, 0.0)
        hi = log_keep.astype(BF16)
        lo = (log_keep - hi.astype(F32)).astype(BF16)
        flat = (N_HEADS * LANES, LANES)
        cs = _dot(hi.reshape(flat), su) + _dot(lo.reshape(flat), su)
        carry = carry_ref[...]
        a = jnp.where(mask, jnp.exp(ls_pos + cs[:, :LANES].reshape(z.shape) + carry), 0.0).astype(BF16)
        carry = carry + cs[:, LANES:].reshape(z.shape)
        carry_ref[...] = carry
        for p in range(N_HEADS // 2):
            vb = v_ref[rows, p * LANES:(p + 1) * LANES]
            zero = jnp.zeros_like(vb)
            acc_ref[p] += _dot(a[2 * p], jnp.where(low, vb, zero)) + _dot(a[2 * p + 1], jnp.where(high, vb, zero))
        return j - 1, jnp.max(carry)

    def cond(st):
        return (st[0] >= 0) & (st[1] > EXP_UNDERFLOW)

    lax.while_loop(cond, body, (ib, jnp.float32(0.0)))
    for p in range(N_HEADS // 2):
        o_ref[:, p * LANES:(p + 1) * LANES] = acc_ref[p].astype(BF16)


def _sb(qkv_x, kvp, su, batch, seq):
    nq = seq // LANES
    tp = seq + LANES
    return pl.pallas_call(
        _sb_kernel,
        grid=(batch, nq),
        in_specs=[pl.BlockSpec((LANES, HEADS_W), lambda b, i: (b * nq + i, 0)),
                  pl.BlockSpec((tp, HEADS_W), lambda b, i: (b, 0)),
                  pl.BlockSpec((tp, HEADS_W), lambda b, i: (b, 1)),
                  pl.BlockSpec(su.shape, lambda b, i: (0, 0))],
        out_specs=pl.BlockSpec((LANES, HEADS_W), lambda b, i: (b * nq + i, 0)),
        out_shape=jax.ShapeDtypeStruct((batch * seq, HEADS_W), BF16),
        scratch_shapes=[pltpu.VMEM((N_HEADS // 2, LANES, LANES), F32), pltpu.VMEM((N_HEADS, LANES, LANES), F32)],
        compiler_params=pltpu.CompilerParams(dimension_semantics=("arbitrary",) * 2, vmem_limit_bytes=VMEM_LIMIT),
        name="sb",
    )(qkv_x, kvp, kvp, su)


def _dsa_kernel(qi_ref, wit_ref, qf_ref, ki_ref, kv_ref, ckvt_ref, lt_ref, wuvt_ref, o_ref,
                sk_ref, acc_ref, m_ref, kvn_ref, qim_ref, qfp_ref, ybt_ref, *, ktop):
    ib = pl.program_id(1) + 1
    nt = (ib + DSA_SUB) // DSA_SUB
    lane = lax.broadcasted_iota(jnp.int32, (LANES, LANES), 1)
    klane = lax.broadcasted_iota(jnp.int32, (DSA_KEYS, LANES), 1)
    krow = lax.broadcasted_iota(jnp.int32, (DSA_KEYS, LANES), 0)
    low = lane < HEAD_DIM
    pairs = range(N_HEADS // 2)
    for p in pairs:
        blk = qi_ref[:, p * LANES:(p + 1) * LANES]
        zero = jnp.zeros_like(blk)
        qim_ref[2 * p * LANES:(2 * p + 1) * LANES, :] = jnp.where(low, blk, zero)
        qim_ref[(2 * p + 1) * LANES:(2 * p + 2) * LANES, :] = jnp.where(low, zero, blk)
        qfp_ref[2 * p * LANES:(2 * p + 1) * LANES, :] = qf_ref[:, 2 * p * QF_W:(2 * p + 1) * QF_W]
        qfp_ref[(2 * p + 1) * LANES:(2 * p + 2) * LANES, :] = qf_ref[:, (2 * p + 1) * QF_W:(2 * p + 2) * QF_W]
    wit = wit_ref[...]
    wit_all = jnp.concatenate([wit[h:h + 1, :] for h in range(N_HEADS)], axis=1)
    cq = 1 + ((ib * LANES + klane - LANES) >> 6)

    def tile(t):
        return pl.ds(pl.multiple_of(t * DSA_KEYS, DSA_KEYS), DSA_KEYS)

    def scores(t, c):
        kb = ki_ref[tile(t), :]
        part = wit_all * jnp.maximum(_dot_t(kb, qim_ref[...]), 0.0)
        s = part[:, :LANES]
        for h in range(1, N_HEADS):
            s = s + part[:, h * LANES:(h + 1) * LANES]
        s = jnp.where(s == 0.0, 0.0, s)
        bits = lax.bitcast_convert_type(s, jnp.int32)
        key = bits ^ ((bits >> 31) & 0x7FFFFFFF)
        rk = t * DSA_KEYS + krow
        ck = jnp.where(rk < LANES, 0, 1 + ((rk - LANES) >> 6))
        adm = ((rk < N_META) | (rk >= LANES)) & (ck <= cq)
        sk_ref[tile(t), :] = jnp.where(adm, key, INT_MIN)
        return c

    lax.fori_loop(0, nt, scores, 0)

    def count(pred):
        def step(t, acc):
            start = pl.multiple_of(t * CNT_KEYS, CNT_KEYS)
            hit = pred(sk_ref[pl.ds(start, CNT_KEYS), :]).astype(jnp.int32)
            for u in range(CNT_SUB):
                acc = acc + hit[u * LANES:(u + 1) * LANES]
            return acc
        acc = lax.fori_loop(0, (ib + CNT_SUB) // CNT_SUB, step, jnp.zeros((LANES, LANES), jnp.int32))
        return jnp.sum(acc, axis=0, keepdims=True)

    zero_row = jnp.zeros((1, LANES), jnp.int32)
    thr = jnp.where(count(lambda k: k >= zero_row) >= ktop, zero_row, jnp.full((1, LANES), INT_MIN, jnp.int32))

    def bit_step(it, thr):
        cand = thr | jnp.left_shift(jnp.int32(1), 30 - it)
        return jnp.where(count(lambda k: k >= cand) >= ktop, cand, thr)

    thr = lax.fori_loop(0, 31, bit_step, thr)
    excess = jnp.where(thr == INT_MIN, 0, count(lambda k: k >= thr) - ktop)

    @pl.when(pl.program_id(1) == 0)
    def _():
        def widest(t, m):
            kvf = kv_ref[tile(t), :].astype(F32)
            return jnp.maximum(m, jnp.max(jnp.sum(kvf * kvf, axis=1, keepdims=True)))
        kvn_ref[...] = jnp.full(kvn_ref.shape, lax.fori_loop(0, kv_ref.shape[0] // DSA_KEYS, widest, jnp.float32(0.0)))

    ones = jnp.ones((8, QF_W), BF16)
    bounds = []
    for p in pairs:
        qp = qfp_ref[2 * p * LANES:(2 * p + 2) * LANES, :].astype(F32)
        qn2 = _dot_t(ones, (qp * qp).astype(BF16))[0:1, :]
        kvn = kvn_ref[0:1, :]
        bounds.append(jnp.sqrt(qn2 * jnp.concatenate([kvn, kvn], axis=1)) * 1.05 + 1e-6)
    bound_max = functools.reduce(jnp.maximum, bounds)
    bound_all = jnp.concatenate(bounds, axis=1)
    fast = (jnp.max(excess) == 0) & (jnp.max(bound_max) <= SAFE_LOGIT_BOUND)
    acc_ref[...] = jnp.zeros_like(acc_ref)

    def twice(a):
        return jnp.concatenate([a, a], axis=1)

    def atile(t):
        return pl.ds(pl.multiple_of(t * ATT_KEYS, ATT_KEYS), ATT_KEYS)

    nta = (ib + ATT_SUB) // ATT_SUB

    @pl.when(fast)
    def _():
        def attend(t, c):
            skt = sk_ref[atile(t), :]
            sel = jnp.where((skt >= thr) & (skt > INT_MIN), 1.0, 0.0).astype(BF16)
            logits = _dot_t(kv_ref[atile(t), :], qfp_ref[...])
            prob = jnp.exp(logits - bound_all).astype(BF16) * jnp.concatenate([sel] * N_HEADS, axis=1)
            acc_ref[...] += _dot(ckvt_ref[t], prob)
            return c

        lax.fori_loop(0, nta, attend, 0)

    @pl.when(jnp.logical_not(fast))
    def _():
        need = (ktop - count(lambda k: k > thr)).astype(F32)
        m_ref[...] = jnp.full_like(m_ref, -1e29)
        lt = lt_ref[...]

        def attend(t, tie_carry):
            skt = sk_ref[atile(t), :]
            tie = (skt == thr) & (skt > INT_MIN)
            rank = _dot(lt, jnp.where(tie, 1.0, 0.0).astype(BF16)) + tie_carry
            sel = twice((skt > thr) | (tie & (rank <= need)))
            kvb = kv_ref[atile(t), :]
            cb = ckvt_ref[t]
            for p in pairs:
                pc = slice(2 * p * LANES, (2 * p + 2) * LANES)
                s = jnp.where(sel, _dot_t(kvb, qfp_ref[pc, :]), -1e30)
                m_old = m_ref[p:p + 1, :]
                m_new = jnp.maximum(m_old, jnp.max(s, axis=0, keepdims=True))
                prob = jnp.exp(s - m_new).astype(BF16)
                acc_ref[:, pc] = acc_ref[:, pc] * jnp.exp(m_old - m_new) + _dot(cb, prob)
                m_ref[p:p + 1, :] = m_new
            return rank[ATT_KEYS - 1:ATT_KEYS, :]

        lax.fori_loop(0, nta, attend, jnp.zeros((1, LANES), F32))

    for h in range(N_HEADS):
        cols = slice(h * LANES, (h + 1) * LANES)
        o = (acc_ref[:KV_LATENT, cols] / acc_ref[KV_LATENT:KV_LATENT + 1, cols]).astype(BF16)
        ybt_ref[h * HEAD_DIM:(h + 1) * HEAD_DIM, :] = _dot(wuvt_ref[h], o)
    o_ref[...] = ybt_ref[...].T.astype(BF16)


def _dsa(qi, wit, qf, kip, kvp, ckvt3, lt, wuvt, batch, seq, ktop):
    nq = seq // LANES
    tpk = kip.shape[0] // batch
    ntk = tpk // ATT_KEYS
    return pl.pallas_call(
        functools.partial(_dsa_kernel, ktop=ktop),
        grid=(batch, nq),
        in_specs=[pl.BlockSpec((LANES, HEADS_W), lambda b, i: (b * nq + i, 0)),
                  pl.BlockSpec((16, LANES), lambda b, i: (0, b * nq + i)),
                  pl.BlockSpec((LANES, N_HEADS * QF_W), lambda b, i: (b * nq + i, 0)),
                  pl.BlockSpec((tpk, LANES), lambda b, i: (b, 0)),
                  pl.BlockSpec((tpk, QF_W), lambda b, i: (b, 0)),
                  pl.BlockSpec((ntk, VAL_ROWS, ATT_KEYS), lambda b, i: (b, 0, 0)),
                  pl.BlockSpec(lt.shape, lambda b, i: (0, 0)),
                  pl.BlockSpec(wuvt.shape, lambda b, i: (0, 0, 0))],
        out_specs=pl.BlockSpec((LANES, HEADS_W), lambda b, i: (b * nq + i, 0)),
        out_shape=jax.ShapeDtypeStruct((batch * seq, HEADS_W), BF16),
        scratch_shapes=[pltpu.VMEM((tpk, LANES), jnp.int32),
                        pltpu.VMEM((VAL_ROWS, N_HEADS * LANES), F32),
                        pltpu.VMEM((8, 2 * LANES), F32),
                        pltpu.VMEM((8, LANES), F32),
                        pltpu.VMEM((N_HEADS * LANES, LANES), BF16),
                        pltpu.VMEM((N_HEADS * LANES, QF_W), BF16),
                        pltpu.VMEM((HEADS_W, LANES), F32)],
        compiler_params=pltpu.CompilerParams(dimension_semantics=("arbitrary",) * 2, vmem_limit_bytes=VMEM_LIMIT),
        name="dsa",
    )(qi, wit, qf, kip, kvp, ckvt3, lt, wuvt)


def _merge_kernel(x_ref, ya_ref, yb_ref, gt_ref, wua_ref, wub_ref, wo_ref, g_ref, wrt_ref, brt_ref, slt_ref,
                  h1_ref, u2_ref, r_ref, cnt_ref, carry_ref):
    @pl.when(pl.program_id(0) == 0)
    def _():
        carry_ref[...] = jnp.zeros_like(carry_ref)

    gt = gt_ref[...].astype(F32)
    z = gt[:, :D_MODEL] * _dot(ya_ref[...], wua_ref[...]) + gt[:, D_MODEL:] * _dot(yb_ref[...], wub_ref[...])
    h1 = x_ref[...] + _dot(z.astype(BF16), wo_ref[...])
    h1_ref[...] = h1
    u2 = _rmsnorm(h1, g_ref[...])
    u2_ref[...] = u2
    u_hi = u2.astype(BF16)
    u_lo = (u2 - u_hi.astype(F32)).astype(BF16)
    hi_terms = _dot(u_hi, wrt_ref[...])
    lg = hi_terms[:, :LANES] + hi_terms[:, LANES:] + _dot(u_lo, wrt_ref[:, :LANES]) + brt_ref[...]

    tm = lg.shape[0]
    lane = lax.broadcasted_iota(jnp.int32, (tm, LANES), 1)
    big = jnp.int32(LANES)

    def softmax(mask):
        x = jnp.where(mask, lg, -jnp.inf)
        e = jnp.where(mask, jnp.exp(x - jnp.max(x, axis=-1, keepdims=True)), 0.0)
        return e / jnp.sum(e, axis=-1, keepdims=True)

    def top1(p, mask):
        best = jnp.max(jnp.where(mask, p, -1.0), axis=-1, keepdims=True)
        idx = jnp.min(jnp.where(mask & (p == best), lane, big), axis=-1, keepdims=True)
        return best, idx

    gmask = (lane >= N_EXPERTS) & (lane < N_EXPERTS + N_GROUPS)
    p_grp, g_lane = top1(softmax(gmask), gmask)
    first = (g_lane - N_EXPERTS) * EXP_PER_GROUP
    emask = (lane >= first) & (lane < first + EXP_PER_GROUP)
    ep = softmax(emask)
    p0, e0 = top1(ep, emask)
    rest = emask & (lane != e0)
    p1, e1 = top1(ep, rest)
    den = p0 + p1
    w0 = p_grp * p0 / den
    w1 = p_grp * p1 / den

    oh0 = lane == e0
    oh1 = lane == e1
    both = (oh0 | oh1).astype(BF16)
    before = _dot(slt_ref[...], both) + carry_ref[...]
    rank0 = jnp.sum(jnp.where(oh0, before, 0.0), axis=-1, keepdims=True)
    rank1 = jnp.sum(jnp.where(oh1, before, 0.0), axis=-1, keepdims=True)
    carry = carry_ref[...] + jnp.sum(both.astype(F32), axis=0, keepdims=True)
    carry_ref[...] = carry
    cnt_ref[...] = carry

    cols = (e0.astype(F32), e1.astype(F32), w0, w1, rank0, rank1)
    r = jnp.zeros((tm, LANES), F32)
    for c, v in enumerate(cols):
        r = jnp.where(lane == c, v, r)
    r_ref[...] = r


def _merge(x2, ya, yb, gt, wua, wub, wo, g, wrt, brt, slt):
    n = x2.shape[0]
    tm = MERGE_TILE
    row = lambda width: pl.BlockSpec((tm, width), lambda i: (i, 0))
    const = lambda a: pl.BlockSpec(a.shape, lambda i: (0,) * a.ndim)
    return pl.pallas_call(
        _merge_kernel,
        grid=(n // tm,),
        in_specs=[row(D_MODEL), row(HEADS_W), row(HEADS_W), row(2 * D_MODEL), const(wua), const(wub), const(wo),
                  const(g), const(wrt), const(brt), const(slt)],
        out_specs=(row(D_MODEL), row(D_MODEL), row(LANES), pl.BlockSpec((1, LANES), lambda i: (0, 0))),
        out_shape=(jax.ShapeDtypeStruct((n, D_MODEL), F32), jax.ShapeDtypeStruct((n, D_MODEL), F32),
                   jax.ShapeDtypeStruct((n, LANES), F32), jax.ShapeDtypeStruct((1, LANES), F32)),
        scratch_shapes=[pltpu.VMEM((1, LANES), F32)],
        compiler_params=pltpu.CompilerParams(dimension_semantics=("arbitrary",), vmem_limit_bytes=VMEM_LIMIT),
        name="merge",
    )(x2, ya, yb, gt, wua, wub, wo, g, wrt, brt, slt)


def _row_copy(src, src_row, dst, dst_row, sem):
    return pltpu.make_async_copy(src.at[pl.ds(src_row, 1)], dst.at[pl.ds(dst_row, 1)], sem)


def _dispatch_kernel(dest_ref, u_ref, xs_in_ref, xs_ref, sem):
    del xs_in_ref

    def issue(t, c):
        for s in range(2):
            _row_copy(u_ref, t, xs_ref, dest_ref[0, 0, 2 * t + s], sem).start(priority=s)
        return c

    def drain(t, c):
        for s in range(2):
            _row_copy(u_ref, t, xs_ref, dest_ref[0, 0, 2 * t + s], sem).wait()
        return c

    lax.fori_loop(0, DISPATCH_TILE, issue, 0, unroll=ROW_DMA_UNROLL)
    lax.fori_loop(0, DISPATCH_TILE, drain, 0, unroll=ROW_DMA_UNROLL)


def _dispatch(dest3, u2, xs0):
    n = u2.shape[0]
    return pl.pallas_call(
        _dispatch_kernel,
        grid=(n // DISPATCH_TILE,),
        in_specs=[pl.BlockSpec((1, 1, 2 * DISPATCH_TILE), lambda i: (i, 0, 0), memory_space=pltpu.SMEM),
                  pl.BlockSpec((DISPATCH_TILE, D_MODEL), lambda i: (i, 0)), pl.BlockSpec(memory_space=pl.ANY)],
        out_specs=pl.BlockSpec(memory_space=pl.ANY),
        out_shape=jax.ShapeDtypeStruct(xs0.shape, xs0.dtype),
        scratch_shapes=[pltpu.SemaphoreType.DMA],
        input_output_aliases={2: 0},
        compiler_params=pltpu.CompilerParams(dimension_semantics=("arbitrary",), has_side_effects=True),
        name="dispatch",
    )(dest3, u2, xs0)


def _ffn_kernel(te_ref, nu_ref, xs_ref, w1_ref, w3_ref, w2_ref, ys_ref, w1b_ref, w3b_ref, w2b_ref):
    t = pl.program_id(0)

    @pl.when((t == 0) | (te_ref[t] != te_ref[jnp.maximum(t - 1, 0)]))
    def _():
        w1b_ref[...] = w1_ref[0].astype(BF16)
        w3b_ref[...] = w3_ref[0].astype(BF16)
        w2b_ref[...] = w2_ref[0].astype(BF16)

    @pl.when(t < nu_ref[0])
    def _():
        x = xs_ref[...].astype(BF16)
        hid = jax.nn.silu(_dot(x, w1b_ref[...])) * _dot(x, w3b_ref[...])
        ys_ref[...] = _dot(hid.astype(BF16), w2b_ref[...])

    @pl.when(t >= nu_ref[0])
    def _():
        ys_ref[...] = jnp.zeros_like(ys_ref)


def _ffn(tile_expert, n_used, xs, w1, w3, w2):
    p = xs.shape[0]
    grid_spec = pltpu.PrefetchScalarGridSpec(
        num_scalar_prefetch=2,
        grid=(p // FFN_TILE,),
        in_specs=[pl.BlockSpec((FFN_TILE, D_MODEL), lambda t, te, nu: (t, 0)),
                  pl.BlockSpec((1, D_MODEL, D_EXPERT), lambda t, te, nu: (te[t], 0, 0)),
                  pl.BlockSpec((1, D_MODEL, D_EXPERT), lambda t, te, nu: (te[t], 0, 0)),
                  pl.BlockSpec((1, D_EXPERT, D_MODEL), lambda t, te, nu: (te[t], 0, 0))],
        out_specs=pl.BlockSpec((FFN_TILE, D_MODEL), lambda t, te, nu: (t, 0)),
        scratch_shapes=[pltpu.VMEM((D_MODEL, D_EXPERT), BF16), pltpu.VMEM((D_MODEL, D_EXPERT), BF16),
                        pltpu.VMEM((D_EXPERT, D_MODEL), BF16)],
    )
    return pl.pallas_call(
        _ffn_kernel,
        grid_spec=grid_spec,
        out_shape=jax.ShapeDtypeStruct((p, D_MODEL), F32),
        compiler_params=pltpu.CompilerParams(dimension_semantics=("arbitrary",), vmem_limit_bytes=VMEM_LIMIT),
        name="ffn",
    )(tile_expert, n_used, xs, w1, w3, w2)


def _final_kernel(dest_ref, h1_ref, r_ref, g_ref, ys_ref, o_ref, buf_ref, sem):
    def issue(t, c):
        for s in range(2):
            _row_copy(ys_ref, dest_ref[0, 0, 2 * t + s], buf_ref.at[s], t, sem).start(priority=s)
        return c

    def drain(t, c):
        for s in range(2):
            _row_copy(ys_ref, dest_ref[0, 0, 2 * t + s], buf_ref.at[s], t, sem).wait()
        return c

    lax.fori_loop(0, FINAL_TILE, issue, 0, unroll=ROW_DMA_UNROLL)
    lax.fori_loop(0, FINAL_TILE, drain, 0, unroll=ROW_DMA_UNROLL)
    r = r_ref[...]
    moe = r[:, 2:3] * buf_ref[0] + r[:, 3:4] * buf_ref[1]
    o_ref[...] = _rmsnorm(h1_ref[...] + moe, g_ref[...])


def _final(dest3, h1, r, g, ys):
    n = h1.shape[0]
    tm = FINAL_TILE
    return pl.pallas_call(
        _final_kernel,
        grid=(n // tm,),
        in_specs=[pl.BlockSpec((1, 1, 2 * tm), lambda i: (i, 0, 0), memory_space=pltpu.SMEM),
                  pl.BlockSpec((tm, D_MODEL), lambda i: (i, 0)),
                  pl.BlockSpec((tm, LANES), lambda i: (i, 0)),
                  pl.BlockSpec((1, D_MODEL), lambda i: (0, 0)),
                  pl.BlockSpec(memory_space=pl.ANY)],
        out_specs=pl.BlockSpec((tm, D_MODEL), lambda i: (i, 0)),
        out_shape=jax.ShapeDtypeStruct((n, D_MODEL), F32),
        scratch_shapes=[pltpu.VMEM((2, tm, D_MODEL), F32), pltpu.SemaphoreType.DMA],
        compiler_params=pltpu.CompilerParams(dimension_semantics=("arbitrary",), vmem_limit_bytes=VMEM_LIMIT),
        name="final",
    )(dest3, h1, r, g, ys)


def _rope_tables(pos):
    half = ROPE_DIM // 2
    inv = np.float32(ROPE_THETA) ** (-np.arange(half, dtype=np.float32) / np.float32(half))
    ang = np.asarray(pos, np.float32)[:, None] * inv[None, :]
    cos, sin = np.cos(ang).astype(np.float32), np.sin(ang).astype(np.float32)
    d = np.arange(LANES) % HEAD_DIM
    f = d % half
    cos_t = np.where(d < ROPE_DIM, cos[:, f], np.float32(1))
    sa = np.where((d >= half) & (d < ROPE_DIM), sin[:, f], np.float32(0))
    sb = np.where(d < half, -sin[:, f], np.float32(0))
    return tuple(jnp.asarray(t, F32) for t in (cos_t, sa, sb))


def _proj_weights(w_in, w_uk):
    qa, ka, va, qb, ckv, krope, qi, ki, wi, ga, gb = jnp.split(w_in, list(np.cumsum(SPLITS)[:-1]), axis=-1)
    zeros = lambda width: jnp.zeros((D_MODEL, width), w_in.dtype)
    kv = jnp.concatenate([krope, zeros(LANES - ROPE_DIM), ckv], axis=1)
    w = jnp.concatenate([qa * ATTN_SCALE, ka, va, qb, qi, ki, ki, kv, ga, gb], axis=1).astype(BF16)
    assert w.shape[1] == PROJ_W
    nope = HEAD_DIM - ROPE_DIM
    per_head = jnp.concatenate([
        jnp.concatenate([jnp.eye(ROPE_DIM, dtype=F32)[None].repeat(N_HEADS, 0),
                         jnp.zeros((N_HEADS, ROPE_DIM, QF_W - ROPE_DIM), F32)], axis=2),
        jnp.concatenate([jnp.zeros((N_HEADS, nope, LANES), F32), jnp.swapaxes(w_uk, 1, 2)], axis=2)], axis=1)
    head_eye = jnp.eye(N_HEADS, dtype=F32)
    wabs = (per_head[:, :, None, :] * (head_eye * ATTN_SCALE)[:, None, :, None])
    wabs = wabs.reshape(HEADS_W, N_HEADS * QF_W).astype(BF16)
    wwi = jnp.concatenate([wi.T, jnp.zeros((16 - N_HEADS, D_MODEL), w_in.dtype)], axis=0).astype(BF16)
    wckv = ckv.T.astype(BF16)
    return w, wabs, wwi, wckv


def _tri(n, strict_lower):
    r = np.arange(n)
    m = (r[None, :] < r[:, None]) if strict_lower else (r[None, :] <= r[:, None])
    return jnp.asarray(m, BF16)


def kernel(x, meta_tokens, norm_mix_g, w_in, w_uk, w_uv, w_up_a, w_up_b, w_o, norm_ffn_g, w_group, b_group,
           w_router, b_router, w1, w3, w2, norm_final_g):
    batch, seq, d = x.shape
    assert d == D_MODEL and seq % LANES == 0 and norm_mix_g.shape[0] == 1
    n = batch * seq
    nq = seq // LANES
    tp = seq + LANES
    ktop = min(TOPK_MAX, seq // 4)
    x2 = x.reshape(n, d)

    w, wabs, wwi, wckv = _proj_weights(w_in[0], w_uk[0])
    g_mix = norm_mix_g[0][None, :]
    proj_tile = 1024 if seq % 1024 == 0 else LANES
    px = _proj(x2, g_mix, _rope_tables(N_META + np.arange(seq)), w, wabs, wwi, wckv, proj_tile, seq // proj_tile)
    meta = jnp.concatenate([meta_tokens.astype(x.dtype), jnp.zeros((LANES - N_META, d), x.dtype)], axis=0)
    pm = _proj(meta, g_mix, _rope_tables(np.arange(LANES)), w, wabs, wwi, wckv, LANES, 1)
    qa_x, kva_x, qf_x, qi_x, ki_x, kv_x, gt_x, wit_x, ckvt_x = px
    _, kva_m, _, _, ki_m, kv_m, _, _, ckvt_m = pm

    def with_meta(xpart, mpart, rows):
        wdt = xpart.shape[1]
        parts = [jnp.broadcast_to(mpart[None], (batch, LANES, wdt)), xpart.reshape(batch, seq, wdt)]
        if rows > tp:
            parts.append(jnp.zeros((batch, rows - tp, wdt), xpart.dtype))
        return jnp.concatenate(parts, axis=1).reshape(batch * rows, wdt)

    kv_a = with_meta(kva_x, kva_m, tp)
    su = jnp.concatenate([_tri(LANES, True), jnp.ones((LANES, LANES), BF16)], axis=1)
    ya = _sb(qa_x, kv_a, su, batch, seq)

    tpk = -(-tp // DSA_KEYS) * DSA_KEYS
    kip = with_meta(ki_x, ki_m, tpk)
    kvp = with_meta(kv_x, kv_m, tpk)
    ckvt = jnp.concatenate([jnp.broadcast_to(ckvt_m[:, None, :], (KV_LATENT, batch, LANES)),
                            ckvt_x.reshape(KV_LATENT, batch, seq),
                            jnp.zeros((KV_LATENT, batch, tpk - tp), BF16)], axis=2)
    ones_row = jnp.concatenate([jnp.ones((1, batch, tpk), BF16),
                                jnp.zeros((VAL_ROWS - KV_LATENT - 1, batch, tpk), BF16)], axis=0)
    ckvt = jnp.concatenate([ckvt, ones_row], axis=0)
    ckvt3 = ckvt.reshape(VAL_ROWS, batch * tpk // ATT_KEYS, ATT_KEYS).transpose(1, 0, 2)
    wuvt = jnp.swapaxes(w_uv[0], 1, 2).astype(BF16)
    yb = _dsa(qi_x, wit_x, qf_x, kip, kvp, ckvt3, _tri(ATT_KEYS, False), wuvt, batch, seq, ktop)

    wrt = jnp.concatenate([w_router[0], w_group[0],
                           jnp.zeros((d, LANES - N_EXPERTS - N_GROUPS), F32)], axis=1).astype(F32)
    wrt_hi = wrt.astype(BF16)
    wrt = jnp.concatenate([wrt_hi, (wrt - wrt_hi.astype(F32)).astype(BF16)], axis=1)
    brt = jnp.concatenate([b_router[0], b_group[0], jnp.zeros((LANES - N_EXPERTS - N_GROUPS,), F32)])[None, :]
    h1, u2, r, counts = _merge(x2, ya, yb, gt_x, w_up_a[0].astype(BF16), w_up_b[0].astype(BF16),
                               w_o[0].astype(BF16), norm_ffn_g[0][None, :], wrt, brt.astype(F32),
                               _tri(MERGE_TILE, True))

    cnt = counts[0, :N_EXPERTS].astype(jnp.int32)
    padded = (cnt + FFN_TILE - 1) // FFN_TILE * FFN_TILE
    p_end = jnp.cumsum(padded)
    p_start = p_end - padded
    e_sel = r[:, 0:2].astype(jnp.int32)
    first_slot = jnp.sum(jnp.where(e_sel[..., None] == jnp.arange(N_EXPERTS), p_start, 0), axis=-1)
    dest = first_slot + r[:, 4:6].astype(jnp.int32)
    n_tiles = (2 * n) // FFN_TILE + N_EXPERTS
    tile_row = jnp.arange(n_tiles, dtype=jnp.int32) * FFN_TILE
    tile_expert = jnp.minimum(jnp.sum((p_end[None, :] <= tile_row[:, None]).astype(jnp.int32), axis=1),
                              N_EXPERTS - 1)
    n_used = (p_end[-1] // FFN_TILE).astype(jnp.int32)
    tile_expert = jnp.where(jnp.arange(n_tiles) < n_used, tile_expert, tile_expert[jnp.maximum(n_used - 1, 0)])

    xs0 = jnp.zeros((n_tiles * FFN_TILE, d), F32)
    xs = _dispatch(dest.reshape(n // DISPATCH_TILE, 1, 2 * DISPATCH_TILE), u2, xs0)
    ys = _ffn(tile_expert, n_used[None], xs, w1[0], w3[0], w2[0])
    out = _final(dest.reshape(n // FINAL_TILE, 1, 2 * FINAL_TILE), h1, r, norm_final_g[None, :], ys)
    return out.reshape(batch, seq, d)
```

```python
import functools
import math

import numpy as np
import jax
import jax.numpy as jnp
from jax import lax
from jax.experimental import pallas as pl
from jax.experimental.pallas import tpu as pltpu

D_MODEL = 1024
CHUNK = 64
N_META = 16
HEAD_DIM = 64
ROPE_DIM = 16
ROPE_THETA = 500000.0
N_HEADS = 8
KV_LATENT = 128
IDX_DIM = 64
TOPK_MAX = 256
N_GROUPS = 4
EXP_PER_GROUP = 8
N_EXPERTS = 32
D_EXPERT = 512
NORM_EPS = 1e-6
HEADS_W = N_HEADS * HEAD_DIM
SPLITS = (HEADS_W, HEADS_W, HEADS_W, HEADS_W, KV_LATENT, ROPE_DIM, N_HEADS * IDX_DIM, IDX_DIM, N_HEADS,
          D_MODEL, D_MODEL)

LANES = 128
QF_W = 2 * LANES
INT_MIN = -2 ** 31
EXP_UNDERFLOW = -104.0
ATTN_SCALE = 1.0 / math.sqrt(HEAD_DIM)
VMEM_LIMIT = 56 * 1024 * 1024
DSA_SUB = 8
DSA_KEYS = DSA_SUB * LANES
ATT_SUB = 4
ATT_KEYS = ATT_SUB * LANES
CNT_SUB = 2
CNT_KEYS = CNT_SUB * LANES
VAL_ROWS = KV_LATENT + 8
SAFE_LOGIT_BOUND = 40.0
FFN_TILE = 512
MERGE_TILE = 1024
FINAL_TILE = 1024
DISPATCH_TILE = 1024
ROW_DMA_UNROLL = 32

BF16 = jnp.bfloat16
F32 = jnp.float32

_C_QKV = (0, 3 * HEADS_W)
_C_QB = (_C_QKV[1], _C_QKV[1] + HEADS_W)
_C_QI = (_C_QB[1], _C_QB[1] + HEADS_W)
_C_KI = (_C_QI[1], _C_QI[1] + LANES)
_C_KV = (_C_KI[1], _C_KI[1] + QF_W)
_C_G = (_C_KV[1], _C_KV[1] + 2 * D_MODEL)
PROJ_W = _C_G[1]


def _dot(a, b):
    return jnp.dot(a, b, preferred_element_type=F32)


def _dot_t(a, b):
    return lax.dot_general(a, b, (((1,), (1,)), ((), ())), preferred_element_type=F32)


def _rmsnorm(x, g):
    return x * lax.rsqrt(jnp.mean(x * x, axis=-1, keepdims=True) + NORM_EPS) * g


def _rope(x, cos, sa, sb):
    w = x.shape[1]
    n = w // LANES
    if n > 1:
        cos, sa, sb = (jnp.concatenate([t] * n, axis=1) for t in (cos, sa, sb))
    return x * cos + pltpu.roll(x, 8, 1) * sa + pltpu.roll(x, w - 8, 1) * sb


def _proj_kernel(x_ref, g_ref, cos_ref, sa_ref, sb_ref, w_ref, wabs_ref, wwi_ref, wckv_ref,
                 qa_ref, kva_ref, qf_ref, qi_ref, ki_ref, kv_ref, gt_ref, wit_ref, ckvt_ref):
    ub = _rmsnorm(x_ref[...], g_ref[...]).astype(BF16)
    cos, sa, sb = cos_ref[...], sa_ref[...], sb_ref[...]
    qa_ref[...] = _dot(ub, w_ref[:, _C_QKV[0]:_C_QKV[0] + HEADS_W]).astype(BF16)
    kva_ref[...] = _dot(ub, w_ref[:, _C_QKV[0] + HEADS_W:_C_QKV[1]]).astype(BF16)
    qb = _rope(_dot(ub, w_ref[:, _C_QB[0]:_C_QB[1]]), cos, sa, sb).astype(BF16)
    qf_ref[...] = _dot(qb, wabs_ref[...]).astype(BF16)
    qi_ref[...] = _rope(_dot(ub, w_ref[:, _C_QI[0]:_C_QI[1]]), cos, sa, sb).astype(BF16)
    ki_ref[...] = _rope(_dot(ub, w_ref[:, _C_KI[0]:_C_KI[1]]), cos, sa, sb).astype(BF16)
    kv = _dot(ub, w_ref[:, _C_KV[0]:_C_KV[1]])
    kv_ref[...] = jnp.concatenate([_rope(kv[:, :LANES], cos, sa, sb), kv[:, LANES:]], axis=1).astype(BF16)
    gt_ref[...] = jax.nn.sigmoid(_dot(ub, w_ref[:, _C_G[0]:_C_G[1]])).astype(BF16)
    wit_ref[...] = _dot_t(wwi_ref[...], ub)
    ckvt_ref[...] = _dot_t(wckv_ref[...], ub).astype(BF16)


def _proj(x2, g, tabs, w, wabs, wwi, wckv, tile, tab_tiles):
    n = x2.shape[0]
    row = lambda width: pl.BlockSpec((tile, width), lambda i: (i, 0))
    const = lambda shape: pl.BlockSpec(shape, lambda i: (0, 0))
    tab = pl.BlockSpec((tile, LANES), lambda i: (i % tab_tiles, 0))
    out_shape = (
        jax.ShapeDtypeStruct((n, HEADS_W), BF16),
        jax.ShapeDtypeStruct((n, 2 * HEADS_W), BF16),
        jax.ShapeDtypeStruct((n, N_HEADS * QF_W), BF16),
        jax.ShapeDtypeStruct((n, HEADS_W), BF16),
        jax.ShapeDtypeStruct((n, LANES), BF16),
        jax.ShapeDtypeStruct((n, QF_W), BF16),
        jax.ShapeDtypeStruct((n, 2 * D_MODEL), BF16),
        jax.ShapeDtypeStruct((16, n), F32),
        jax.ShapeDtypeStruct((KV_LATENT, n), BF16),
    )
    out_specs = (row(HEADS_W), row(2 * HEADS_W), row(N_HEADS * QF_W), row(HEADS_W), row(LANES), row(QF_W),
                 row(2 * D_MODEL),
                 pl.BlockSpec((16, tile), lambda i: (0, i)), pl.BlockSpec((KV_LATENT, tile), lambda i: (0, i)))
    return pl.pallas_call(
        _proj_kernel,
        grid=(n // tile,),
        in_specs=[row(D_MODEL), const((1, D_MODEL)), tab, tab, tab, const(w.shape), const(wabs.shape),
                  const(wwi.shape), const(wckv.shape)],
        out_specs=out_specs,
        out_shape=out_shape,
        compiler_params=pltpu.CompilerParams(dimension_semantics=("arbitrary",), vmem_limit_bytes=VMEM_LIMIT),
        name="proj",
    )(x2, g, *tabs, w, wabs, wwi, wckv)


def _sb_kernel(q_ref, k_ref, v_ref, su_ref, o_ref, acc_ref, carry_ref):
    ib = pl.program_id(1) + 1
    lane = lax.broadcasted_iota(jnp.int32, (LANES, LANES), 1)
    row = lax.broadcasted_iota(jnp.int32, (LANES, LANES), 0)
    low = lane < HEAD_DIM
    high = jnp.logical_not(low)
    su = su_ref[...]
    qpos = ib * LANES + row
    acc_ref[...] = jnp.zeros_like(acc_ref)
    carry_ref[...] = jnp.zeros_like(carry_ref)

    def body(st):
        j, _ = st
        rows = pl.ds(pl.multiple_of(j * LANES, LANES), LANES)
        kpos = j * LANES + lane
        mask = ((kpos < qpos) & ((kpos < N_META) | (kpos >= LANES)))[None]
        zs = []
        for p in range(N_HEADS // 2):
            cols = slice(p * LANES, (p + 1) * LANES)
            q2, kb = q_ref[:, cols], k_ref[rows, cols]
            zero = jnp.zeros_like(q2)
            for keep in (low, high):
                zs.append(_dot_t(jnp.where(keep, q2, zero), kb))
        z = jnp.stack(zs)
        ls_pos = jnp.minimum(z, 0.0) - jnp.log(1.0 + jnp.exp(-jnp.abs(z)))
        log_keep = jnp.where(mask, ls_pos - z, 0.0)
        hi = log_keep.astype(BF16)
        lo = (log_keep - hi.astype(F32)).astype(BF16)
        flat = (N_HEADS * LANES, LANES)
        cs = _dot(hi.reshape(flat), su) + _dot(lo.reshape(flat), su)
        carry = carry_ref[...]
        a = jnp.where(mask, jnp.exp(ls_pos + cs[:, :LANES].reshape(z.shape) + carry), 0.0).astype(BF16)
        carry = carry + cs[:, LANES:].reshape(z.shape)
        carry_ref[...] = carry
        for p in range(N_HEADS // 2):
            vb = v_ref[rows, p * LANES:(p + 1) * LANES]
            zero = jnp.zeros_like(vb)
            acc_ref[p] += _dot(a[2 * p], jnp.where(low, vb, zero)) + _dot(a[2 * p + 1], jnp.where(high, vb, zero))
        return j - 1, jnp.max(carry)

    def cond(st):
        return (st[0] >= 0) & (st[1] > EXP_UNDERFLOW)

    lax.while_loop(cond, body, (ib, jnp.float32(0.0)))
    for p in range(N_HEADS // 2):
        o_ref[:, p * LANES:(p + 1) * LANES] = acc_ref[p].astype(BF16)


def _sb(qkv_x, kvp, su, batch, seq):
    nq = seq // LANES
    tp = seq + LANES
    return pl.pallas_call(
        _sb_kernel,
        grid=(batch, nq),
        in_specs=[pl.BlockSpec((LANES, HEADS_W), lambda b, i: (b * nq + i, 0)),
                  pl.BlockSpec((tp, HEADS_W), lambda b, i: (b, 0)),
                  pl.BlockSpec((tp, HEADS_W), lambda b, i: (b, 1)),
                  pl.BlockSpec(su.shape, lambda b, i: (0, 0))],
        out_specs=pl.BlockSpec((LANES, HEADS_W), lambda b, i: (b * nq + i, 0)),
        out_shape=jax.ShapeDtypeStruct((batch * seq, HEADS_W), BF16),
        scratch_shapes=[pltpu.VMEM((N_HEADS // 2, LANES, LANES), F32), pltpu.VMEM((N_HEADS, LANES, LANES), F32)],
        compiler_params=pltpu.CompilerParams(dimension_semantics=("arbitrary",) * 2, vmem_limit_bytes=VMEM_LIMIT),
        name="sb",
    )(qkv_x, kvp, kvp, su)


def _dsa_kernel(qi_ref, wit_ref, qf_ref, ki_ref, kv_ref, ckvt_ref, lt_ref, wuvt_ref, o_ref,
                sk_ref, acc_ref, m_ref, kvn_ref, qim_ref, qfp_ref, ybt_ref, *, ktop):
    ib = pl.program_id(1) + 1
    nt = (ib + DSA_SUB) // DSA_SUB
    lane = lax.broadcasted_iota(jnp.int32, (LANES, LANES), 1)
    klane = lax.broadcasted_iota(jnp.int32, (DSA_KEYS, LANES), 1)
    krow = lax.broadcasted_iota(jnp.int32, (DSA_KEYS, LANES), 0)
    low = lane < HEAD_DIM
    pairs = range(N_HEADS // 2)
    for p in pairs:
        blk = qi_ref[:, p * LANES:(p + 1) * LANES]
        zero = jnp.zeros_like(blk)
        qim_ref[2 * p * LANES:(2 * p + 1) * LANES, :] = jnp.where(low, blk, zero)
        qim_ref[(2 * p + 1) * LANES:(2 * p + 2) * LANES, :] = jnp.where(low, zero, blk)
        qfp_ref[2 * p * LANES:(2 * p + 1) * LANES, :] = qf_ref[:, 2 * p * QF_W:(2 * p + 1) * QF_W]
        qfp_ref[(2 * p + 1) * LANES:(2 * p + 2) * LANES, :] = qf_ref[:, (2 * p + 1) * QF_W:(2 * p + 2) * QF_W]
    wit = wit_ref[...]
    wit_all = jnp.concatenate([wit[h:h + 1, :] for h in range(N_HEADS)], axis=1)
    cq = 1 + ((ib * LANES + klane - LANES) >> 6)

    def tile(t):
        return pl.ds(pl.multiple_of(t * DSA_KEYS, DSA_KEYS), DSA_KEYS)

    def scores(t, c):
        kb = ki_ref[tile(t), :]
        part = wit_all * jnp.maximum(_dot_t(kb, qim_ref[...]), 0.0)
        s = part[:, :LANES]
        for h in range(1, N_HEADS):
            s = s + part[:, h * LANES:(h + 1) * LANES]
        s = jnp.where(s == 0.0, 0.0, s)
        bits = lax.bitcast_convert_type(s, jnp.int32)
        key = bits ^ ((bits >> 31) & 0x7FFFFFFF)
        rk = t * DSA_KEYS + krow
        ck = jnp.where(rk < LANES, 0, 1 + ((rk - LANES) >> 6))
        adm = ((rk < N_META) | (rk >= LANES)) & (ck <= cq)
        sk_ref[tile(t), :] = jnp.where(adm, key, INT_MIN)
        return c

    lax.fori_loop(0, nt, scores, 0)

    def count(pred):
        def step(t, acc):
            start = pl.multiple_of(t * CNT_KEYS, CNT_KEYS)
            hit = pred(sk_ref[pl.ds(start, CNT_KEYS), :]).astype(jnp.int32)
            for u in range(CNT_SUB):
                acc = acc + hit[u * LANES:(u + 1) * LANES]
            return acc
        acc = lax.fori_loop(0, (ib + CNT_SUB) // CNT_SUB, step, jnp.zeros((LANES, LANES), jnp.int32))
        return jnp.sum(acc, axis=0, keepdims=True)

    zero_row = jnp.zeros((1, LANES), jnp.int32)
    at_zero = count(lambda k: k >= zero_row)
    start = (jnp.where(at_zero >= ktop, zero_row, jnp.full((1, LANES), INT_MIN, jnp.int32)),
             jnp.where(at_zero >= ktop, at_zero, ktop))

    def bit_step(it, st):
        thr, at_thr = st
        cand = thr | jnp.left_shift(jnp.int32(1), 30 - it)
        at_cand = count(lambda k: k >= cand)
        ok = at_cand >= ktop
        return jnp.where(ok, cand, thr), jnp.where(ok, at_cand, at_thr)

    thr, at_thr = lax.fori_loop(0, 31, bit_step, start)
    excess = jnp.where(thr == INT_MIN, 0, at_thr - ktop)

    @pl.when(pl.program_id(1) == 0)
    def _():
        def widest(t, m):
            kvf = kv_ref[tile(t), :].astype(F32)
            return jnp.maximum(m, jnp.max(jnp.sum(kvf * kvf, axis=1, keepdims=True)))
        kvn_ref[...] = jnp.full(kvn_ref.shape, lax.fori_loop(0, kv_ref.shape[0] // DSA_KEYS, widest, jnp.float32(0.0)))

    ones = jnp.ones((8, QF_W), BF16)
    bounds = []
    for p in pairs:
        qp = qfp_ref[2 * p * LANES:(2 * p + 2) * LANES, :].astype(F32)
        qn2 = _dot_t(ones, (qp * qp).astype(BF16))[0:1, :]
        kvn = kvn_ref[0:1, :]
        bounds.append(jnp.sqrt(qn2 * jnp.concatenate([kvn, kvn], axis=1)) * 1.05 + 1e-6)
    bound_max = functools.reduce(jnp.maximum, bounds)
    bound_all = jnp.concatenate(bounds, axis=1)
    fast = (jnp.max(excess) == 0) & (jnp.max(bound_max) <= SAFE_LOGIT_BOUND)
    acc_ref[...] = jnp.zeros_like(acc_ref)

    def twice(a):
        return jnp.concatenate([a, a], axis=1)

    def atile(t):
        return pl.ds(pl.multiple_of(t * ATT_KEYS, ATT_KEYS), ATT_KEYS)

    nta = (ib + ATT_SUB) // ATT_SUB

    @pl.when(fast)
    def _():
        def attend(t, c):
            skt = sk_ref[atile(t), :]
            sel = jnp.where((skt >= thr) & (skt > INT_MIN), 1.0, 0.0).astype(BF16)
            logits = _dot_t(kv_ref[atile(t), :], qfp_ref[...])
            prob = jnp.exp(logits - bound_all).astype(BF16) * jnp.concatenate([sel] * N_HEADS, axis=1)
            acc_ref[...] += _dot(ckvt_ref[t], prob)
            return c

        lax.fori_loop(0, nta, attend, 0)

    @pl.when(jnp.logical_not(fast))
    def _():
        need = (ktop - count(lambda k: k > thr)).astype(F32)
        m_ref[...] = jnp.full_like(m_ref, -1e29)
        lt = lt_ref[...]

        def attend(t, tie_carry):
            skt = sk_ref[atile(t), :]
            tie = (skt == thr) & (skt > INT_MIN)
            rank = _dot(lt, jnp.where(tie, 1.0, 0.0).astype(BF16)) + tie_carry
            sel = twice((skt > thr) | (tie & (rank <= need)))
            kvb = kv_ref[atile(t), :]
            cb = ckvt_ref[t]
            for p in pairs:
                pc = slice(2 * p * LANES, (2 * p + 2) * LANES)
                s = jnp.where(sel, _dot_t(kvb, qfp_ref[pc, :]), -1e30)
                m_old = m_ref[p:p + 1, :]
                m_new = jnp.maximum(m_old, jnp.max(s, axis=0, keepdims=True))
                prob = jnp.exp(s - m_new).astype(BF16)
                acc_ref[:, pc] = acc_ref[:, pc] * jnp.exp(m_old - m_new) + _dot(cb, prob)
                m_ref[p:p + 1, :] = m_new
            return rank[ATT_KEYS - 1:ATT_KEYS, :]

        lax.fori_loop(0, nta, attend, jnp.zeros((1, LANES), F32))

    for h in range(N_HEADS):
        cols = slice(h * LANES, (h + 1) * LANES)
        o = (acc_ref[:KV_LATENT, cols] / acc_ref[KV_LATENT:KV_LATENT + 1, cols]).astype(BF16)
        ybt_ref[h * HEAD_DIM:(h + 1) * HEAD_DIM, :] = _dot(wuvt_ref[h], o)
    o_ref[...] = ybt_ref[...].T.astype(BF16)


def _dsa(qi, wit, qf, kip, kvp, ckvt3, lt, wuvt, batch, seq, ktop):
    nq = seq // LANES
    tpk = kip.shape[0] // batch
    ntk = tpk // ATT_KEYS
    return pl.pallas_call(
        functools.partial(_dsa_kernel, ktop=ktop),
        grid=(batch, nq),
        in_specs=[pl.BlockSpec((LANES, HEADS_W), lambda b, i: (b * nq + i, 0)),
                  pl.BlockSpec((16, LANES), lambda b, i: (0, b * nq + i)),
                  pl.BlockSpec((LANES, N_HEADS * QF_W), lambda b, i: (b * nq + i, 0)),
                  pl.BlockSpec((tpk, LANES), lambda b, i: (b, 0)),
                  pl.BlockSpec((tpk, QF_W), lambda b, i: (b, 0)),
                  pl.BlockSpec((ntk, VAL_ROWS, ATT_KEYS), lambda b, i: (b, 0, 0)),
                  pl.BlockSpec(lt.shape, lambda b, i: (0, 0)),
                  pl.BlockSpec(wuvt.shape, lambda b, i: (0, 0, 0))],
        out_specs=pl.BlockSpec((LANES, HEADS_W), lambda b, i: (b * nq + i, 0)),
        out_shape=jax.ShapeDtypeStruct((batch * seq, HEADS_W), BF16),
        scratch_shapes=[pltpu.VMEM((tpk, LANES), jnp.int32),
                        pltpu.VMEM((VAL_ROWS, N_HEADS * LANES), F32),
                        pltpu.VMEM((8, 2 * LANES), F32),
                        pltpu.VMEM((8, LANES), F32),
                        pltpu.VMEM((N_HEADS * LANES, LANES), BF16),
                        pltpu.VMEM((N_HEADS * LANES, QF_W), BF16),
                        pltpu.VMEM((HEADS_W, LANES), F32)],
        compiler_params=pltpu.CompilerParams(dimension_semantics=("arbitrary",) * 2, vmem_limit_bytes=VMEM_LIMIT),
        name="dsa",
    )(qi, wit, qf, kip, kvp, ckvt3, lt, wuvt)


def _merge_kernel(x_ref, ya_ref, yb_ref, gt_ref, wua_ref, wub_ref, wo_ref, g_ref, wrt_ref, brt_ref, slt_ref,
                  h1_ref, u2_ref, r_ref, cnt_ref, carry_ref):
    @pl.when(pl.program_id(0) == 0)
    def _():
        carry_ref[...] = jnp.zeros_like(carry_ref)

    gt = gt_ref[...].astype(F32)
    z = gt[:, :D_MODEL] * _dot(ya_ref[...], wua_ref[...]) + gt[:, D_MODEL:] * _dot(yb_ref[...], wub_ref[...])
    h1 = x_ref[...] + _dot(z.astype(BF16), wo_ref[...])
    h1_ref[...] = h1
    u2 = _rmsnorm(h1, g_ref[...])
    u2_ref[...] = u2
    u_hi = u2.astype(BF16)
    u_lo = (u2 - u_hi.astype(F32)).astype(BF16)
    hi_terms = _dot(u_hi, wrt_ref[...])
    lg = hi_terms[:, :LANES] + hi_terms[:, LANES:] + _dot(u_lo, wrt_ref[:, :LANES]) + brt_ref[...]

    tm = lg.shape[0]
    lane = lax.broadcasted_iota(jnp.int32, (tm, LANES), 1)
    big = jnp.int32(LANES)

    def softmax(mask):
        x = jnp.where(mask, lg, -jnp.inf)
        e = jnp.where(mask, jnp.exp(x - jnp.max(x, axis=-1, keepdims=True)), 0.0)
        return e / jnp.sum(e, axis=-1, keepdims=True)

    def top1(p, mask):
        best = jnp.max(jnp.where(mask, p, -1.0), axis=-1, keepdims=True)
        idx = jnp.min(jnp.where(mask & (p == best), lane, big), axis=-1, keepdims=True)
        return best, idx

    gmask = (lane >= N_EXPERTS) & (lane < N_EXPERTS + N_GROUPS)
    p_grp, g_lane = top1(softmax(gmask), gmask)
    first = (g_lane - N_EXPERTS) * EXP_PER_GROUP
    emask = (lane >= first) & (lane < first + EXP_PER_GROUP)
    ep = softmax(emask)
    p0, e0 = top1(ep, emask)
    rest = emask & (lane != e0)
    p1, e1 = top1(ep, rest)
    den = p0 + p1
    w0 = p_grp * p0 / den
    w1 = p_grp * p1 / den

    oh0 = lane == e0
    oh1 = lane == e1
    both = (oh0 | oh1).astype(BF16)
    before = _dot(slt_ref[...], both) + carry_ref[...]
    rank0 = jnp.sum(jnp.where(oh0, before, 0.0), axis=-1, keepdims=True)
    rank1 = jnp.sum(jnp.where(oh1, before, 0.0), axis=-1, keepdims=True)
    carry = carry_ref[...] + jnp.sum(both.astype(F32), axis=0, keepdims=True)
    carry_ref[...] = carry
    cnt_ref[...] = carry

    cols = (e0.astype(F32), e1.astype(F32), w0, w1, rank0, rank1)
    r = jnp.zeros((tm, LANES), F32)
    for c, v in enumerate(cols):
        r = jnp.where(lane == c, v, r)
    r_ref[...] = r


def _merge(x2, ya, yb, gt, wua, wub, wo, g, wrt, brt, slt):
    n = x2.shape[0]
    tm = MERGE_TILE
    row = lambda width: pl.BlockSpec((tm, width), lambda i: (i, 0))
    const = lambda a: pl.BlockSpec(a.shape, lambda i: (0,) * a.ndim)
    return pl.pallas_call(
        _merge_kernel,
        grid=(n // tm,),
        in_specs=[row(D_MODEL), row(HEADS_W), row(HEADS_W), row(2 * D_MODEL), const(wua), const(wub), const(wo),
                  const(g), const(wrt), const(brt), const(slt)],
        out_specs=(row(D_MODEL), row(D_MODEL), row(LANES), pl.BlockSpec((1, LANES), lambda i: (0, 0))),
        out_shape=(jax.ShapeDtypeStruct((n, D_MODEL), F32), jax.ShapeDtypeStruct((n, D_MODEL), F32),
                   jax.ShapeDtypeStruct((n, LANES), F32), jax.ShapeDtypeStruct((1, LANES), F32)),
        scratch_shapes=[pltpu.VMEM((1, LANES), F32)],
        compiler_params=pltpu.CompilerParams(dimension_semantics=("arbitrary",), vmem_limit_bytes=VMEM_LIMIT),
        name="merge",
    )(x2, ya, yb, gt, wua, wub, wo, g, wrt, brt, slt)


def _row_copy(src, src_row, dst, dst_row, sem):
    return pltpu.make_async_copy(src.at[pl.ds(src_row, 1)], dst.at[pl.ds(dst_row, 1)], sem)


def _dispatch_kernel(dest_ref, u_ref, xs_in_ref, xs_ref, sem):
    del xs_in_ref

    def issue(t, c):
        for s in range(2):
            _row_copy(u_ref, t, xs_ref, dest_ref[0, 0, 2 * t + s], sem).start(priority=s)
        return c

    def drain(t, c):
        for s in range(2):
            _row_copy(u_ref, t, xs_ref, dest_ref[0, 0, 2 * t + s], sem).wait()
        return c

    lax.fori_loop(0, DISPATCH_TILE, issue, 0, unroll=ROW_DMA_UNROLL)
    lax.fori_loop(0, DISPATCH_TILE, drain, 0, unroll=ROW_DMA_UNROLL)


def _dispatch(dest3, u2, xs0):
    n = u2.shape[0]
    return pl.pallas_call(
        _dispatch_kernel,
        grid=(n // DISPATCH_TILE,),
        in_specs=[pl.BlockSpec((1, 1, 2 * DISPATCH_TILE), lambda i: (i, 0, 0), memory_space=pltpu.SMEM),
                  pl.BlockSpec((DISPATCH_TILE, D_MODEL), lambda i: (i, 0)), pl.BlockSpec(memory_space=pl.ANY)],
        out_specs=pl.BlockSpec(memory_space=pl.ANY),
        out_shape=jax.ShapeDtypeStruct(xs0.shape, xs0.dtype),
        scratch_shapes=[pltpu.SemaphoreType.DMA],
        input_output_aliases={2: 0},
        compiler_params=pltpu.CompilerParams(dimension_semantics=("arbitrary",), has_side_effects=True),
        name="dispatch",
    )(dest3, u2, xs0)


def _ffn_kernel(te_ref, nu_ref, xs_ref, w1_ref, w3_ref, w2_ref, ys_ref, w1b_ref, w3b_ref, w2b_ref):
    t = pl.program_id(0)

    @pl.when((t == 0) | (te_ref[t] != te_ref[jnp.maximum(t - 1, 0)]))
    def _():
        w1b_ref[...] = w1_ref[0].astype(BF16)
        w3b_ref[...] = w3_ref[0].astype(BF16)
        w2b_ref[...] = w2_ref[0].astype(BF16)

    @pl.when(t < nu_ref[0])
    def _():
        x = xs_ref[...].astype(BF16)
        hid = jax.nn.silu(_dot(x, w1b_ref[...])) * _dot(x, w3b_ref[...])
        ys_ref[...] = _dot(hid.astype(BF16), w2b_ref[...])

    @pl.when(t >= nu_ref[0])
    def _():
        ys_ref[...] = jnp.zeros_like(ys_ref)


def _ffn(tile_expert, n_used, xs, w1, w3, w2):
    p = xs.shape[0]
    grid_spec = pltpu.PrefetchScalarGridSpec(
        num_scalar_prefetch=2,
        grid=(p // FFN_TILE,),
        in_specs=[pl.BlockSpec((FFN_TILE, D_MODEL), lambda t, te, nu: (jnp.minimum(t, nu[0] - 1), 0)),
                  pl.BlockSpec((1, D_MODEL, D_EXPERT), lambda t, te, nu: (te[t], 0, 0)),
                  pl.BlockSpec((1, D_MODEL, D_EXPERT), lambda t, te, nu: (te[t], 0, 0)),
                  pl.BlockSpec((1, D_EXPERT, D_MODEL), lambda t, te, nu: (te[t], 0, 0))],
        out_specs=pl.BlockSpec((FFN_TILE, D_MODEL), lambda t, te, nu: (t, 0)),
        scratch_shapes=[pltpu.VMEM((D_MODEL, D_EXPERT), BF16), pltpu.VMEM((D_MODEL, D_EXPERT), BF16),
                        pltpu.VMEM((D_EXPERT, D_MODEL), BF16)],
    )
    return pl.pallas_call(
        _ffn_kernel,
        grid_spec=grid_spec,
        out_shape=jax.ShapeDtypeStruct((p, D_MODEL), F32),
        compiler_params=pltpu.CompilerParams(dimension_semantics=("arbitrary",), vmem_limit_bytes=VMEM_LIMIT),
        name="ffn",
    )(tile_expert, n_used, xs, w1, w3, w2)


def _final_kernel(dest_ref, h1_ref, r_ref, g_ref, ys_ref, o_ref, buf_ref, sem):
    def issue(t, c):
        for s in range(2):
            _row_copy(ys_ref, dest_ref[0, 0, 2 * t + s], buf_ref.at[s], t, sem).start(priority=s)
        return c

    def drain(t, c):
        for s in range(2):
            _row_copy(ys_ref, dest_ref[0, 0, 2 * t + s], buf_ref.at[s], t, sem).wait()
        return c

    lax.fori_loop(0, FINAL_TILE, issue, 0, unroll=ROW_DMA_UNROLL)
    lax.fori_loop(0, FINAL_TILE, drain, 0, unroll=ROW_DMA_UNROLL)
    r = r_ref[...]
    moe = r[:, 2:3] * buf_ref[0] + r[:, 3:4] * buf_ref[1]
    o_ref[...] = _rmsnorm(h1_ref[...] + moe, g_ref[...])


def _final(dest3, h1, r, g, ys):
    n = h1.shape[0]
    tm = FINAL_TILE
    return pl.pallas_call(
        _final_kernel,
        grid=(n // tm,),
        in_specs=[pl.BlockSpec((1, 1, 2 * tm), lambda i: (i, 0, 0), memory_space=pltpu.SMEM),
                  pl.BlockSpec((tm, D_MODEL), lambda i: (i, 0)),
                  pl.BlockSpec((tm, LANES), lambda i: (i, 0)),
                  pl.BlockSpec((1, D_MODEL), lambda i: (0, 0)),
                  pl.BlockSpec(memory_space=pl.ANY)],
        out_specs=pl.BlockSpec((tm, D_MODEL), lambda i: (i, 0)),
        out_shape=jax.ShapeDtypeStruct((n, D_MODEL), F32),
        scratch_shapes=[pltpu.VMEM((2, tm, D_MODEL), F32), pltpu.SemaphoreType.DMA],
        compiler_params=pltpu.CompilerParams(dimension_semantics=("arbitrary",), vmem_limit_bytes=VMEM_LIMIT),
        name="final",
    )(dest3, h1, r, g, ys)


def _rope_tables(pos):
    half = ROPE_DIM // 2
    inv = np.float32(ROPE_THETA) ** (-np.arange(half, dtype=np.float32) / np.float32(half))
    ang = np.asarray(pos, np.float32)[:, None] * inv[None, :]
    cos, sin = np.cos(ang).astype(np.float32), np.sin(ang).astype(np.float32)
    d = np.arange(LANES) % HEAD_DIM
    f = d % half
    cos_t = np.where(d < ROPE_DIM, cos[:, f], np.float32(1))
    sa = np.where((d >= half) & (d < ROPE_DIM), sin[:, f], np.float32(0))
    sb = np.where(d < half, -sin[:, f], np.float32(0))
    return tuple(jnp.asarray(t, F32) for t in (cos_t, sa, sb))


def _proj_weights(w_in, w_uk):
    qa, ka, va, qb, ckv, krope, qi, ki, wi, ga, gb = jnp.split(w_in, list(np.cumsum(SPLITS)[:-1]), axis=-1)
    zeros = lambda width: jnp.zeros((D_MODEL, width), w_in.dtype)
    kv = jnp.concatenate([krope, zeros(LANES - ROPE_DIM), ckv], axis=1)
    w = jnp.concatenate([qa * ATTN_SCALE, ka, va, qb, qi, ki, ki, kv, ga, gb], axis=1).astype(BF16)
    assert w.shape[1] == PROJ_W
    nope = HEAD_DIM - ROPE_DIM
    per_head = jnp.concatenate([
        jnp.concatenate([jnp.eye(ROPE_DIM, dtype=F32)[None].repeat(N_HEADS, 0),
                         jnp.zeros((N_HEADS, ROPE_DIM, QF_W - ROPE_DIM), F32)], axis=2),
        jnp.concatenate([jnp.zeros((N_HEADS, nope, LANES), F32), jnp.swapaxes(w_uk, 1, 2)], axis=2)], axis=1)
    head_eye = jnp.eye(N_HEADS, dtype=F32)
    wabs = (per_head[:, :, None, :] * (head_eye * ATTN_SCALE)[:, None, :, None])
    wabs = wabs.reshape(HEADS_W, N_HEADS * QF_W).astype(BF16)
    wwi = jnp.concatenate([wi.T, jnp.zeros((16 - N_HEADS, D_MODEL), w_in.dtype)], axis=0).astype(BF16)
    wckv = ckv.T.astype(BF16)
    return w, wabs, wwi, wckv


def _tri(n, strict_lower):
    r = np.arange(n)
    m = (r[None, :] < r[:, None]) if strict_lower else (r[None, :] <= r[:, None])
    return jnp.asarray(m, BF16)


def kernel(x, meta_tokens, norm_mix_g, w_in, w_uk, w_uv, w_up_a, w_up_b, w_o, norm_ffn_g, w_group, b_group,
           w_router, b_router, w1, w3, w2, norm_final_g):
    batch, seq, d = x.shape
    assert d == D_MODEL and seq % LANES == 0 and norm_mix_g.shape[0] == 1
    n = batch * seq
    nq = seq // LANES
    tp = seq + LANES
    ktop = min(TOPK_MAX, seq // 4)
    x2 = x.reshape(n, d)

    w, wabs, wwi, wckv = _proj_weights(w_in[0], w_uk[0])
    g_mix = norm_mix_g[0][None, :]
    proj_tile = 1024 if seq % 1024 == 0 else LANES
    px = _proj(x2, g_mix, _rope_tables(N_META + np.arange(seq)), w, wabs, wwi, wckv, proj_tile, seq // proj_tile)
    meta = jnp.concatenate([meta_tokens.astype(x.dtype), jnp.zeros((LANES - N_META, d), x.dtype)], axis=0)
    pm = _proj(meta, g_mix, _rope_tables(np.arange(LANES)), w, wabs, wwi, wckv, LANES, 1)
    qa_x, kva_x, qf_x, qi_x, ki_x, kv_x, gt_x, wit_x, ckvt_x = px
    _, kva_m, _, _, ki_m, kv_m, _, _, ckvt_m = pm

    def with_meta(xpart, mpart, rows):
        wdt = xpart.shape[1]
        parts = [jnp.broadcast_to(mpart[None], (batch, LANES, wdt)), xpart.reshape(batch, seq, wdt)]
        if rows > tp:
            parts.append(jnp.zeros((batch, rows - tp, wdt), xpart.dtype))
        return jnp.concatenate(parts, axis=1).reshape(batch * rows, wdt)

    kv_a = with_meta(kva_x, kva_m, tp)
    su = jnp.concatenate([_tri(LANES, True), jnp.ones((LANES, LANES), BF16)], axis=1)
    ya = _sb(qa_x, kv_a, su, batch, seq)

    tpk = -(-tp // DSA_KEYS) * DSA_KEYS
    kip = with_meta(ki_x, ki_m, tpk)
    kvp = with_meta(kv_x, kv_m, tpk)
    ckvt = jnp.concatenate([jnp.broadcast_to(ckvt_m[:, None, :], (KV_LATENT, batch, LANES)),
                            ckvt_x.reshape(KV_LATENT, batch, seq),
                            jnp.zeros((KV_LATENT, batch, tpk - tp), BF16)], axis=2)
    ones_row = jnp.concatenate([jnp.ones((1, batch, tpk), BF16),
                                jnp.zeros((VAL_ROWS - KV_LATENT - 1, batch, tpk), BF16)], axis=0)
    ckvt = jnp.concatenate([ckvt, ones_row], axis=0)
    ckvt3 = ckvt.reshape(VAL_ROWS, batch * tpk // ATT_KEYS, ATT_KEYS).transpose(1, 0, 2)
    wuvt = jnp.swapaxes(w_uv[0], 1, 2).astype(BF16)
    yb = _dsa(qi_x, wit_x, qf_x, kip, kvp, ckvt3, _tri(ATT_KEYS, False), wuvt, batch, seq, ktop)

    wrt = jnp.concatenate([w_router[0], w_group[0],
                           jnp.zeros((d, LANES - N_EXPERTS - N_GROUPS), F32)], axis=1).astype(F32)
    wrt_hi = wrt.astype(BF16)
    wrt = jnp.concatenate([wrt_hi, (wrt - wrt_hi.astype(F32)).astype(BF16)], axis=1)
    brt = jnp.concatenate([b_router[0], b_group[0], jnp.zeros((LANES - N_EXPERTS - N_GROUPS,), F32)])[None, :]
    h1, u2, r, counts = _merge(x2, ya, yb, gt_x, w_up_a[0].astype(BF16), w_up_b[0].astype(BF16),
                               w_o[0].astype(BF16), norm_ffn_g[0][None, :], wrt, brt.astype(F32),
                               _tri(MERGE_TILE, True))

    cnt = counts[0, :N_EXPERTS].astype(jnp.int32)
    padded = (cnt + FFN_TILE - 1) // FFN_TILE * FFN_TILE
    p_end = jnp.cumsum(padded)
    p_start = p_end - padded
    e_sel = r[:, 0:2].astype(jnp.int32)
    first_slot = jnp.sum(jnp.where(e_sel[..., None] == jnp.arange(N_EXPERTS), p_start, 0), axis=-1)
    dest = first_slot + r[:, 4:6].astype(jnp.int32)
    n_tiles = (2 * n) // FFN_TILE + N_EXPERTS
    tile_row = jnp.arange(n_tiles, dtype=jnp.int32) * FFN_TILE
    tile_expert = jnp.minimum(jnp.sum((p_end[None, :] <= tile_row[:, None]).astype(jnp.int32), axis=1),
                              N_EXPERTS - 1)
    n_used = (p_end[-1] // FFN_TILE).astype(jnp.int32)
    tile_expert = jnp.where(jnp.arange(n_tiles) < n_used, tile_expert, tile_expert[jnp.maximum(n_used - 1, 0)])

    xs0 = jnp.zeros((n_tiles * FFN_TILE, d), F32)
    xs = _dispatch(dest.reshape(n // DISPATCH_TILE, 1, 2 * DISPATCH_TILE), u2, xs0)
    ys = _ffn(tile_expert, n_used[None], xs, w1[0], w3[0], w2[0])
    out = _final(dest.reshape(n // FINAL_TILE, 1, 2 * FINAL_TILE), h1, r, norm_final_g[None, :], ys)
    return out.reshape(batch, seq, d)
```

```python
import functools
import math

import numpy as np
import jax
import jax.numpy as jnp
from jax import lax
from jax.experimental import pallas as pl
from jax.experimental.pallas import tpu as pltpu

D_MODEL = 1024
CHUNK = 64
N_META = 16
HEAD_DIM = 64
ROPE_DIM = 16
ROPE_THETA = 500000.0
N_HEADS = 8
KV_LATENT = 128
IDX_DIM = 64
TOPK_MAX = 256
N_GROUPS = 4
EXP_PER_GROUP = 8
N_EXPERTS = 32
D_EXPERT = 512
NORM_EPS = 1e-6
HEADS_W = N_HEADS * HEAD_DIM
SPLITS = (HEADS_W, HEADS_W, HEADS_W, HEADS_W, KV_LATENT, ROPE_DIM, N_HEADS * IDX_DIM, IDX_DIM, N_HEADS,
          D_MODEL, D_MODEL)

LANES = 128
QF_W = 2 * LANES
INT_MIN = -2 ** 31
EXP_UNDERFLOW = -104.0
ATTN_SCALE = 1.0 / math.sqrt(HEAD_DIM)
VMEM_LIMIT = 56 * 1024 * 1024
DSA_SUB = 8
DSA_KEYS = DSA_SUB * LANES
ATT_SUB = 4
ATT_KEYS = ATT_SUB * LANES
CNT_SUB = 2
CNT_KEYS = CNT_SUB * LANES
CNT_GROUP = 4
VAL_ROWS = KV_LATENT + 8
SAFE_LOGIT_BOUND = 40.0
FFN_TILE = 512
MERGE_TILE = 1024
FINAL_TILE = 1024
DISPATCH_TILE = 1024
ROW_DMA_UNROLL = 32

BF16 = jnp.bfloat16
F32 = jnp.float32

_C_QKV = (0, 3 * HEADS_W)
_C_QB = (_C_QKV[1], _C_QKV[1] + HEADS_W)
_C_QI = (_C_QB[1], _C_QB[1] + HEADS_W)
_C_KI = (_C_QI[1], _C_QI[1] + LANES)
_C_KV = (_C_KI[1], _C_KI[1] + QF_W)
_C_G = (_C_KV[1], _C_KV[1] + 2 * D_MODEL)
PROJ_W = _C_G[1]


def _dot(a, b):
    return jnp.dot(a, b, preferred_element_type=F32)


def _dot_t(a, b):
    return lax.dot_general(a, b, (((1,), (1,)), ((), ())), preferred_element_type=F32)


def _rmsnorm(x, g):
    return x * lax.rsqrt(jnp.mean(x * x, axis=-1, keepdims=True) + NORM_EPS) * g


def _rope(x, cos, sa, sb):
    w = x.shape[1]
    n = w // LANES
    if n > 1:
        cos, sa, sb = (jnp.concatenate([t] * n, axis=1) for t in (cos, sa, sb))
    return x * cos + pltpu.roll(x, 8, 1) * sa + pltpu.roll(x, w - 8, 1) * sb


def _proj_kernel(x_ref, g_ref, cos_ref, sa_ref, sb_ref, w_ref, wabs_ref, wwi_ref, wckv_ref,
                 qa_ref, kva_ref, qf_ref, qi_ref, ki_ref, kv_ref, gt_ref, wit_ref, ckvt_ref):
    ub = _rmsnorm(x_ref[...], g_ref[...]).astype(BF16)
    cos, sa, sb = cos_ref[...], sa_ref[...], sb_ref[...]
    qa_ref[...] = _dot(ub, w_ref[:, _C_QKV[0]:_C_QKV[0] + HEADS_W]).astype(BF16)
    kva_ref[...] = _dot(ub, w_ref[:, _C_QKV[0] + HEADS_W:_C_QKV[1]]).astype(BF16)
    qb = _rope(_dot(ub, w_ref[:, _C_QB[0]:_C_QB[1]]), cos, sa, sb).astype(BF16)
    qf_ref[...] = _dot(qb, wabs_ref[...]).astype(BF16)
    qi_ref[...] = _rope(_dot(ub, w_ref[:, _C_QI[0]:_C_QI[1]]), cos, sa, sb).astype(BF16)
    ki_ref[...] = _rope(_dot(ub, w_ref[:, _C_KI[0]:_C_KI[1]]), cos, sa, sb).astype(BF16)
    kv = _dot(ub, w_ref[:, _C_KV[0]:_C_KV[1]])
    kv_ref[...] = jnp.concatenate([_rope(kv[:, :LANES], cos, sa, sb), kv[:, LANES:]], axis=1).astype(BF16)
    gt_ref[...] = jax.nn.sigmoid(_dot(ub, w_ref[:, _C_G[0]:_C_G[1]])).astype(BF16)
    wit_ref[...] = _dot_t(wwi_ref[...], ub)
    ckvt_ref[...] = _dot_t(wckv_ref[...], ub).astype(BF16)


def _proj(x2, g, tabs, w, wabs, wwi, wckv, tile, tab_tiles):
    n = x2.shape[0]
    row = lambda width: pl.BlockSpec((tile, width), lambda i: (i, 0))
    const = lambda shape: pl.BlockSpec(shape, lambda i: (0, 0))
    tab = pl.BlockSpec((tile, LANES), lambda i: (i % tab_tiles, 0))
    out_shape = (
        jax.ShapeDtypeStruct((n, HEADS_W), BF16),
        jax.ShapeDtypeStruct((n, 2 * HEADS_W), BF16),
        jax.ShapeDtypeStruct((n, N_HEADS * QF_W), BF16),
        jax.ShapeDtypeStruct((n, HEADS_W), BF16),
        jax.ShapeDtypeStruct((n, LANES), BF16),
        jax.ShapeDtypeStruct((n, QF_W), BF16),
        jax.ShapeDtypeStruct((n, 2 * D_MODEL), BF16),
        jax.ShapeDtypeStruct((16, n), F32),
        jax.ShapeDtypeStruct((KV_LATENT, n), BF16),
    )
    out_specs = (row(HEADS_W), row(2 * HEADS_W), row(N_HEADS * QF_W), row(HEADS_W), row(LANES), row(QF_W),
                 row(2 * D_MODEL),
                 pl.BlockSpec((16, tile), lambda i: (0, i)), pl.BlockSpec((KV_LATENT, tile), lambda i: (0, i)))
    return pl.pallas_call(
        _proj_kernel,
        grid=(n // tile,),
        in_specs=[row(D_MODEL), const((1, D_MODEL)), tab, tab, tab, const(w.shape), const(wabs.shape),
                  const(wwi.shape), const(wckv.shape)],
        out_specs=out_specs,
        out_shape=out_shape,
        compiler_params=pltpu.CompilerParams(dimension_semantics=("arbitrary",), vmem_limit_bytes=VMEM_LIMIT),
        name="proj",
    )(x2, g, *tabs, w, wabs, wwi, wckv)


def _sb_kernel(q_ref, k_ref, v_ref, su_ref, o_ref, acc_ref, carry_ref):
    ib = pl.program_id(1) + 1
    lane = lax.broadcasted_iota(jnp.int32, (LANES, LANES), 1)
    row = lax.broadcasted_iota(jnp.int32, (LANES, LANES), 0)
    low = lane < HEAD_DIM
    high = jnp.logical_not(low)
    su = su_ref[...]
    qpos = ib * LANES + row
    acc_ref[...] = jnp.zeros_like(acc_ref)
    carry_ref[...] = jnp.zeros_like(carry_ref)

    def body(st):
        j, _ = st
        rows = pl.ds(pl.multiple_of(j * LANES, LANES), LANES)
        kpos = j * LANES + lane
        mask = ((kpos < qpos) & ((kpos < N_META) | (kpos >= LANES)))[None]
        zs = []
        for p in range(N_HEADS // 2):
            cols = slice(p * LANES, (p + 1) * LANES)
            q2, kb = q_ref[:, cols], k_ref[rows, cols]
            zero = jnp.zeros_like(q2)
            for keep in (low, high):
                zs.append(_dot_t(jnp.where(keep, q2, zero), kb))
        z = jnp.stack(zs)
        ls_pos = jnp.minimum(z, 0.0) - jnp.log(1.0 + jnp.exp(-jnp.abs(z)))
        log_keep = jnp.where(mask, ls_pos - z, 0.0)
        hi = log_keep.astype(BF16)
        lo = (log_keep - hi.astype(F32)).astype(BF16)
        flat = (N_HEADS * LANES, LANES)
        cs = _dot(hi.reshape(flat), su) + _dot(lo.reshape(flat), su)
        carry = carry_ref[...]
        a = jnp.where(mask, jnp.exp(ls_pos + cs[:, :LANES].reshape(z.shape) + carry), 0.0).astype(BF16)
        carry = carry + cs[:, LANES:].reshape(z.shape)
        carry_ref[...] = carry
        for p in range(N_HEADS // 2):
            vb = v_ref[rows, p * LANES:(p + 1) * LANES]
            zero = jnp.zeros_like(vb)
            acc_ref[p] += _dot(a[2 * p], jnp.where(low, vb, zero)) + _dot(a[2 * p + 1], jnp.where(high, vb, zero))
        return j - 1, jnp.max(carry)

    def cond(st):
        return (st[0] >= 0) & (st[1] > EXP_UNDERFLOW)

    lax.while_loop(cond, body, (ib, jnp.float32(0.0)))
    for p in range(N_HEADS // 2):
        o_ref[:, p * LANES:(p + 1) * LANES] = acc_ref[p].astype(BF16)


def _sb(qkv_x, kvp, su, batch, seq):
    nq = seq // LANES
    tp = seq + LANES
    return pl.pallas_call(
        _sb_kernel,
        grid=(batch, nq),
        in_specs=[pl.BlockSpec((LANES, HEADS_W), lambda b, i: (b * nq + i, 0)),
                  pl.BlockSpec((tp, HEADS_W), lambda b, i: (b, 0)),
                  pl.BlockSpec((tp, HEADS_W), lambda b, i: (b, 1)),
                  pl.BlockSpec(su.shape, lambda b, i: (0, 0))],
        out_specs=pl.BlockSpec((LANES, HEADS_W), lambda b, i: (b * nq + i, 0)),
        out_shape=jax.ShapeDtypeStruct((batch * seq, HEADS_W), BF16),
        scratch_shapes=[pltpu.VMEM((N_HEADS // 2, LANES, LANES), F32), pltpu.VMEM((N_HEADS, LANES, LANES), F32)],
        compiler_params=pltpu.CompilerParams(dimension_semantics=("arbitrary",) * 2, vmem_limit_bytes=VMEM_LIMIT),
        name="sb",
    )(qkv_x, kvp, kvp, su)


def _dsa_kernel(qi_ref, wit_ref, qf_ref, ki_ref, kv_ref, ckvt_ref, lt_ref, wuvt_ref, o_ref,
                sk_ref, acc_ref, m_ref, kvn_ref, qim_ref, qfp_ref, ybt_ref, *, ktop):
    ib = pl.program_id(1) + 1
    nt = (ib + DSA_SUB) // DSA_SUB
    lane = lax.broadcasted_iota(jnp.int32, (LANES, LANES), 1)
    klane = lax.broadcasted_iota(jnp.int32, (DSA_KEYS, LANES), 1)
    krow = lax.broadcasted_iota(jnp.int32, (DSA_KEYS, LANES), 0)
    low = lane < HEAD_DIM
    pairs = range(N_HEADS // 2)
    for p in pairs:
        blk = qi_ref[:, p * LANES:(p + 1) * LANES]
        zero = jnp.zeros_like(blk)
        qim_ref[2 * p * LANES:(2 * p + 1) * LANES, :] = jnp.where(low, blk, zero)
        qim_ref[(2 * p + 1) * LANES:(2 * p + 2) * LANES, :] = jnp.where(low, zero, blk)
        qfp_ref[2 * p * LANES:(2 * p + 1) * LANES, :] = qf_ref[:, 2 * p * QF_W:(2 * p + 1) * QF_W]
        qfp_ref[(2 * p + 1) * LANES:(2 * p + 2) * LANES, :] = qf_ref[:, (2 * p + 1) * QF_W:(2 * p + 2) * QF_W]
    wit = wit_ref[...]
    wit_all = jnp.concatenate([wit[h:h + 1, :] for h in range(N_HEADS)], axis=1)
    cq = 1 + ((ib * LANES + klane - LANES) >> 6)

    def tile(t):
        return pl.ds(pl.multiple_of(t * DSA_KEYS, DSA_KEYS), DSA_KEYS)

    def scores(t, c):
        kb = ki_ref[tile(t), :]
        part = wit_all * jnp.maximum(_dot_t(kb, qim_ref[...]), 0.0)
        s = part[:, :LANES]
        for h in range(1, N_HEADS):
            s = s + part[:, h * LANES:(h + 1) * LANES]
        s = jnp.where(s == 0.0, 0.0, s)
        bits = lax.bitcast_convert_type(s, jnp.int32)
        key = bits ^ ((bits >> 31) & 0x7FFFFFFF)
        rk = t * DSA_KEYS + krow
        ck = jnp.where(rk < LANES, 0, 1 + ((rk - LANES) >> 6))
        adm = ((rk < N_META) | (rk >= LANES)) & (ck <= cq)
        sk_ref[tile(t), :] = jnp.where(adm, key, INT_MIN)
        return c

    lax.fori_loop(0, nt, scores, 0)

    def count(pred):
        def step(t, acc):
            start = pl.multiple_of(t * CNT_KEYS, CNT_KEYS)
            hit = pred(sk_ref[pl.ds(start, CNT_KEYS), :]).astype(jnp.int32)
            for u in range(CNT_SUB):
                acc = acc + hit[u * LANES:(u + 1) * LANES]
            return acc
        ntc = (ib + CNT_SUB) // CNT_SUB

        def group(g, a):
            for u in range(CNT_GROUP):
                a = step(CNT_GROUP * g + u, a)
            return a

        acc = lax.fori_loop(0, ntc // CNT_GROUP, group, jnp.zeros((LANES, LANES), jnp.int32))
        acc = lax.fori_loop(ntc // CNT_GROUP * CNT_GROUP, ntc, step, acc)
        return jnp.sum(acc, axis=0, keepdims=True)

    zero_row = jnp.zeros((1, LANES), jnp.int32)
    at_zero = count(lambda k: k >= zero_row)
    start = (jnp.where(at_zero >= ktop, zero_row, jnp.full((1, LANES), INT_MIN, jnp.int32)),
             jnp.where(at_zero >= ktop, at_zero, ktop))

    def bit_step(it, st):
        thr, at_thr = st
        cand = thr | jnp.left_shift(jnp.int32(1), 30 - it)
        at_cand = count(lambda k: k >= cand)
        ok = at_cand >= ktop
        return jnp.where(ok, cand, thr), jnp.where(ok, at_cand, at_thr)

    thr, at_thr = lax.fori_loop(0, 31, bit_step, start)
    excess = jnp.where(thr == INT_MIN, 0, at_thr - ktop)

    @pl.when(pl.program_id(1) == 0)
    def _():
        def widest(t, m):
            kvf = kv_ref[tile(t), :].astype(F32)
            return jnp.maximum(m, jnp.max(jnp.sum(kvf * kvf, axis=1, keepdims=True)))
        kvn_ref[...] = jnp.full(kvn_ref.shape, lax.fori_loop(0, kv_ref.shape[0] // DSA_KEYS, widest, jnp.float32(0.0)))

    ones = jnp.ones((8, QF_W), BF16)
    bounds = []
    for p in pairs:
        qp = qfp_ref[2 * p * LANES:(2 * p + 2) * LANES, :].astype(F32)
        qn2 = _dot_t(ones, (qp * qp).astype(BF16))[0:1, :]
        kvn = kvn_ref[0:1, :]
        bounds.append(jnp.sqrt(qn2 * jnp.concatenate([kvn, kvn], axis=1)) * 1.05 + 1e-6)
    bound_max = functools.reduce(jnp.maximum, bounds)
    bound_all = jnp.concatenate(bounds, axis=1)
    fast = (jnp.max(excess) == 0) & (jnp.max(bound_max) <= SAFE_LOGIT_BOUND)
    acc_ref[...] = jnp.zeros_like(acc_ref)

    def twice(a):
        return jnp.concatenate([a, a], axis=1)

    def atile(t):
        return pl.ds(pl.multiple_of(t * ATT_KEYS, ATT_KEYS), ATT_KEYS)

    nta = (ib + ATT_SUB) // ATT_SUB

    @pl.when(fast)
    def _():
        def attend(t, c):
            skt = sk_ref[atile(t), :]
            sel = jnp.concatenate([(skt >= thr) & (skt > INT_MIN)] * N_HEADS, axis=1)
            logits = _dot_t(kv_ref[atile(t), :], qfp_ref[...])
            prob = jnp.where(sel, jnp.exp(logits - bound_all), 0.0).astype(BF16)
            acc_ref[...] += _dot(ckvt_ref[t], prob)
            return c

        lax.fori_loop(0, nta, attend, 0)

    @pl.when(jnp.logical_not(fast))
    def _():
        need = (ktop - count(lambda k: k > thr)).astype(F32)
        m_ref[...] = jnp.full_like(m_ref, -1e29)
        lt = lt_ref[...]

        def attend(t, tie_carry):
            skt = sk_ref[atile(t), :]
            tie = (skt == thr) & (skt > INT_MIN)
            rank = _dot(lt, jnp.where(tie, 1.0, 0.0).astype(BF16)) + tie_carry
            sel = twice((skt > thr) | (tie & (rank <= need)))
            kvb = kv_ref[atile(t), :]
            cb = ckvt_ref[t]
            for p in pairs:
                pc = slice(2 * p * LANES, (2 * p + 2) * LANES)
                s = jnp.where(sel, _dot_t(kvb, qfp_ref[pc, :]), -1e30)
                m_old = m_ref[p:p + 1, :]
                m_new = jnp.maximum(m_old, jnp.max(s, axis=0, keepdims=True))
                prob = jnp.exp(s - m_new).astype(BF16)
                acc_ref[:, pc] = acc_ref[:, pc] * jnp.exp(m_old - m_new) + _dot(cb, prob)
                m_ref[p:p + 1, :] = m_new
            return rank[ATT_KEYS - 1:ATT_KEYS, :]

        lax.fori_loop(0, nta, attend, jnp.zeros((1, LANES), F32))

    for h in range(N_HEADS):
        cols = slice(h * LANES, (h + 1) * LANES)
        o = (acc_ref[:KV_LATENT, cols] / acc_ref[KV_LATENT:KV_LATENT + 1, cols]).astype(BF16)
        ybt_ref[h * HEAD_DIM:(h + 1) * HEAD_DIM, :] = _dot(wuvt_ref[h], o)
    o_ref[...] = ybt_ref[...].T.astype(BF16)


def _dsa(qi, wit, qf, kip, kvp, ckvt3, lt, wuvt, batch, seq, ktop):
    nq = seq // LANES
    tpk = kip.shape[0] // batch
    ntk = tpk // ATT_KEYS
    return pl.pallas_call(
        functools.partial(_dsa_kernel, ktop=ktop),
        grid=(batch, nq),
        in_specs=[pl.BlockSpec((LANES, HEADS_W), lambda b, i: (b * nq + i, 0)),
                  pl.BlockSpec((16, LANES), lambda b, i: (0, b * nq + i)),
                  pl.BlockSpec((LANES, N_HEADS * QF_W), lambda b, i: (b * nq + i, 0)),
                  pl.BlockSpec((tpk, LANES), lambda b, i: (b, 0)),
                  pl.BlockSpec((tpk, QF_W), lambda b, i: (b, 0)),
                  pl.BlockSpec((ntk, VAL_ROWS, ATT_KEYS), lambda b, i: (b, 0, 0)),
                  pl.BlockSpec(lt.shape, lambda b, i: (0, 0)),
                  pl.BlockSpec(wuvt.shape, lambda b, i: (0, 0, 0))],
        out_specs=pl.BlockSpec((LANES, HEADS_W), lambda b, i: (b * nq + i, 0)),
        out_shape=jax.ShapeDtypeStruct((batch * seq, HEADS_W), BF16),
        scratch_shapes=[pltpu.VMEM((tpk, LANES), jnp.int32),
                        pltpu.VMEM((VAL_ROWS, N_HEADS * LANES), F32),
                        pltpu.VMEM((8, 2 * LANES), F32),
                        pltpu.VMEM((8, LANES), F32),
                        pltpu.VMEM((N_HEADS * LANES, LANES), BF16),
                        pltpu.VMEM((N_HEADS * LANES, QF_W), BF16),
                        pltpu.VMEM((HEADS_W, LANES), F32)],
        compiler_params=pltpu.CompilerParams(dimension_semantics=("arbitrary",) * 2, vmem_limit_bytes=VMEM_LIMIT),
        name="dsa",
    )(qi, wit, qf, kip, kvp, ckvt3, lt, wuvt)


def _merge_kernel(x_ref, ya_ref, yb_ref, gt_ref, wua_ref, wub_ref, wo_ref, g_ref, wrt_ref, brt_ref, slt_ref,
                  h1_ref, u2_ref, r_ref, cnt_ref, carry_ref):
    @pl.when(pl.program_id(0) == 0)
    def _():
        carry_ref[...] = jnp.zeros_like(carry_ref)

    gt = gt_ref[...].astype(F32)
    z = gt[:, :D_MODEL] * _dot(ya_ref[...], wua_ref[...]) + gt[:, D_MODEL:] * _dot(yb_ref[...], wub_ref[...])
    h1 = x_ref[...] + _dot(z.astype(BF16), wo_ref[...])
    h1_ref[...] = h1
    u2 = _rmsnorm(h1, g_ref[...])
    u2_ref[...] = u2
    u_hi = u2.astype(BF16)
    u_lo = (u2 - u_hi.astype(F32)).astype(BF16)
    hi_terms = _dot(u_hi, wrt_ref[...])
    lg = hi_terms[:, :LANES] + hi_terms[:, LANES:] + _dot(u_lo, wrt_ref[:, :LANES]) + brt_ref[...]

    tm = lg.shape[0]
    lane = lax.broadcasted_iota(jnp.int32, (tm, LANES), 1)
    big = jnp.int32(LANES)

    def softmax(mask):
        x = jnp.where(mask, lg, -jnp.inf)
        e = jnp.where(mask, jnp.exp(x - jnp.max(x, axis=-1, keepdims=True)), 0.0)
        return e / jnp.sum(e, axis=-1, keepdims=True)

    def top1(p, mask):
        best = jnp.max(jnp.where(mask, p, -1.0), axis=-1, keepdims=True)
        idx = jnp.min(jnp.where(mask & (p == best), lane, big), axis=-1, keepdims=True)
        return best, idx

    gmask = (lane >= N_EXPERTS) & (lane < N_EXPERTS + N_GROUPS)
    p_grp, g_lane = top1(softmax(gmask), gmask)
    first = (g_lane - N_EXPERTS) * EXP_PER_GROUP
    emask = (lane >= first) & (lane < first + EXP_PER_GROUP)
    ep = softmax(emask)
    p0, e0 = top1(ep, emask)
    rest = emask & (lane != e0)
    p1, e1 = top1(ep, rest)
    den = p0 + p1
    w0 = p_grp * p0 / den
    w1 = p_grp * p1 / den

    oh0 = lane == e0
    oh1 = lane == e1
    both = (oh0 | oh1).astype(BF16)
    before = _dot(slt_ref[...], both) + carry_ref[...]
    rank0 = jnp.sum(jnp.where(oh0, before, 0.0), axis=-1, keepdims=True)
    rank1 = jnp.sum(jnp.where(oh1, before, 0.0), axis=-1, keepdims=True)
    carry = carry_ref[...] + jnp.sum(both.astype(F32), axis=0, keepdims=True)
    carry_ref[...] = carry
    cnt_ref[...] = carry

    cols = (e0.astype(F32), e1.astype(F32), w0, w1, rank0, rank1)
    r = jnp.zeros((tm, LANES), F32)
    for c, v in enumerate(cols):
        r = jnp.where(lane == c, v, r)
    r_ref[...] = r


def _merge(x2, ya, yb, gt, wua, wub, wo, g, wrt, brt, slt):
    n = x2.shape[0]
    tm = MERGE_TILE
    row = lambda width: pl.BlockSpec((tm, width), lambda i: (i, 0))
    const = lambda a: pl.BlockSpec(a.shape, lambda i: (0,) * a.ndim)
    return pl.pallas_call(
        _merge_kernel,
        grid=(n // tm,),
        in_specs=[row(D_MODEL), row(HEADS_W), row(HEADS_W), row(2 * D_MODEL), const(wua), const(wub), const(wo),
                  const(g), const(wrt), const(brt), const(slt)],
        out_specs=(row(D_MODEL), row(D_MODEL), row(LANES), pl.BlockSpec((1, LANES), lambda i: (0, 0))),
        out_shape=(jax.ShapeDtypeStruct((n, D_MODEL), F32), jax.ShapeDtypeStruct((n, D_MODEL), F32),
                   jax.ShapeDtypeStruct((n, LANES), F32), jax.ShapeDtypeStruct((1, LANES), F32)),
        scratch_shapes=[pltpu.VMEM((1, LANES), F32)],
        compiler_params=pltpu.CompilerParams(dimension_semantics=("arbitrary",), vmem_limit_bytes=VMEM_LIMIT),
        name="merge",
    )(x2, ya, yb, gt, wua, wub, wo, g, wrt, brt, slt)


def _row_copy(src, src_row, dst, dst_row, sem):
    return pltpu.make_async_copy(src.at[pl.ds(src_row, 1)], dst.at[pl.ds(dst_row, 1)], sem)


def _dispatch_kernel(dest_ref, u_ref, xs_in_ref, xs_ref, sem):
    del xs_in_ref

    def issue(t, c):
        for s in range(2):
            _row_copy(u_ref, t, xs_ref, dest_ref[0, 0, 2 * t + s], sem).start(priority=s)
        return c

    def drain(t, c):
        for s in range(2):
            _row_copy(u_ref, t, xs_ref, dest_ref[0, 0, 2 * t + s], sem).wait()
        return c

    lax.fori_loop(0, DISPATCH_TILE, issue, 0, unroll=ROW_DMA_UNROLL)
    lax.fori_loop(0, DISPATCH_TILE, drain, 0, unroll=ROW_DMA_UNROLL)


def _dispatch(dest3, u2, xs0):
    n = u2.shape[0]
    return pl.pallas_call(
        _dispatch_kernel,
        grid=(n // DISPATCH_TILE,),
        in_specs=[pl.BlockSpec((1, 1, 2 * DISPATCH_TILE), lambda i: (i, 0, 0), memory_space=pltpu.SMEM),
                  pl.BlockSpec((DISPATCH_TILE, D_MODEL), lambda i: (i, 0)), pl.BlockSpec(memory_space=pl.ANY)],
        out_specs=pl.BlockSpec(memory_space=pl.ANY),
        out_shape=jax.ShapeDtypeStruct(xs0.shape, xs0.dtype),
        scratch_shapes=[pltpu.SemaphoreType.DMA],
        input_output_aliases={2: 0},
        compiler_params=pltpu.CompilerParams(dimension_semantics=("arbitrary",), has_side_effects=True),
        name="dispatch",
    )(dest3, u2, xs0)


def _ffn_kernel(te_ref, nu_ref, xs_ref, w1_ref, w3_ref, w2_ref, ys_ref, w1b_ref, w3b_ref, w2b_ref):
    t = pl.program_id(0)

    @pl.when((t == 0) | (te_ref[t] != te_ref[jnp.maximum(t - 1, 0)]))
    def _():
        w1b_ref[...] = w1_ref[0].astype(BF16)
        w3b_ref[...] = w3_ref[0].astype(BF16)
        w2b_ref[...] = w2_ref[0].astype(BF16)

    @pl.when(t < nu_ref[0])
    def _():
        x = xs_ref[...].astype(BF16)
        hid = jax.nn.silu(_dot(x, w1b_ref[...])) * _dot(x, w3b_ref[...])
        ys_ref[...] = _dot(hid.astype(BF16), w2b_ref[...])

    @pl.when(t >= nu_ref[0])
    def _():
        ys_ref[...] = jnp.zeros_like(ys_ref)


def _ffn(tile_expert, n_used, xs, w1, w3, w2):
    p = xs.shape[0]
    grid_spec = pltpu.PrefetchScalarGridSpec(
        num_scalar_prefetch=2,
        grid=(p // FFN_TILE,),
        in_specs=[pl.BlockSpec((FFN_TILE, D_MODEL), lambda t, te, nu: (jnp.minimum(t, nu[0] - 1), 0)),
                  pl.BlockSpec((1, D_MODEL, D_EXPERT), lambda t, te, nu: (te[t], 0, 0)),
                  pl.BlockSpec((1, D_MODEL, D_EXPERT), lambda t, te, nu: (te[t], 0, 0)),
                  pl.BlockSpec((1, D_EXPERT, D_MODEL), lambda t, te, nu: (te[t], 0, 0))],
        out_specs=pl.BlockSpec((FFN_TILE, D_MODEL), lambda t, te, nu: (t, 0)),
        scratch_shapes=[pltpu.VMEM((D_MODEL, D_EXPERT), BF16), pltpu.VMEM((D_MODEL, D_EXPERT), BF16),
                        pltpu.VMEM((D_EXPERT, D_MODEL), BF16)],
    )
    return pl.pallas_call(
        _ffn_kernel,
        grid_spec=grid_spec,
        out_shape=jax.ShapeDtypeStruct((p, D_MODEL), F32),
        compiler_params=pltpu.CompilerParams(dimension_semantics=("arbitrary",), vmem_limit_bytes=VMEM_LIMIT),
        name="ffn",
    )(tile_expert, n_used, xs, w1, w3, w2)


def _final_kernel(dest_ref, h1_ref, r_ref, g_ref, ys_ref, o_ref, buf_ref, sem):
    def issue(t, c):
        for s in range(2):
            _row_copy(ys_ref, dest_ref[0, 0, 2 * t + s], buf_ref.at[s], t, sem).start(priority=s)
        return c

    def drain(t, c):
        for s in range(2):
            _row_copy(ys_ref, dest_ref[0, 0, 2 * t + s], buf_ref.at[s], t, sem).wait()
        return c

    lax.fori_loop(0, FINAL_TILE, issue, 0, unroll=ROW_DMA_UNROLL)
    lax.fori_loop(0, FINAL_TILE, drain, 0, unroll=ROW_DMA_UNROLL)
    r = r_ref[...]
    moe = r[:, 2:3] * buf_ref[0] + r[:, 3:4] * buf_ref[1]
    o_ref[...] = _rmsnorm(h1_ref[...] + moe, g_ref[...])


def _final(dest3, h1, r, g, ys):
    n = h1.shape[0]
    tm = FINAL_TILE
    return pl.pallas_call(
        _final_kernel,
        grid=(n // tm,),
        in_specs=[pl.BlockSpec((1, 1, 2 * tm), lambda i: (i, 0, 0), memory_space=pltpu.SMEM),
                  pl.BlockSpec((tm, D_MODEL), lambda i: (i, 0)),
                  pl.BlockSpec((tm, LANES), lambda i: (i, 0)),
                  pl.BlockSpec((1, D_MODEL), lambda i: (0, 0)),
                  pl.BlockSpec(memory_space=pl.ANY)],
        out_specs=pl.BlockSpec((tm, D_MODEL), lambda i: (i, 0)),
        out_shape=jax.ShapeDtypeStruct((n, D_MODEL), F32),
        scratch_shapes=[pltpu.VMEM((2, tm, D_MODEL), F32), pltpu.SemaphoreType.DMA],
        compiler_params=pltpu.CompilerParams(dimension_semantics=("arbitrary",), vmem_limit_bytes=VMEM_LIMIT),
        name="final",
    )(dest3, h1, r, g, ys)


def _rope_tables(pos):
    half = ROPE_DIM // 2
    inv = np.float32(ROPE_THETA) ** (-np.arange(half, dtype=np.float32) / np.float32(half))
    ang = np.asarray(pos, np.float32)[:, None] * inv[None, :]
    cos, sin = np.cos(ang).astype(np.float32), np.sin(ang).astype(np.float32)
    d = np.arange(LANES) % HEAD_DIM
    f = d % half
    cos_t = np.where(d < ROPE_DIM, cos[:, f], np.float32(1))
    sa = np.where((d >= half) & (d < ROPE_DIM), sin[:, f], np.float32(0))
    sb = np.where(d < half, -sin[:, f], np.float32(0))
    return tuple(jnp.asarray(t, F32) for t in (cos_t, sa, sb))


def _proj_weights(w_in, w_uk):
    qa, ka, va, qb, ckv, krope, qi, ki, wi, ga, gb = jnp.split(w_in, list(np.cumsum(SPLITS)[:-1]), axis=-1)
    zeros = lambda width: jnp.zeros((D_MODEL, width), w_in.dtype)
    kv = jnp.concatenate([krope, zeros(LANES - ROPE_DIM), ckv], axis=1)
    w = jnp.concatenate([qa * ATTN_SCALE, ka, va, qb, qi, ki, ki, kv, ga, gb], axis=1).astype(BF16)
    assert w.shape[1] == PROJ_W
    nope = HEAD_DIM - ROPE_DIM
    per_head = jnp.concatenate([
        jnp.concatenate([jnp.eye(ROPE_DIM, dtype=F32)[None].repeat(N_HEADS, 0),
                         jnp.zeros((N_HEADS, ROPE_DIM, QF_W - ROPE_DIM), F32)], axis=2),
        jnp.concatenate([jnp.zeros((N_HEADS, nope, LANES), F32), jnp.swapaxes(w_uk, 1, 2)], axis=2)], axis=1)
    head_eye = jnp.eye(N_HEADS, dtype=F32)
    wabs = (per_head[:, :, None, :] * (head_eye * ATTN_SCALE)[:, None, :, None])
    wabs = wabs.reshape(HEADS_W, N_HEADS * QF_W).astype(BF16)
    wwi = jnp.concatenate([wi.T, jnp.zeros((16 - N_HEADS, D_MODEL), w_in.dtype)], axis=0).astype(BF16)
    wckv = ckv.T.astype(BF16)
    return w, wabs, wwi, wckv


def _tri(n, strict_lower):
    r = np.arange(n)
    m = (r[None, :] < r[:, None]) if strict_lower else (r[None, :] <= r[:, None])
    return jnp.asarray(m, BF16)


def kernel(x, meta_tokens, norm_mix_g, w_in, w_uk, w_uv, w_up_a, w_up_b, w_o, norm_ffn_g, w_group, b_group,
           w_router, b_router, w1, w3, w2, norm_final_g):
    batch, seq, d = x.shape
    assert d == D_MODEL and seq % LANES == 0 and norm_mix_g.shape[0] == 1
    n = batch * seq
    nq = seq // LANES
    tp = seq + LANES
    ktop = min(TOPK_MAX, seq // 4)
    x2 = x.reshape(n, d)

    w, wabs, wwi, wckv = _proj_weights(w_in[0], w_uk[0])
    g_mix = norm_mix_g[0][None, :]
    proj_tile = 1024 if seq % 1024 == 0 else LANES
    px = _proj(x2, g_mix, _rope_tables(N_META + np.arange(seq)), w, wabs, wwi, wckv, proj_tile, seq // proj_tile)
    meta = jnp.concatenate([meta_tokens.astype(x.dtype), jnp.zeros((LANES - N_META, d), x.dtype)], axis=0)
    pm = _proj(meta, g_mix, _rope_tables(np.arange(LANES)), w, wabs, wwi, wckv, LANES, 1)
    qa_x, kva_x, qf_x, qi_x, ki_x, kv_x, gt_x, wit_x, ckvt_x = px
    _, kva_m, _, _, ki_m, kv_m, _, _, ckvt_m = pm

    def with_meta(xpart, mpart, rows):
        wdt = xpart.shape[1]
        parts = [jnp.broadcast_to(mpart[None], (batch, LANES, wdt)), xpart.reshape(batch, seq, wdt)]
        if rows > tp:
            parts.append(jnp.zeros((batch, rows - tp, wdt), xpart.dtype))
        return jnp.concatenate(parts, axis=1).reshape(batch * rows, wdt)

    kv_a = with_meta(kva_x, kva_m, tp)
    su = jnp.concatenate([_tri(LANES, True), jnp.ones((LANES, LANES), BF16)], axis=1)
    ya = _sb(qa_x, kv_a, su, batch, seq)

    tpk = -(-tp // DSA_KEYS) * DSA_KEYS
    kip = with_meta(ki_x, ki_m, tpk)
    kvp = with_meta(kv_x, kv_m, tpk)
    ckvt = jnp.concatenate([jnp.broadcast_to(ckvt_m[:, None, :], (KV_LATENT, batch, LANES)),
                            ckvt_x.reshape(KV_LATENT, batch, seq),
                            jnp.zeros((KV_LATENT, batch, tpk - tp), BF16)], axis=2)
    ones_row = jnp.concatenate([jnp.ones((1, batch, tpk), BF16),
                                jnp.zeros((VAL_ROWS - KV_LATENT - 1, batch, tpk), BF16)], axis=0)
    ckvt = jnp.concatenate([ckvt, ones_row], axis=0)
    ckvt3 = ckvt.reshape(VAL_ROWS, batch * tpk // ATT_KEYS, ATT_KEYS).transpose(1, 0, 2)
    wuvt = jnp.swapaxes(w_uv[0], 1, 2).astype(BF16)
    yb = _dsa(qi_x, wit_x, qf_x, kip, kvp, ckvt3, _tri(ATT_KEYS, False), wuvt, batch, seq, ktop)

    wrt = jnp.concatenate([w_router[0], w_group[0],
                           jnp.zeros((d, LANES - N_EXPERTS - N_GROUPS), F32)], axis=1).astype(F32)
    wrt_hi = wrt.astype(BF16)
    wrt = jnp.concatenate([wrt_hi, (wrt - wrt_hi.astype(F32)).astype(BF16)], axis=1)
    brt = jnp.concatenate([b_router[0], b_group[0], jnp.zeros((LANES - N_EXPERTS - N_GROUPS,), F32)])[None, :]
    h1, u2, r, counts = _merge(x2, ya, yb, gt_x, w_up_a[0].astype(BF16), w_up_b[0].astype(BF16),
                               w_o[0].astype(BF16), norm_ffn_g[0][None, :], wrt, brt.astype(F32),
                               _tri(MERGE_TILE, True))

    cnt = counts[0, :N_EXPERTS].astype(jnp.int32)
    padded = (cnt + FFN_TILE - 1) // FFN_TILE * FFN_TILE
    p_end = jnp.cumsum(padded)
    p_start = p_end - padded
    e_sel = r[:, 0:2].astype(jnp.int32)
    first_slot = jnp.sum(jnp.where(e_sel[..., None] == jnp.arange(N_EXPERTS), p_start, 0), axis=-1)
    dest = first_slot + r[:, 4:6].astype(jnp.int32)
    n_tiles = (2 * n) // FFN_TILE + N_EXPERTS
    tile_row = jnp.arange(n_tiles, dtype=jnp.int32) * FFN_TILE
    tile_expert = jnp.minimum(jnp.sum((p_end[None, :] <= tile_row[:, None]).astype(jnp.int32), axis=1),
                              N_EXPERTS - 1)
    n_used = (p_end[-1] // FFN_TILE).astype(jnp.int32)
    tile_expert = jnp.where(jnp.arange(n_tiles) < n_used, tile_expert, tile_expert[jnp.maximum(n_used - 1, 0)])

    xs0 = jnp.zeros((n_tiles * FFN_TILE, d), F32)
    xs = _dispatch(dest.reshape(n // DISPATCH_TILE, 1, 2 * DISPATCH_TILE), u2, xs0)
    ys = _ffn(tile_expert, n_used[None], xs, w1[0], w3[0], w2[0])
    out = _final(dest.reshape(n // FINAL_TILE, 1, 2 * FINAL_TILE), h1, r, norm_final_g[None, :], ys)
    return out.reshape(batch, seq, d)
```

```python
import functools
import math

import numpy as np
import jax
import jax.numpy as jnp
from jax import lax
from jax.experimental import pallas as pl
from jax.experimental.pallas import tpu as pltpu

D_MODEL = 1024
CHUNK = 64
N_META = 16
HEAD_DIM = 64
ROPE_DIM = 16
ROPE_THETA = 500000.0
N_HEADS = 8
KV_LATENT = 128
IDX_DIM = 64
TOPK_MAX = 256
N_GROUPS = 4
EXP_PER_GROUP = 8
N_EXPERTS = 32
D_EXPERT = 512
NORM_EPS = 1e-6
HEADS_W = N_HEADS * HEAD_DIM
SPLITS = (HEADS_W, HEADS_W, HEADS_W, HEADS_W, KV_LATENT, ROPE_DIM, N_HEADS * IDX_DIM, IDX_DIM, N_HEADS,
          D_MODEL, D_MODEL)

LANES = 128
QF_W = 2 * LANES
INT_MIN = -2 ** 31
EXP_UNDERFLOW = -104.0
ATTN_SCALE = 1.0 / math.sqrt(HEAD_DIM)
VMEM_LIMIT = 56 * 1024 * 1024
DSA_SUB = 8
DSA_KEYS = DSA_SUB * LANES
ATT_SUB = 4
ATT_KEYS = ATT_SUB * LANES
CNT_SUB = 2
CNT_KEYS = CNT_SUB * LANES
CNT_GROUP = 4
VAL_ROWS = KV_LATENT + 8
SAFE_LOGIT_BOUND = 40.0
FFN_TILE = 512
MERGE_TILE = 1024
FINAL_TILE = 1024
DISPATCH_TILE = 1024
ROW_DMA_UNROLL = 32

BF16 = jnp.bfloat16
F32 = jnp.float32

_C_QKV = (0, 3 * HEADS_W)
_C_QB = (_C_QKV[1], _C_QKV[1] + HEADS_W)
_C_QI = (_C_QB[1], _C_QB[1] + HEADS_W)
_C_KI = (_C_QI[1], _C_QI[1] + LANES)
_C_KV = (_C_KI[1], _C_KI[1] + QF_W)
_C_G = (_C_KV[1], _C_KV[1] + 2 * D_MODEL)
PROJ_W = _C_G[1]


def _dot(a, b):
    return jnp.dot(a, b, preferred_element_type=F32)


def _dot_t(a, b):
    return lax.dot_general(a, b, (((1,), (1,)), ((), ())), preferred_element_type=F32)


def _rmsnorm(x, g):
    return x * lax.rsqrt(jnp.mean(x * x, axis=-1, keepdims=True) + NORM_EPS) * g


def _rope(x, cos, sa, sb):
    w = x.shape[1]
    n = w // LANES
    if n > 1:
        cos, sa, sb = (jnp.concatenate([t] * n, axis=1) for t in (cos, sa, sb))
    return x * cos + pltpu.roll(x, 8, 1) * sa + pltpu.roll(x, w - 8, 1) * sb


def _proj_kernel(x_ref, g_ref, cos_ref, sa_ref, sb_ref, w_ref, wabs_ref, wwi_ref, wckv_ref,
                 qa_ref, kva_ref, qf_ref, qi_ref, ki_ref, kv_ref, gt_ref, wit_ref, ckvt_ref):
    ub = _rmsnorm(x_ref[...], g_ref[...]).astype(BF16)
    cos, sa, sb = cos_ref[...], sa_ref[...], sb_ref[...]
    qa_ref[...] = _dot(ub, w_ref[:, _C_QKV[0]:_C_QKV[0] + HEADS_W]).astype(BF16)
    kva_ref[...] = _dot(ub, w_ref[:, _C_QKV[0] + HEADS_W:_C_QKV[1]]).astype(BF16)
    qb = _rope(_dot(ub, w_ref[:, _C_QB[0]:_C_QB[1]]), cos, sa, sb).astype(BF16)
    qf_ref[...] = _dot(qb, wabs_ref[...]).astype(BF16)
    qi_ref[...] = _rope(_dot(ub, w_ref[:, _C_QI[0]:_C_QI[1]]), cos, sa, sb).astype(BF16)
    ki_ref[...] = _rope(_dot(ub, w_ref[:, _C_KI[0]:_C_KI[1]]), cos, sa, sb).astype(BF16)
    kv = _dot(ub, w_ref[:, _C_KV[0]:_C_KV[1]])
    kv_ref[...] = jnp.concatenate([_rope(kv[:, :LANES], cos, sa, sb), kv[:, LANES:]], axis=1).astype(BF16)
    gt_ref[...] = jax.nn.sigmoid(_dot(ub, w_ref[:, _C_G[0]:_C_G[1]])).astype(BF16)
    wit_ref[...] = _dot_t(wwi_ref[...], ub)
    ckvt_ref[...] = _dot_t(wckv_ref[...], ub).astype(BF16)


def _proj(x2, g, tabs, w, wabs, wwi, wckv, tile, tab_tiles):
    n = x2.shape[0]
    row = lambda width: pl.BlockSpec((tile, width), lambda i: (i, 0))
    const = lambda shape: pl.BlockSpec(shape, lambda i: (0, 0))
    tab = pl.BlockSpec((tile, LANES), lambda i: (i % tab_tiles, 0))
    out_shape = (
        jax.ShapeDtypeStruct((n, HEADS_W), BF16),
        jax.ShapeDtypeStruct((n, 2 * HEADS_W), BF16),
        jax.ShapeDtypeStruct((n, N_HEADS * QF_W), BF16),
        jax.ShapeDtypeStruct((n, HEADS_W), BF16),
        jax.ShapeDtypeStruct((n, LANES), BF16),
        jax.ShapeDtypeStruct((n, QF_W), BF16),
        jax.ShapeDtypeStruct((n, 2 * D_MODEL), BF16),
        jax.ShapeDtypeStruct((16, n), F32),
        jax.ShapeDtypeStruct((KV_LATENT, n), BF16),
    )
    out_specs = (row(HEADS_W), row(2 * HEADS_W), row(N_HEADS * QF_W), row(HEADS_W), row(LANES), row(QF_W),
                 row(2 * D_MODEL),
                 pl.BlockSpec((16, tile), lambda i: (0, i)), pl.BlockSpec((KV_LATENT, tile), lambda i: (0, i)))
    return pl.pallas_call(
        _proj_kernel,
        grid=(n // tile,),
        in_specs=[row(D_MODEL), const((1, D_MODEL)), tab, tab, tab, const(w.shape), const(wabs.shape),
                  const(wwi.shape), const(wckv.shape)],
        out_specs=out_specs,
        out_shape=out_shape,
        compiler_params=pltpu.CompilerParams(dimension_semantics=("arbitrary",), vmem_limit_bytes=VMEM_LIMIT),
        name="proj",
    )(x2, g, *tabs, w, wabs, wwi, wckv)


def _sb_kernel(q_ref, k_ref, v_ref, su_ref, o_ref, acc_ref, carry_ref):
    ib = pl.program_id(1) + 1
    lane = lax.broadcasted_iota(jnp.int32, (LANES, LANES), 1)
    row = lax.broadcasted_iota(jnp.int32, (LANES, LANES), 0)
    low = lane < HEAD_DIM
    high = jnp.logical_not(low)
    su = su_ref[...]
    qpos = ib * LANES + row
    acc_ref[...] = jnp.zeros_like(acc_ref)
    carry_ref[...] = jnp.zeros_like(carry_ref)

    def body(st):
        j, _ = st
        rows = pl.ds(pl.multiple_of(j * LANES, LANES), LANES)
        kpos = j * LANES + lane
        mask = ((kpos < qpos) & ((kpos < N_META) | (kpos >= LANES)))[None]
        zs = []
        for p in range(N_HEADS // 2):
            cols = slice(p * LANES, (p + 1) * LANES)
            q2, kb = q_ref[:, cols], k_ref[rows, cols]
            zero = jnp.zeros_like(q2)
            for keep in (low, high):
                zs.append(_dot_t(jnp.where(keep, q2, zero), kb))
        z = jnp.stack(zs)
        ls_pos = jnp.minimum(z, 0.0) - jnp.log(1.0 + jnp.exp(-jnp.abs(z)))
        log_keep = jnp.where(mask, ls_pos - z, 0.0)
        hi = log_keep.astype(BF16)
        lo = (log_keep - hi.astype(F32)).astype(BF16)
        flat = (N_HEADS * LANES, LANES)
        cs = _dot(hi.reshape(flat), su) + _dot(lo.reshape(flat), su)
        carry = carry_ref[...]
        a = jnp.where(mask, jnp.exp(ls_pos + cs[:, :LANES].reshape(z.shape) + carry), 0.0).astype(BF16)
        carry = carry + cs[:, LANES:].reshape(z.shape)
        carry_ref[...] = carry
        for p in range(N_HEADS // 2):
            vb = v_ref[rows, p * LANES:(p + 1) * LANES]
            zero = jnp.zeros_like(vb)
            acc_ref[p] += _dot(a[2 * p], jnp.where(low, vb, zero)) + _dot(a[2 * p + 1], jnp.where(high, vb, zero))
        return j - 1, jnp.max(carry)

    def cond(st):
        return (st[0] >= 0) & (st[1] > EXP_UNDERFLOW)

    lax.while_loop(cond, body, (ib, jnp.float32(0.0)))
    for p in range(N_HEADS // 2):
        o_ref[:, p * LANES:(p + 1) * LANES] = acc_ref[p].astype(BF16)


def _sb(qkv_x, kvp, su, batch, seq):
    nq = seq // LANES
    tp = seq + LANES
    return pl.pallas_call(
        _sb_kernel,
        grid=(batch, nq),
        in_specs=[pl.BlockSpec((LANES, HEADS_W), lambda b, i: (b * nq + i, 0)),
                  pl.BlockSpec((tp, HEADS_W), lambda b, i: (b, 0)),
                  pl.BlockSpec((tp, HEADS_W), lambda b, i: (b, 1)),
                  pl.BlockSpec(su.shape, lambda b, i: (0, 0))],
        out_specs=pl.BlockSpec((LANES, HEADS_W), lambda b, i: (b * nq + i, 0)),
        out_shape=jax.ShapeDtypeStruct((batch * seq, HEADS_W), BF16),
        scratch_shapes=[pltpu.VMEM((N_HEADS // 2, LANES, LANES), F32), pltpu.VMEM((N_HEADS, LANES, LANES), F32)],
        compiler_params=pltpu.CompilerParams(dimension_semantics=("arbitrary",) * 2, vmem_limit_bytes=VMEM_LIMIT),
        name="sb",
    )(qkv_x, kvp, kvp, su)


def _dsa_kernel(qi_ref, wit_ref, qf_ref, ki_ref, kv_ref, ckvt_ref, lt_ref, wuvt_ref, o_ref,
                sk_ref, acc_ref, m_ref, kvn_ref, qim_ref, qfp_ref, ybt_ref, *, ktop):
    ib = pl.program_id(1) + 1
    nt = (ib + DSA_SUB) // DSA_SUB
    lane = lax.broadcasted_iota(jnp.int32, (LANES, LANES), 1)
    klane = lax.broadcasted_iota(jnp.int32, (DSA_KEYS, LANES), 1)
    krow = lax.broadcasted_iota(jnp.int32, (DSA_KEYS, LANES), 0)
    low = lane < HEAD_DIM
    pairs = range(N_HEADS // 2)
    for p in pairs:
        blk = qi_ref[:, p * LANES:(p + 1) * LANES]
        zero = jnp.zeros_like(blk)
        qim_ref[2 * p * LANES:(2 * p + 1) * LANES, :] = jnp.where(low, blk, zero)
        qim_ref[(2 * p + 1) * LANES:(2 * p + 2) * LANES, :] = jnp.where(low, zero, blk)
        qfp_ref[2 * p * LANES:(2 * p + 1) * LANES, :] = qf_ref[:, 2 * p * QF_W:(2 * p + 1) * QF_W]
        qfp_ref[(2 * p + 1) * LANES:(2 * p + 2) * LANES, :] = qf_ref[:, (2 * p + 1) * QF_W:(2 * p + 2) * QF_W]
    wit = wit_ref[...]
    wit_all = jnp.concatenate([wit[h:h + 1, :] for h in range(N_HEADS)], axis=1)
    cq = 1 + ((ib * LANES + klane - LANES) >> 6)

    def tile(t):
        return pl.ds(pl.multiple_of(t * DSA_KEYS, DSA_KEYS), DSA_KEYS)

    def scores(t, c):
        kb = ki_ref[tile(t), :]
        part = wit_all * jnp.maximum(_dot_t(kb, qim_ref[...]), 0.0)
        s = part[:, :LANES]
        for h in range(1, N_HEADS):
            s = s + part[:, h * LANES:(h + 1) * LANES]
        s = jnp.where(s == 0.0, 0.0, s)
        bits = lax.bitcast_convert_type(s, jnp.int32)
        key = bits ^ ((bits >> 31) & 0x7FFFFFFF)
        rk = t * DSA_KEYS + krow
        ck = jnp.where(rk < LANES, 0, 1 + ((rk - LANES) >> 6))
        adm = ((rk < N_META) | (rk >= LANES)) & (ck <= cq)
        sk_ref[tile(t), :] = jnp.where(adm, key, INT_MIN)
        return c

    lax.fori_loop(0, nt, scores, 0)

    def count(pred):
        def step(t, acc):
            start = pl.multiple_of(t * CNT_KEYS, CNT_KEYS)
            hit = pred(sk_ref[pl.ds(start, CNT_KEYS), :]).astype(jnp.int32)
            return acc + jnp.sum(hit.reshape(CNT_KEYS // 8, 8, LANES), axis=0)
        ntc = (ib + CNT_SUB) // CNT_SUB

        def group(g, a):
            for u in range(CNT_GROUP):
                a = step(CNT_GROUP * g + u, a)
            return a

        acc = lax.fori_loop(0, ntc // CNT_GROUP, group, jnp.zeros((8, LANES), jnp.int32))
        acc = lax.fori_loop(ntc // CNT_GROUP * CNT_GROUP, ntc, step, acc)
        return jnp.sum(acc, axis=0, keepdims=True)

    zero_row = jnp.zeros((1, LANES), jnp.int32)
    at_zero = count(lambda k: k >= zero_row)
    start = (jnp.where(at_zero >= ktop, zero_row, jnp.full((1, LANES), INT_MIN, jnp.int32)),
             jnp.where(at_zero >= ktop, at_zero, ktop))

    def bit_step(it, st):
        thr, at_thr = st
        cand = thr | jnp.left_shift(jnp.int32(1), 30 - it)
        at_cand = count(lambda k: k >= cand)
        ok = at_cand >= ktop
        return jnp.where(ok, cand, thr), jnp.where(ok, at_cand, at_thr)

    thr, at_thr = lax.fori_loop(0, 31, bit_step, start)
    excess = jnp.where(thr == INT_MIN, 0, at_thr - ktop)

    @pl.when(pl.program_id(1) == 0)
    def _():
        def widest(t, m):
            kvf = kv_ref[tile(t), :].astype(F32)
            return jnp.maximum(m, jnp.max(jnp.sum(kvf * kvf, axis=1, keepdims=True)))
        kvn_ref[...] = jnp.full(kvn_ref.shape, lax.fori_loop(0, kv_ref.shape[0] // DSA_KEYS, widest, jnp.float32(0.0)))

    ones = jnp.ones((8, QF_W), BF16)
    bounds = []
    for p in pairs:
        qp = qfp_ref[2 * p * LANES:(2 * p + 2) * LANES, :].astype(F32)
        qn2 = _dot_t(ones, (qp * qp).astype(BF16))[0:1, :]
        kvn = kvn_ref[0:1, :]
        bounds.append(jnp.sqrt(qn2 * jnp.concatenate([kvn, kvn], axis=1)) * 1.05 + 1e-6)
    bound_max = functools.reduce(jnp.maximum, bounds)
    bound_all = jnp.concatenate(bounds, axis=1)
    fast = (jnp.max(excess) == 0) & (jnp.max(bound_max) <= SAFE_LOGIT_BOUND)
    acc_ref[...] = jnp.zeros_like(acc_ref)

    def twice(a):
        return jnp.concatenate([a, a], axis=1)

    def atile(t):
        return pl.ds(pl.multiple_of(t * ATT_KEYS, ATT_KEYS), ATT_KEYS)

    nta = (ib + ATT_SUB) // ATT_SUB

    @pl.when(fast)
    def _():
        def attend(t, c):
            skt = sk_ref[atile(t), :]
            sel = jnp.concatenate([(skt >= thr) & (skt > INT_MIN)] * N_HEADS, axis=1)
            logits = _dot_t(kv_ref[atile(t), :], qfp_ref[...])
            prob = jnp.where(sel, jnp.exp(logits - bound_all), 0.0).astype(BF16)
            acc_ref[...] += _dot(ckvt_ref[t], prob)
            return c

        lax.fori_loop(0, nta, attend, 0)

    @pl.when(jnp.logical_not(fast))
    def _():
        need = (ktop - count(lambda k: k > thr)).astype(F32)
        m_ref[...] = jnp.full_like(m_ref, -1e29)
        lt = lt_ref[...]

        def attend(t, tie_carry):
            skt = sk_ref[atile(t), :]
            tie = (skt == thr) & (skt > INT_MIN)
            rank = _dot(lt, jnp.where(tie, 1.0, 0.0).astype(BF16)) + tie_carry
            sel = twice((skt > thr) | (tie & (rank <= need)))
            kvb = kv_ref[atile(t), :]
            cb = ckvt_ref[t]
            for p in pairs:
                pc = slice(2 * p * LANES, (2 * p + 2) * LANES)
                s = jnp.where(sel, _dot_t(kvb, qfp_ref[pc, :]), -1e30)
                m_old = m_ref[p:p + 1, :]
                m_new = jnp.maximum(m_old, jnp.max(s, axis=0, keepdims=True))
                prob = jnp.exp(s - m_new).astype(BF16)
                acc_ref[:, pc] = acc_ref[:, pc] * jnp.exp(m_old - m_new) + _dot(cb, prob)
                m_ref[p:p + 1, :] = m_new
            return rank[ATT_KEYS - 1:ATT_KEYS, :]

        lax.fori_loop(0, nta, attend, jnp.zeros((1, LANES), F32))

    for h in range(N_HEADS):
        cols = slice(h * LANES, (h + 1) * LANES)
        o = (acc_ref[:KV_LATENT, cols] / acc_ref[KV_LATENT:KV_LATENT + 1, cols]).astype(BF16)
        ybt_ref[h * HEAD_DIM:(h + 1) * HEAD_DIM, :] = _dot(wuvt_ref[h], o)
    o_ref[...] = ybt_ref[...].T.astype(BF16)


def _dsa(qi, wit, qf, kip, kvp, ckvt3, lt, wuvt, batch, seq, ktop):
    nq = seq // LANES
    tpk = kip.shape[0] // batch
    ntk = tpk // ATT_KEYS
    return pl.pallas_call(
        functools.partial(_dsa_kernel, ktop=ktop),
        grid=(batch, nq),
        in_specs=[pl.BlockSpec((LANES, HEADS_W), lambda b, i: (b * nq + i, 0)),
                  pl.BlockSpec((16, LANES), lambda b, i: (0, b * nq + i)),
                  pl.BlockSpec((LANES, N_HEADS * QF_W), lambda b, i: (b * nq + i, 0)),
                  pl.BlockSpec((tpk, LANES), lambda b, i: (b, 0)),
                  pl.BlockSpec((tpk, QF_W), lambda b, i: (b, 0)),
                  pl.BlockSpec((ntk, VAL_ROWS, ATT_KEYS), lambda b, i: (b, 0, 0)),
                  pl.BlockSpec(lt.shape, lambda b, i: (0, 0)),
                  pl.BlockSpec(wuvt.shape, lambda b, i: (0, 0, 0))],
        out_specs=pl.BlockSpec((LANES, HEADS_W), lambda b, i: (b * nq + i, 0)),
        out_shape=jax.ShapeDtypeStruct((batch * seq, HEADS_W), BF16),
        scratch_shapes=[pltpu.VMEM((tpk, LANES), jnp.int32),
                        pltpu.VMEM((VAL_ROWS, N_HEADS * LANES), F32),
                        pltpu.VMEM((8, 2 * LANES), F32),
                        pltpu.VMEM((8, LANES), F32),
                        pltpu.VMEM((N_HEADS * LANES, LANES), BF16),
                        pltpu.VMEM((N_HEADS * LANES, QF_W), BF16),
                        pltpu.VMEM((HEADS_W, LANES), F32)],
        compiler_params=pltpu.CompilerParams(dimension_semantics=("arbitrary",) * 2, vmem_limit_bytes=VMEM_LIMIT),
        name="dsa",
    )(qi, wit, qf, kip, kvp, ckvt3, lt, wuvt)


def _merge_kernel(x_ref, ya_ref, yb_ref, gt_ref, wua_ref, wub_ref, wo_ref, g_ref, wrt_ref, brt_ref, slt_ref,
                  h1_ref, u2_ref, r_ref, cnt_ref, carry_ref):
    @pl.when(pl.program_id(0) == 0)
    def _():
        carry_ref[...] = jnp.zeros_like(carry_ref)

    gt = gt_ref[...].astype(F32)
    z = gt[:, :D_MODEL] * _dot(ya_ref[...], wua_ref[...]) + gt[:, D_MODEL:] * _dot(yb_ref[...], wub_ref[...])
    h1 = x_ref[...] + _dot(z.astype(BF16), wo_ref[...])
    h1_ref[...] = h1
    u2 = _rmsnorm(h1, g_ref[...])
    u2_ref[...] = u2
    u_hi = u2.astype(BF16)
    u_lo = (u2 - u_hi.astype(F32)).astype(BF16)
    hi_terms = _dot(u_hi, wrt_ref[...])
    lg = hi_terms[:, :LANES] + hi_terms[:, LANES:] + _dot(u_lo, wrt_ref[:, :LANES]) + brt_ref[...]

    tm = lg.shape[0]
    lane = lax.broadcasted_iota(jnp.int32, (tm, LANES), 1)
    big = jnp.int32(LANES)

    def softmax(mask):
        x = jnp.where(mask, lg, -jnp.inf)
        e = jnp.where(mask, jnp.exp(x - jnp.max(x, axis=-1, keepdims=True)), 0.0)
        return e / jnp.sum(e, axis=-1, keepdims=True)

    def top1(p, mask):
        best = jnp.max(jnp.where(mask, p, -1.0), axis=-1, keepdims=True)
        idx = jnp.min(jnp.where(mask & (p == best), lane, big), axis=-1, keepdims=True)
        return best, idx

    gmask = (lane >= N_EXPERTS) & (lane < N_EXPERTS + N_GROUPS)
    p_grp, g_lane = top1(softmax(gmask), gmask)
    first = (g_lane - N_EXPERTS) * EXP_PER_GROUP
    emask = (lane >= first) & (lane < first + EXP_PER_GROUP)
    ep = softmax(emask)
    p0, e0 = top1(ep, emask)
    rest = emask & (lane != e0)
    p1, e1 = top1(ep, rest)
    den = p0 + p1
    w0 = p_grp * p0 / den
    w1 = p_grp * p1 / den

    oh0 = lane == e0
    oh1 = lane == e1
    both = (oh0 | oh1).astype(BF16)
    before = _dot(slt_ref[...], both) + carry_ref[...]
    rank0 = jnp.sum(jnp.where(oh0, before, 0.0), axis=-1, keepdims=True)
    rank1 = jnp.sum(jnp.where(oh1, before, 0.0), axis=-1, keepdims=True)
    carry = carry_ref[...] + jnp.sum(both.astype(F32), axis=0, keepdims=True)
    carry_ref[...] = carry
    cnt_ref[...] = carry

    cols = (e0.astype(F32), e1.astype(F32), w0, w1, rank0, rank1)
    r = jnp.zeros((tm, LANES), F32)
    for c, v in enumerate(cols):
        r = jnp.where(lane == c, v, r)
    r_ref[...] = r


def _merge(x2, ya, yb, gt, wua, wub, wo, g, wrt, brt, slt):
    n = x2.shape[0]
    tm = MERGE_TILE
    row = lambda width: pl.BlockSpec((tm, width), lambda i: (i, 0))
    const = lambda a: pl.BlockSpec(a.shape, lambda i: (0,) * a.ndim)
    return pl.pallas_call(
        _merge_kernel,
        grid=(n // tm,),
        in_specs=[row(D_MODEL), row(HEADS_W), row(HEADS_W), row(2 * D_MODEL), const(wua), const(wub), const(wo),
                  const(g), const(wrt), const(brt), const(slt)],
        out_specs=(row(D_MODEL), row(D_MODEL), row(LANES), pl.BlockSpec((1, LANES), lambda i: (0, 0))),
        out_shape=(jax.ShapeDtypeStruct((n, D_MODEL), F32), jax.ShapeDtypeStruct((n, D_MODEL), F32),
                   jax.ShapeDtypeStruct((n, LANES), F32), jax.ShapeDtypeStruct((1, LANES), F32)),
        scratch_shapes=[pltpu.VMEM((1, LANES), F32)],
        compiler_params=pltpu.CompilerParams(dimension_semantics=("arbitrary",), vmem_limit_bytes=VMEM_LIMIT),
        name="merge",
    )(x2, ya, yb, gt, wua, wub, wo, g, wrt, brt, slt)


def _row_copy(src, src_row, dst, dst_row, sem):
    return pltpu.make_async_copy(src.at[pl.ds(src_row, 1)], dst.at[pl.ds(dst_row, 1)], sem)


def _dispatch_kernel(dest_ref, u_ref, xs_in_ref, xs_ref, sem):
    del xs_in_ref

    def issue(t, c):
        for s in range(2):
            _row_copy(u_ref, t, xs_ref, dest_ref[0, 0, 2 * t + s], sem).start(priority=s)
        return c

    def drain(t, c):
        for s in range(2):
            _row_copy(u_ref, t, xs_ref, dest_ref[0, 0, 2 * t + s], sem).wait()
        return c

    lax.fori_loop(0, DISPATCH_TILE, issue, 0, unroll=ROW_DMA_UNROLL)
    lax.fori_loop(0, DISPATCH_TILE, drain, 0, unroll=ROW_DMA_UNROLL)


def _dispatch(dest3, u2, xs0):
    n = u2.shape[0]
    return pl.pallas_call(
        _dispatch_kernel,
        grid=(n // DISPATCH_TILE,),
        in_specs=[pl.BlockSpec((1, 1, 2 * DISPATCH_TILE), lambda i: (i, 0, 0), memory_space=pltpu.SMEM),
                  pl.BlockSpec((DISPATCH_TILE, D_MODEL), lambda i: (i, 0)), pl.BlockSpec(memory_space=pl.ANY)],
        out_specs=pl.BlockSpec(memory_space=pl.ANY),
        out_shape=jax.ShapeDtypeStruct(xs0.shape, xs0.dtype),
        scratch_shapes=[pltpu.SemaphoreType.DMA],
        input_output_aliases={2: 0},
        compiler_params=pltpu.CompilerParams(dimension_semantics=("arbitrary",), has_side_effects=True),
        name="dispatch",
    )(dest3, u2, xs0)


def _ffn_kernel(te_ref, nu_ref, xs_ref, w1_ref, w3_ref, w2_ref, ys_ref, w1b_ref, w3b_ref, w2b_ref):
    t = pl.program_id(0)

    @pl.when((t == 0) | (te_ref[t] != te_ref[jnp.maximum(t - 1, 0)]))
    def _():
        w1b_ref[...] = w1_ref[0].astype(BF16)
        w3b_ref[...] = w3_ref[0].astype(BF16)
        w2b_ref[...] = w2_ref[0].astype(BF16)

    @pl.when(t < nu_ref[0])
    def _():
        x = xs_ref[...].astype(BF16)
        hid = jax.nn.silu(_dot(x, w1b_ref[...])) * _dot(x, w3b_ref[...])
        ys_ref[...] = _dot(hid.astype(BF16), w2b_ref[...])

    @pl.when(t >= nu_ref[0])
    def _():
        ys_ref[...] = jnp.zeros_like(ys_ref)


def _ffn(tile_expert, n_used, xs, w1, w3, w2):
    p = xs.shape[0]
    grid_spec = pltpu.PrefetchScalarGridSpec(
        num_scalar_prefetch=2,
        grid=(p // FFN_TILE,),
        in_specs=[pl.BlockSpec((FFN_TILE, D_MODEL), lambda t, te, nu: (jnp.minimum(t, nu[0] - 1), 0)),
                  pl.BlockSpec((1, D_MODEL, D_EXPERT), lambda t, te, nu: (te[t], 0, 0)),
                  pl.BlockSpec((1, D_MODEL, D_EXPERT), lambda t, te, nu: (te[t], 0, 0)),
                  pl.BlockSpec((1, D_EXPERT, D_MODEL), lambda t, te, nu: (te[t], 0, 0))],
        out_specs=pl.BlockSpec((FFN_TILE, D_MODEL), lambda t, te, nu: (t, 0)),
        scratch_shapes=[pltpu.VMEM((D_MODEL, D_EXPERT), BF16), pltpu.VMEM((D_MODEL, D_EXPERT), BF16),
                        pltpu.VMEM((D_EXPERT, D_MODEL), BF16)],
    )
    return pl.pallas_call(
        _ffn_kernel,
        grid_spec=grid_spec,
        out_shape=jax.ShapeDtypeStruct((p, D_MODEL), F32),
        compiler_params=pltpu.CompilerParams(dimension_semantics=("arbitrary",), vmem_limit_bytes=VMEM_LIMIT),
        name="ffn",
    )(tile_expert, n_used, xs, w1, w3, w2)


def _final_kernel(dest_ref, h1_ref, r_ref, g_ref, ys_ref, o_ref, buf_ref, sem):
    def issue(t, c):
        for s in range(2):
            _row_copy(ys_ref, dest_ref[0, 0, 2 * t + s], buf_ref.at[s], t, sem).start(priority=s)
        return c

    def drain(t, c):
        for s in range(2):
            _row_copy(ys_ref, dest_ref[0, 0, 2 * t + s], buf_ref.at[s], t, sem).wait()
        return c

    lax.fori_loop(0, FINAL_TILE, issue, 0, unroll=ROW_DMA_UNROLL)
    lax.fori_loop(0, FINAL_TILE, drain, 0, unroll=ROW_DMA_UNROLL)
    r = r_ref[...]
    moe = r[:, 2:3] * buf_ref[0] + r[:, 3:4] * buf_ref[1]
    o_ref[...] = _rmsnorm(h1_ref[...] + moe, g_ref[...])


def _final(dest3, h1, r, g, ys):
    n = h1.shape[0]
    tm = FINAL_TILE
    return pl.pallas_call(
        _final_kernel,
        grid=(n // tm,),
        in_specs=[pl.BlockSpec((1, 1, 2 * tm), lambda i: (i, 0, 0), memory_space=pltpu.SMEM),
                  pl.BlockSpec((tm, D_MODEL), lambda i: (i, 0)),
                  pl.BlockSpec((tm, LANES), lambda i: (i, 0)),
                  pl.BlockSpec((1, D_MODEL), lambda i: (0, 0)),
                  pl.BlockSpec(memory_space=pl.ANY)],
        out_specs=pl.BlockSpec((tm, D_MODEL), lambda i: (i, 0)),
        out_shape=jax.ShapeDtypeStruct((n, D_MODEL), F32),
        scratch_shapes=[pltpu.VMEM((2, tm, D_MODEL), F32), pltpu.SemaphoreType.DMA],
        compiler_params=pltpu.CompilerParams(dimension_semantics=("arbitrary",), vmem_limit_bytes=VMEM_LIMIT),
        name="final",
    )(dest3, h1, r, g, ys)


def _rope_tables(pos):
    half = ROPE_DIM // 2
    inv = np.float32(ROPE_THETA) ** (-np.arange(half, dtype=np.float32) / np.float32(half))
    ang = np.asarray(pos, np.float32)[:, None] * inv[None, :]
    cos, sin = np.cos(ang).astype(np.float32), np.sin(ang).astype(np.float32)
    d = np.arange(LANES) % HEAD_DIM
    f = d % half
    cos_t = np.where(d < ROPE_DIM, cos[:, f], np.float32(1))
    sa = np.where((d >= half) & (d < ROPE_DIM), sin[:, f], np.float32(0))
    sb = np.where(d < half, -sin[:, f], np.float32(0))
    return tuple(jnp.asarray(t, F32) for t in (cos_t, sa, sb))


def _proj_weights(w_in, w_uk):
    qa, ka, va, qb, ckv, krope, qi, ki, wi, ga, gb = jnp.split(w_in, list(np.cumsum(SPLITS)[:-1]), axis=-1)
    zeros = lambda width: jnp.zeros((D_MODEL, width), w_in.dtype)
    kv = jnp.concatenate([krope, zeros(LANES - ROPE_DIM), ckv], axis=1)
    w = jnp.concatenate([qa * ATTN_SCALE, ka, va, qb, qi, ki, ki, kv, ga, gb], axis=1).astype(BF16)
    assert w.shape[1] == PROJ_W
    nope = HEAD_DIM - ROPE_DIM
    per_head = jnp.concatenate([
        jnp.concatenate([jnp.eye(ROPE_DIM, dtype=F32)[None].repeat(N_HEADS, 0),
                         jnp.zeros((N_HEADS, ROPE_DIM, QF_W - ROPE_DIM), F32)], axis=2),
        jnp.concatenate([jnp.zeros((N_HEADS, nope, LANES), F32), jnp.swapaxes(w_uk, 1, 2)], axis=2)], axis=1)
    head_eye = jnp.eye(N_HEADS, dtype=F32)
    wabs = (per_head[:, :, None, :] * (head_eye * ATTN_SCALE)[:, None, :, None])
    wabs = wabs.reshape(HEADS_W, N_HEADS * QF_W).astype(BF16)
    wwi = jnp.concatenate([wi.T, jnp.zeros((16 - N_HEADS, D_MODEL), w_in.dtype)], axis=0).astype(BF16)
    wckv = ckv.T.astype(BF16)
    return w, wabs, wwi, wckv


def _tri(n, strict_lower):
    r = np.arange(n)
    m = (r[None, :] < r[:, None]) if strict_lower else (r[None, :] <= r[:, None])
    return jnp.asarray(m, BF16)


def kernel(x, meta_tokens, norm_mix_g, w_in, w_uk, w_uv, w_up_a, w_up_b, w_o, norm_ffn_g, w_group, b_group,
           w_router, b_router, w1, w3, w2, norm_final_g):
    batch, seq, d = x.shape
    assert d == D_MODEL and seq % LANES == 0 and norm_mix_g.shape[0] == 1
    n = batch * seq
    nq = seq // LANES
    tp = seq + LANES
    ktop = min(TOPK_MAX, seq // 4)
    x2 = x.reshape(n, d)

    w, wabs, wwi, wckv = _proj_weights(w_in[0], w_uk[0])
    g_mix = norm_mix_g[0][None, :]
    proj_tile = 1024 if seq % 1024 == 0 else LANES
    px = _proj(x2, g_mix, _rope_tables(N_META + np.arange(seq)), w, wabs, wwi, wckv, proj_tile, seq // proj_tile)
    meta = jnp.concatenate([meta_tokens.astype(x.dtype), jnp.zeros((LANES - N_META, d), x.dtype)], axis=0)
    pm = _proj(meta, g_mix, _rope_tables(np.arange(LANES)), w, wabs, wwi, wckv, LANES, 1)
    qa_x, kva_x, qf_x, qi_x, ki_x, kv_x, gt_x, wit_x, ckvt_x = px
    _, kva_m, _, _, ki_m, kv_m, _, _, ckvt_m = pm

    def with_meta(xpart, mpart, rows):
        wdt = xpart.shape[1]
        parts = [jnp.broadcast_to(mpart[None], (batch, LANES, wdt)), xpart.reshape(batch, seq, wdt)]
        if rows > tp:
            parts.append(jnp.zeros((batch, rows - tp, wdt), xpart.dtype))
        return jnp.concatenate(parts, axis=1).reshape(batch * rows, wdt)

    kv_a = with_meta(kva_x, kva_m, tp)
    su = jnp.concatenate([_tri(LANES, True), jnp.ones((LANES, LANES), BF16)], axis=1)
    ya = _sb(qa_x, kv_a, su, batch, seq)

    tpk = -(-tp // DSA_KEYS) * DSA_KEYS
    kip = with_meta(ki_x, ki_m, tpk)
    kvp = with_meta(kv_x, kv_m, tpk)
    ckvt = jnp.concatenate([jnp.broadcast_to(ckvt_m[:, None, :], (KV_LATENT, batch, LANES)),
                            ckvt_x.reshape(KV_LATENT, batch, seq),
                            jnp.zeros((KV_LATENT, batch, tpk - tp), BF16)], axis=2)
    ones_row = jnp.concatenate([jnp.ones((1, batch, tpk), BF16),
                                jnp.zeros((VAL_ROWS - KV_LATENT - 1, batch, tpk), BF16)], axis=0)
    ckvt = jnp.concatenate([ckvt, ones_row], axis=0)
    ckvt3 = ckvt.reshape(VAL_ROWS, batch * tpk // ATT_KEYS, ATT_KEYS).transpose(1, 0, 2)
    wuvt = jnp.swapaxes(w_uv[0], 1, 2).astype(BF16)
    yb = _dsa(qi_x, wit_x, qf_x, kip, kvp, ckvt3, _tri(ATT_KEYS, False), wuvt, batch, seq, ktop)

    wrt = jnp.concatenate([w_router[0], w_group[0],
                           jnp.zeros((d, LANES - N_EXPERTS - N_GROUPS), F32)], axis=1).astype(F32)
    wrt_hi = wrt.astype(BF16)
    wrt = jnp.concatenate([wrt_hi, (wrt - wrt_hi.astype(F32)).astype(BF16)], axis=1)
    brt = jnp.concatenate([b_router[0], b_group[0], jnp.zeros((LANES - N_EXPERTS - N_GROUPS,), F32)])[None, :]
    h1, u2, r, counts = _merge(x2, ya, yb, gt_x, w_up_a[0].astype(BF16), w_up_b[0].astype(BF16),
                               w_o[0].astype(BF16), norm_ffn_g[0][None, :], wrt, brt.astype(F32),
                               _tri(MERGE_TILE, True))

    cnt = counts[0, :N_EXPERTS].astype(jnp.int32)
    padded = (cnt + FFN_TILE - 1) // FFN_TILE * FFN_TILE
    p_end = jnp.cumsum(padded)
    p_start = p_end - padded
    e_sel = r[:, 0:2].astype(jnp.int32)
    first_slot = jnp.sum(jnp.where(e_sel[..., None] == jnp.arange(N_EXPERTS), p_start, 0), axis=-1)
    dest = first_slot + r[:, 4:6].astype(jnp.int32)
    n_tiles = (2 * n) // FFN_TILE + N_EXPERTS
    tile_row = jnp.arange(n_tiles, dtype=jnp.int32) * FFN_TILE
    tile_expert = jnp.minimum(jnp.sum((p_end[None, :] <= tile_row[:, None]).astype(jnp.int32), axis=1),
                              N_EXPERTS - 1)
    n_used = (p_end[-1] // FFN_TILE).astype(jnp.int32)
    tile_expert = jnp.where(jnp.arange(n_tiles) < n_used, tile_expert, tile_expert[jnp.maximum(n_used - 1, 0)])

    xs0 = jnp.zeros((n_tiles * FFN_TILE, d), F32)
    xs = _dispatch(dest.reshape(n // DISPATCH_TILE, 1, 2 * DISPATCH_TILE), u2, xs0)
    ys = _ffn(tile_expert, n_used[None], xs, w1[0], w3[0], w2[0])
    out = _final(dest.reshape(n // FINAL_TILE, 1, 2 * FINAL_TILE), h1, r, norm_final_g[None, :], ys)
    return out.reshape(batch, seq, d)
```
